```python
import math
import jax
import jax.numpy as jnp
from jax import lax
import numpy as np

D_MODEL = 4096
BATCH = 2
SEQ = 8192
DEPTH = 1
DEC_BATCH = 32
DEC_SEQ = 32
PAST_LEN = 2048

CHUNK = 64
EPS = 1e-6
M_HEADS = 8
M_QK = D_MODEL // 32
M_V = D_MODEL // 16
M_QK_W = M_HEADS * M_QK
M_V_W = M_HEADS * M_V
CONV_W = 4
A_HEADS = 8
A_DH = D_MODEL // 32
A_V = 2 * A_DH
A_QK_W = A_HEADS * 2 * A_DH
A_V_W = A_HEADS * A_V
Q_BLOCK = 128
P_HEADS = 8
P_KEYS = 128
P_EXPERTS = P_KEYS * P_KEYS
P_DK = 256
P_TOPK = 16
P_TOK_BLOCK = 64
SPLITS = (2 * M_QK_W, M_V_W, M_V_W, M_HEADS, M_HEADS, A_QK_W, A_QK_W, A_V_W, 2 * D_MODEL)
D_IN = sum(SPLITS)

kernel_name = 'hybrid_mlstm_diffattn_peer_stream_step'


def _rms(x, g):
    xf = x.astype(jnp.float32)
    y = xf * lax.rsqrt(jnp.mean(xf * xf, -1, keepdims=True) + EPS)
    return (y * g.astype(jnp.float32)).astype(x.dtype)


def _head_rms(h, g):
    return h * lax.rsqrt(jnp.mean(h * h, -1, keepdims=True) + EPS) * g.astype(jnp.float32)


def _causal_conv(u, buf, w, b):
    T = u.shape[1]
    full = jnp.concatenate([buf.astype(u.dtype), u], axis=1)
    out = b + full[:, 0:T] * w[0]
    for j in range(1, CONV_W):
        out = out + full[:, j:j + T] * w[j]
    return out, full[:, -(CONV_W - 1):]


def _mlstm(q, k, v, i_pre, f_pre, C0, n0, m0):
    B, T, H, _ = q.shape
    L = min(CHUNK, T)
    nc = T // L

    def chunks(a):
        a = a.reshape((B, nc, L) + a.shape[2:]).swapaxes(0, 1)
        return jnp.moveaxis(a, 3, 2)

    logf = jax.nn.log_sigmoid(f_pre)
    tri = jnp.tril(jnp.ones((L, L), bool))

    def step(carry, inp):
        C, n, m = carry
        qc, kc, vc, ic, fc = inp
        b = jnp.cumsum(fc, axis=-1)
        dmat = jnp.where(tri, b[..., :, None] - b[..., None, :] + ic[..., None, :], -jnp.inf)
        inter = b + m[..., None]
        m_t = jnp.maximum(inter, jnp.max(dmat, -1))
        s = jnp.einsum('bhtd,bhsd->bhts', qc, kc) * jnp.exp(dmat - m_t[..., None])
        w_prev = jnp.exp(inter - m_t)
        num = jnp.einsum('bhts,bhsv->bhtv', s, vc) + w_prev[..., None] * jnp.einsum('bhvd,bhtd->bhtv', C, qc)
        den = jnp.sum(s, -1) + w_prev * jnp.einsum('bhd,bhtd->bht', n, qc)
        h = num / jnp.maximum(jnp.abs(den), jnp.exp(-m_t))[..., None]
        m_new = m_t[..., -1]
        decay = jnp.exp(b[..., -1] + m - m_new)
        w_src = jnp.exp(b[..., -1:] - b + ic - m_new[..., None])
        C_new = decay[..., None, None] * C + jnp.einsum('bhs,bhsv,bhsd->bhvd', w_src, vc, kc)
        n_new = decay[..., None] * n + jnp.einsum('bhs,bhsd->bhd', w_src, kc)
        return (C_new, n_new, m_new), h

    (C, n, m), hs = lax.scan(step, (C0, n0, m0),
                             (chunks(q), chunks(k), chunks(v), chunks(i_pre), chunks(logf)))
    h = jnp.moveaxis(hs, 2, 3).swapaxes(0, 1).reshape(B, T, H, -1)
    return h, C, n, m


def _alibi_slopes():
    return jnp.array([2.0 ** (-8.0 * (h + 1) / A_HEADS) for h in range(A_HEADS)], jnp.float32)


def _diff_attn(q, k, v, q_pos, k_pos, lam, slopes):
    B, Tq = q.shape[:2]
    QB = min(Q_BLOCK, Tq)
    nb = Tq // QB
    k_chunk = k_pos // CHUNK
    kf = k_pos.astype(jnp.float32)

    def block(args):
        qb, pb = args
        s = jnp.einsum('bqhcd,bkhcd->bhcqk', qb, k) * (A_DH ** -0.5)
        bias = -slopes[:, None, None] * jnp.abs(pb.astype(jnp.float32)[:, None] - kf[None, :])
        vis = k_chunk[None, :] <= (pb // CHUNK)[:, None]
        s = jnp.where(vis, s + bias[None, :, None], -jnp.inf)
        a = jax.nn.softmax(s, axis=-1)
        w = a[:, :, 0] - lam * a[:, :, 1]
        return jnp.einsum('bhqk,bkhv->bqhv', w, v)

    qs = q.reshape((B, nb, QB) + q.shape[2:]).swapaxes(0, 1)
    out = lax.map(block, (qs, q_pos.reshape(nb, QB)))
    return out.swapaxes(0, 1).reshape(B, Tq, A_HEADS, A_V)


def _peer(h, w_pq, sub_keys, u_tab, v_tab):
    B, T, D = h.shape
    N = B * T
    pad = (-N) % P_TOK_BLOCK
    xt = jnp.pad(h.reshape(N, D), ((0, pad), (0, 0)))

    def block(xb):
        q = (xb @ w_pq).reshape(-1, P_HEADS, 2, P_DK // 2).astype(jnp.float32)
        s = jnp.einsum('phcd,hckd->phck', q, sub_keys.astype(jnp.float32))
        sv, si = lax.top_k(s, P_TOPK)
        cand = (sv[:, :, 0, :, None] + sv[:, :, 1, None, :]).reshape(-1, P_HEADS, P_TOPK * P_TOPK)
        cid = (si[:, :, 0, :, None] * P_KEYS + si[:, :, 1, None, :]).reshape(-1, P_HEADS, P_TOPK * P_TOPK)
        fv, fi = lax.top_k(cand, P_TOPK)
        eid = jnp.take_along_axis(cid, fi, axis=-1)
        g = jax.nn.softmax(fv, axis=-1)
        a = jax.nn.gelu(jnp.einsum('phkd,pd->phk', u_tab[eid], xb).astype(jnp.float32))
        return jnp.einsum('phk,phkd->pd', (g * a).astype(xb.dtype), v_tab[eid])

    out = lax.map(block, xt.reshape(-1, P_TOK_BLOCK, D))
    return out.reshape(-1, D)[:N].reshape(B, T, D)


def _layer(l, x, c, k_past, v_past, C0, n0, m0, conv0, p):
    B, T, _ = x.shape
    past = k_past.shape[1]
    f32 = jnp.float32
    mod = (jax.nn.silu(c) @ p['w_ada'][l] + p['b_ada'][l]).reshape(B, 6, 1, D_MODEL)
    sh1, sc1, gt1, sh2, sc2, gt2 = [mod[:, i] for i in range(6)]
    hn = _rms(x, p['g_mix'][l]) * (1 + sc1) + sh1
    proj = hn @ p['w_in'][l]
    qk_raw, mv, mo, mi, mf, aq, ak, av, gates = jnp.split(proj, np.cumsum(SPLITS)[:-1].tolist(), axis=-1)

    qk, conv_new = _causal_conv(qk_raw, conv0, p['conv_w'][l], p['conv_b'][l])
    qk = jax.nn.silu(qk).astype(f32)
    mq = qk[..., :M_QK_W].reshape(B, T, M_HEADS, M_QK)
    mk = qk[..., M_QK_W:].reshape(B, T, M_HEADS, M_QK) * (M_QK ** -0.5)
    mvv = mv.astype(f32).reshape(B, T, M_HEADS, M_V)
    i_pre = mi.astype(f32) + p['b_gates'][l, 0].astype(f32)
    f_pre = mf.astype(f32) + p['b_gates'][l, 1].astype(f32)
    hm, C1, n1, m1 = _mlstm(mq, mk, mvv, i_pre, f_pre, C0.astype(f32), n0.astype(f32), m0.astype(f32))
    hm = (_head_rms(hm, p['m_norm_g'][l]).reshape(B, T, M_V_W) * jax.nn.sigmoid(mo.astype(f32))).astype(x.dtype)
    y_a = hm @ p['w_a'][l]

    lam_init = 0.8 - 0.6 * math.exp(-0.3 * l)
    lam = (jnp.exp(jnp.sum(p['lam_q1'][l].astype(f32) * p['lam_k1'][l].astype(f32)))
           - jnp.exp(jnp.sum(p['lam_q2'][l].astype(f32) * p['lam_k2'][l].astype(f32))) + lam_init)
    k_new = ak.reshape(B, T, A_HEADS, 2, A_DH)
    v_new = av.reshape(B, T, A_HEADS, A_V)
    k_all = jnp.concatenate([k_past.astype(x.dtype), k_new], axis=1).astype(f32)
    v_all = jnp.concatenate([v_past.astype(x.dtype), v_new], axis=1).astype(f32)
    q_pos = past + jnp.arange(T, dtype=jnp.int32)
    k_pos = jnp.arange(past + T, dtype=jnp.int32)
    ha = _diff_attn(aq.reshape(B, T, A_HEADS, 2, A_DH).astype(f32), k_all, v_all, q_pos, k_pos, lam, _alibi_slopes())
    ha = (_head_rms(ha, p['a_norm_g'][l]) * (1.0 - lam_init)).reshape(B, T, A_V_W).astype(x.dtype)
    y_b = ha @ p['w_b'][l]

    g = jax.nn.sigmoid(gates.astype(f32)).reshape(B, T, 2, D_MODEL)
    merged = (g[..., 0, :] * y_a + g[..., 1, :] * y_b).astype(x.dtype)
    x = x + gt1 * (merged @ p['w_out'][l])

    hn2 = _rms(x, p['g_ffn'][l]) * (1 + sc2) + sh2
    x = x + gt2 * _peer(hn2, p['w_pq'][l], p['sub_keys'][l], p['u_tab'][l], p['v_tab'][l])
    return x, (k_new, v_new, C1, n1, m1, conv_new)


def setup_inputs(seed: int = 0) -> dict:
    key = jax.random.key(seed)
    ks = jax.random.split(key, 40)
    f32 = jnp.float32

    def nrm(k, shape, s):
        return jax.random.normal(k, shape, f32) * s

    b_gates = nrm(ks[20], (DEPTH, 2, M_HEADS), 0.1) + jnp.array([0.0, 1.0], f32)[:, None] * jnp.linspace(3.0, 6.0, M_HEADS, dtype=f32)[None, :]
    return {
        'x_prompt': nrm(ks[0], (BATCH, SEQ, D_MODEL), 1.0),
        'x_sample': nrm(ks[1], (DEC_BATCH, DEC_SEQ, D_MODEL), 1.0),
        'cache_k': nrm(ks[2], (DEPTH, DEC_BATCH, PAST_LEN, A_HEADS, 2, A_DH), 1.0),
        'cache_v': nrm(ks[3], (DEPTH, DEC_BATCH, PAST_LEN, A_HEADS, A_V), 1.0),
        'state_C': nrm(ks[4], (DEPTH, DEC_BATCH, M_HEADS, M_V, M_QK), 0.1),
        'state_n': nrm(ks[5], (DEPTH, DEC_BATCH, M_HEADS, M_QK), 0.1),
        'state_m': nrm(ks[6], (DEPTH, DEC_BATCH, M_HEADS), 0.5),
        'state_conv': nrm(ks[7], (DEPTH, DEC_BATCH, CONV_W - 1, 2 * M_QK_W), 1.0),
        'c_prompt': nrm(ks[8], (BATCH, D_MODEL), 1.0),
        'c_sample': nrm(ks[9], (DEC_BATCH, D_MODEL), 1.0),
        'w_ada': nrm(ks[10], (DEPTH, D_MODEL, 6 * D_MODEL), 0.5 * D_MODEL ** -0.5),
        'b_ada': nrm(ks[11], (DEPTH, 6 * D_MODEL), 0.02),
        'g_mix': 1.0 + nrm(ks[12], (DEPTH, D_MODEL), 0.02),
        'w_in': nrm(ks[13], (DEPTH, D_MODEL, D_IN), D_MODEL ** -0.5),
        'conv_w': nrm(ks[14], (DEPTH, CONV_W, 2 * M_QK_W), 0.5),
        'conv_b': nrm(ks[15], (DEPTH, 2 * M_QK_W), 0.02),
        'b_gates': b_gates,
        'm_norm_g': 1.0 + nrm(ks[16], (DEPTH, M_HEADS, M_V), 0.02),
        'lam_q1': nrm(ks[17], (DEPTH, A_DH), 0.1),
        'lam_k1': nrm(ks[18], (DEPTH, A_DH), 0.1),
        'lam_q2': nrm(ks[19], (DEPTH, A_DH), 0.1),
        'lam_k2': nrm(ks[21], (DEPTH, A_DH), 0.1),
        'a_norm_g': 1.0 + nrm(ks[22], (DEPTH, A_HEADS, A_V), 0.02),
        'w_a': nrm(ks[23], (DEPTH, M_V_W, D_MODEL), M_V_W ** -0.5),
        'w_b': nrm(ks[24], (DEPTH, A_V_W, D_MODEL), A_V_W ** -0.5),
        'w_out': nrm(ks[25], (DEPTH, D_MODEL, D_MODEL), D_MODEL ** -0.5),
        'g_ffn': 1.0 + nrm(ks[26], (DEPTH, D_MODEL), 0.02),
        'w_pq': nrm(ks[27], (DEPTH, D_MODEL, P_HEADS * P_DK), D_MODEL ** -0.5),
        'sub_keys': nrm(ks[28], (DEPTH, P_HEADS, 2, P_KEYS, P_DK // 2), (P_DK // 2) ** -0.5),
        'u_tab': nrm(ks[29], (DEPTH, P_EXPERTS, D_MODEL), D_MODEL ** -0.5),
        'v_tab': nrm(ks[30], (DEPTH, P_EXPERTS, D_MODEL), 0.5),
        'g_final': 1.0 + nrm(ks[31], (D_MODEL,), 0.02),
    }


def reference(x_prompt, x_sample, cache_k, cache_v, state_C, state_n, state_m, state_conv,
              c_prompt, c_sample, w_ada, b_ada, g_mix, w_in, conv_w, conv_b, b_gates, m_norm_g,
              lam_q1, lam_k1, lam_q2, lam_k2, a_norm_g, w_a, w_b, w_out, g_ffn, w_pq, sub_keys,
              u_tab, v_tab, g_final):
    p = dict(w_ada=w_ada, b_ada=b_ada, g_mix=g_mix, w_in=w_in, conv_w=conv_w, conv_b=conv_b,
             b_gates=b_gates, m_norm_g=m_norm_g, lam_q1=lam_q1, lam_k1=lam_k1, lam_q2=lam_q2,
             lam_k2=lam_k2, a_norm_g=a_norm_g, w_a=w_a, w_b=w_b, w_out=w_out, g_ffn=g_ffn,
             w_pq=w_pq, sub_keys=sub_keys, u_tab=u_tab, v_tab=v_tab)
    f32 = jnp.float32

    B = x_prompt.shape[0]
    x = x_prompt
    st_p = []
    for l in range(DEPTH):
        x, st = _layer(l, x, c_prompt,
                       jnp.zeros((B, 0, A_HEADS, 2, A_DH), x.dtype), jnp.zeros((B, 0, A_HEADS, A_V), x.dtype),
                       jnp.zeros((B, M_HEADS, M_V, M_QK), f32), jnp.zeros((B, M_HEADS, M_QK), f32),
                       jnp.zeros((B, M_HEADS), f32), jnp.zeros((B, CONV_W - 1, 2 * M_QK_W), x.dtype), p)
        st_p.append(st)
    y_prompt = _rms(x, g_final)

    x = x_sample
    st_s = []
    for l in range(DEPTH):
        x, st = _layer(l, x, c_sample, cache_k[l], cache_v[l], state_C[l], state_n[l], state_m[l],
                       state_conv[l], p)
        st_s.append(st)
    y_sample = _rms(x, g_final)

    k_p = jnp.stack([s[0] for s in st_p])
    v_p = jnp.stack([s[1] for s in st_p])
    C_p = jnp.stack([s[2] for s in st_p])
    n_p = jnp.stack([s[3] for s in st_p])
    m_p = jnp.stack([s[4] for s in st_p])
    conv_p = jnp.stack([s[5] for s in st_p])
    k_s = jnp.stack([s[0] for s in st_s])
    v_s = jnp.stack([s[1] for s in st_s])
    C_s = jnp.stack([s[2] for s in st_s])
    n_s = jnp.stack([s[3] for s in st_s])
    m_s = jnp.stack([s[4] for s in st_s])
    conv_s = jnp.stack([s[5] for s in st_s])
    return (y_prompt, y_sample, k_p, v_p, C_p, n_p, m_p, conv_p, k_s, v_s, C_s, n_s, m_s, conv_s)
```

```python
import functools
import math

import jax
import jax.numpy as jnp
from jax import lax
from jax.experimental import pallas as pl
from jax.experimental.pallas import tpu as pltpu

F32 = jnp.float32
BF16 = jnp.bfloat16

EPS = 1e-6
ATTN_CHUNK = 64
CONV_W = 4
P_TOPK = 16
ROW_GROUP = 32
MLSTM_CHUNK = 256
VMEM_LIMIT_MB = 56


def _params(sem, vmem_mb=VMEM_LIMIT_MB):
    return pltpu.CompilerParams(dimension_semantics=sem, vmem_limit_bytes=vmem_mb << 20)


def _pick(n, cands):
    for c in cands:
        if n % c == 0:
            return c
    raise ValueError(f"no tile in {cands} divides {n}")


def _dot(a, b):
    return jnp.dot(a, b, preferred_element_type=F32)


def _dot_nt(a, b):
    return lax.dot_general(a, b, (((1,), (1,)), ((), ())), preferred_element_type=F32)


def _dot_tn(a, b):
    return lax.dot_general(a, b, (((0,), (0,)), ((), ())), preferred_element_type=F32)


def _split3(x):
    hi = x.astype(BF16)
    r = x - hi.astype(F32)
    mid = r.astype(BF16)
    lo = (r - mid.astype(F32)).astype(BF16)
    return hi, mid, lo


def _dot_exact_lhs(a01, x):
    hi, mid, lo = _split3(x)
    return _dot(a01, hi) + _dot(a01, mid) + _dot(a01, lo)


def _ada_kernel(c_ref, w_ref, b_ref, o_ref):
    c = c_ref[...]
    a = (c * jax.nn.sigmoid(c)).astype(BF16)
    o_ref[...] = _dot(a, w_ref[...].astype(BF16)) + b_ref[...]


def _ada(c_all, w_ada, b_ada):
    m, d = c_all.shape
    n = w_ada.shape[1]
    tn = _pick(n, (512, 256, 128))
    return pl.pallas_call(
        _ada_kernel,
        grid=(n // tn,),
        in_specs=[pl.BlockSpec((m, d), lambda j: (0, 0)),
                  pl.BlockSpec((d, tn), lambda j: (0, j)),
                  pl.BlockSpec((1, tn), lambda j: (0, j))],
        out_specs=pl.BlockSpec((m, tn), lambda j: (0, j)),
        out_shape=jax.ShapeDtypeStruct((m, n), F32),
        compiler_params=_params(("parallel",)),
        name="ada_mod",
    )(c_all, w_ada, b_ada.reshape(1, n))


def _norm_mod_kernel(x_ref, g_ref, sc_ref, sh_ref, o_ref):
    tm, d = x_ref.shape
    ng = sc_ref.shape[0]
    x = x_ref[...]
    y = x * lax.rsqrt(jnp.mean(x * x, -1, keepdims=True) + EPS) * g_ref[...]
    y = y.reshape(ng, tm // ng, d) * (1.0 + sc_ref[...]) + sh_ref[...]
    o_ref[...] = y.reshape(tm, d).astype(o_ref.dtype)


def _norm_mod(x, g, modg, i_scale, i_shift):
    n, d = x.shape
    tm = _pick(n, (256, 128))
    ng = tm // ROW_GROUP
    return pl.pallas_call(
        _norm_mod_kernel,
        grid=(n // tm,),
        in_specs=[pl.BlockSpec((tm, d), lambda i: (i, 0)),
                  pl.BlockSpec((1, d), lambda i: (0, 0)),
                  pl.BlockSpec((None, ng, 1, d), lambda i: (i_scale, i, 0, 0)),
                  pl.BlockSpec((None, ng, 1, d), lambda i: (i_shift, i, 0, 0))],
        out_specs=pl.BlockSpec((tm, d), lambda i: (i, 0)),
        out_shape=jax.ShapeDtypeStruct((n, d), BF16),
        compiler_params=_params(("parallel",)),
        name="norm_mod",
    )(x, g.reshape(1, d), modg, modg)


def _mm_kernel(a_ref, b_ref, o_ref):
    o_ref[...] = _dot(a_ref[...], b_ref[...]).astype(o_ref.dtype)


def _matmul(a, b, out_dtype, name):
    m, k = a.shape
    n = b.shape[1]
    tm = _pick(m, (1024, 512, 256, 128))
    tn = _pick(n, (1024, 512, 256, 128))
    return pl.pallas_call(
        _mm_kernel,
        grid=(m // tm, n // tn),
        in_specs=[pl.BlockSpec((tm, k), lambda i, j: (i, 0)),
                  pl.BlockSpec((k, tn), lambda i, j: (0, j))],
        out_specs=pl.BlockSpec((tm, tn), lambda i, j: (i, j)),
        out_shape=jax.ShapeDtypeStruct((m, n), out_dtype),
        compiler_params=_params(("parallel", "parallel")),
        name=name,
    )(a, b)


def _mlstm_kernel(qk_ref, v_ref, og_ref, gt_ref, conv0_ref, cw_ref, cb_ref, bg_ref, ng_ref,
                  c0_ref, n0_ref, m0_ref,
                  hm_ref, c_ref, n_ref, m_ref, convn_ref, full_ref, *, heads, dk, dv):
    L = qk_ref.shape[0]
    pad = 8
    hist = CONV_W - 1
    ci = pl.program_id(1)

    @pl.when(ci == 0)
    def _():
        c_ref[...] = c0_ref[...]
        n_ref[...] = n0_ref[...]
        m_ref[...] = m0_ref[...]
        full_ref[pad - hist:pad, :] = conv0_ref[0]

    u = qk_ref[...]
    full_ref[pad:pad + L, :] = u
    acc = cb_ref[...] + u * cw_ref[hist:hist + 1, :]
    for j in range(hist):
        acc = acc + full_ref[pad - hist + j:pad - hist + j + L, :] * cw_ref[j:j + 1, :]
    tail = full_ref[pad + L - hist:pad + L, :]
    convn_ref[0] = tail
    full_ref[pad - hist:pad, :] = tail
    qk = acc * jax.nn.sigmoid(acc)

    gates = gt_ref[...]
    row = lax.broadcasted_iota(jnp.int32, (L, L), 0)
    col = lax.broadcasted_iota(jnp.int32, (L, L), 1)
    causal = col <= row
    tril01 = jnp.where(causal, 1.0, 0.0).astype(BF16)
    ones01 = jnp.ones((L, L), BF16)

    for h in range(heads):
        q = qk[:, h * dk:(h + 1) * dk]
        k = qk[:, (heads + h) * dk:(heads + h + 1) * dk] * (dk ** -0.5)
        v = v_ref[:, h * dv:(h + 1) * dv]
        qb, kb, vb = q.astype(BF16), k.astype(BF16), v.astype(BF16)
        i_col = gates[:, h:h + 1] + bg_ref[0:1, h:h + 1]
        f_col = gates[:, heads + h:heads + h + 1] + bg_ref[1:2, h:h + 1]
        logf = jnp.minimum(f_col, 0.0) - jnp.log1p(jnp.exp(-jnp.abs(f_col)))
        lf_b = jnp.broadcast_to(logf, (L, L))
        ig_b = jnp.broadcast_to(i_col, (L, L))
        d1 = _dot_exact_lhs(tril01, lf_b)
        d2 = _dot_exact_lhs(ones01, jnp.where(row == col, ig_b, 0.0) - jnp.where(row <= col, lf_b, 0.0))
        dmat = jnp.where(causal, d1 + d2, -jnp.inf)
        m_prev = m_ref[0, h:h + 1, 0:1]
        b_col = d1[:, 0:1]
        inter = b_col + m_prev
        m_t = jnp.maximum(inter, jnp.max(dmat, axis=1, keepdims=True))
        s = _dot_nt(qb, kb) * jnp.exp(dmat - m_t)
        w_prev = jnp.exp(inter - m_t)
        c_old = c_ref[0, h]
        n_old = n_ref[0, h:h + 1, :]
        num = _dot(s.astype(BF16), vb) + w_prev * _dot_nt(qb, c_old.astype(BF16))
        den = jnp.sum(s, axis=1, keepdims=True) + w_prev * jnp.sum(q * n_old, axis=1, keepdims=True)
        hh = num / jnp.maximum(jnp.abs(den), jnp.exp(-m_t))
        m_new = m_t[L - 1:L, :]
        b_last = b_col[L - 1:L, :]
        decay = jnp.exp(b_last + m_prev - m_new)
        w_src = jnp.exp(b_last - b_col + i_col - m_new)
        c_ref[0, h] = decay * c_old + _dot_tn((w_src * v).astype(BF16), kb)
        n_ref[0, h:h + 1, :] = decay * n_old + jnp.sum(w_src * k, axis=0, keepdims=True)
        m_ref[0, h:h + 1, :] = jnp.broadcast_to(m_new, (1, m_ref.shape[2]))
        hn = hh * lax.rsqrt(jnp.mean(hh * hh, -1, keepdims=True) + EPS) * ng_ref[h:h + 1, :]
        hn = hn * jax.nn.sigmoid(og_ref[:, h * dv:(h + 1) * dv])
        hm_ref[:, h * dv:(h + 1) * dv] = hn.astype(hm_ref.dtype)


def _mlstm(proj, gates, row0, nb, t, conv0, conv_w, conv_b, b_gates, norm_g, c0, n0, m0):
    heads, dv = norm_g.shape
    dk = c0.shape[-1]
    qkw = 2 * heads * dk
    vw = heads * dv
    L = _pick(t, (MLSTM_CHUNK, 128, 64, 32, 16, 8))
    nc = t // L
    assert row0 % L == 0 and qkw % vw == 0 and L >= CONV_W - 1
    rb0 = row0 // L
    lanes = 128
    m0b = jnp.broadcast_to(m0[:, :, None], (nb, heads, lanes))
    rmap = lambda b, c: rb0 + b * nc + c
    kern = functools.partial(_mlstm_kernel, heads=heads, dk=dk, dv=dv)
    hm, c1, n1, m1, convn = pl.pallas_call(
        kern,
        grid=(nb, nc),
        in_specs=[pl.BlockSpec((L, qkw), lambda b, c: (rmap(b, c), 0)),
                  pl.BlockSpec((L, vw), lambda b, c: (rmap(b, c), qkw // vw)),
                  pl.BlockSpec((L, vw), lambda b, c: (rmap(b, c), qkw // vw + 1)),
                  pl.BlockSpec((L, gates.shape[1]), lambda b, c: (rmap(b, c), 0)),
                  pl.BlockSpec((1, CONV_W - 1, qkw), lambda b, c: (b, 0, 0)),
                  pl.BlockSpec((CONV_W, qkw), lambda b, c: (0, 0)),
                  pl.BlockSpec((1, qkw), lambda b, c: (0, 0)),
                  pl.BlockSpec((2, heads), lambda b, c: (0, 0)),
                  pl.BlockSpec((heads, dv), lambda b, c: (0, 0)),
                  pl.BlockSpec((1, heads, dv, dk), lambda b, c: (b, 0, 0, 0)),
                  pl.BlockSpec((1, heads, dk), lambda b, c: (b, 0, 0)),
                  pl.BlockSpec((1, heads, lanes), lambda b, c: (b, 0, 0))],
        out_specs=[pl.BlockSpec((L, vw), lambda b, c: (b * nc + c, 0)),
                   pl.BlockSpec((1, heads, dv, dk), lambda b, c: (b, 0, 0, 0)),
                   pl.BlockSpec((1, heads, dk), lambda b, c: (b, 0, 0)),
                   pl.BlockSpec((1, heads, lanes), lambda b, c: (b, 0, 0)),
                   pl.BlockSpec((1, CONV_W - 1, qkw), lambda b, c: (b, 0, 0))],
        out_shape=[jax.ShapeDtypeStruct((nb * t, vw), BF16),
                   jax.ShapeDtypeStruct((nb, heads, dv, dk), F32),
                   jax.ShapeDtypeStruct((nb, heads, dk), F32),
                   jax.ShapeDtypeStruct((nb, heads, lanes), F32),
                   jax.ShapeDtypeStruct((nb, CONV_W - 1, qkw), F32)],
        scratch_shapes=[pltpu.VMEM((8 + L, qkw), F32)],
        compiler_params=_params(("parallel", "arbitrary")),
        name="mlstm",
    )(proj, proj, proj, gates, conv0, conv_w, conv_b.reshape(1, qkw), b_gates, norm_g, c0, n0, m0b)
    return hm, c1, n1, m1[:, :, 0], convn


def _lambda(lq1, lk1, lq2, lk2, lam_init):
    return (jnp.exp(jnp.sum(lq1[...] * lk1[...], axis=1, keepdims=True))
            - jnp.exp(jnp.sum(lq2[...] * lk2[...], axis=1, keepdims=True)) + lam_init)


def _attn_bias(slope, rowp, colp):
    dist = jnp.abs((rowp - colp).astype(F32))
    vis = (colp // ATTN_CHUNK) <= (rowp // ATTN_CHUNK)
    return -slope * dist, vis


def _head_norm_out(o, g_row, lam_init):
    return o * lax.rsqrt(jnp.mean(o * o, -1, keepdims=True) + EPS) * g_row * (1.0 - lam_init)


def _attn_prompt_kernel(slope_ref, q_ref, k_ref, v_ref, lq1, lk1, lq2, lk2, ng_ref, o_ref,
                        qs_ref, m_ref, l_ref, acc_ref, *, dh, lam_init):
    tq = q_ref.shape[0]
    tk = k_ref.shape[0]
    h = pl.program_id(1)
    qi = pl.program_id(2)
    kj = pl.program_id(3)

    @pl.when(kj == 0)
    def _():
        qs_ref[...] = (q_ref[...] * (dh ** -0.5)).astype(BF16)
        m_ref[...] = jnp.full(m_ref.shape, -jnp.inf, F32)
        l_ref[...] = jnp.zeros(l_ref.shape, F32)
        acc_ref[...] = jnp.zeros(acc_ref.shape, F32)

    @pl.when(kj <= qi)
    def _():
        kb = k_ref[...].astype(BF16)
        vb = v_ref[...].astype(BF16)
        rowp = qi * tq + lax.broadcasted_iota(jnp.int32, (tq, 1), 0)
        colp = kj * tk + lax.broadcasted_iota(jnp.int32, (1, tk), 1)
        bias, vis = _attn_bias(slope_ref[h], rowp, colp)
        for c in range(2):
            s = _dot_nt(qs_ref[:, c * dh:(c + 1) * dh], kb[:, c * dh:(c + 1) * dh]) + bias
            s = jnp.where(vis, s, -jnp.inf)
            m_old = m_ref[c]
            m_new = jnp.maximum(m_old, jnp.max(s, axis=1, keepdims=True))
            alpha = jnp.exp(m_old - m_new)
            p = jnp.exp(s - m_new)
            l_ref[c] = alpha * l_ref[c] + jnp.sum(p, axis=1, keepdims=True)
            acc_ref[c] = alpha * acc_ref[c] + _dot(p.astype(BF16), vb)
            m_ref[c] = m_new

    @pl.when(kj == qi)
    def _():
        lam = _lambda(lq1, lk1, lq2, lk2, lam_init)
        o = acc_ref[0] / l_ref[0] - lam * (acc_ref[1] / l_ref[1])
        o_ref[...] = _head_norm_out(o, ng_ref[...], lam_init).astype(o_ref.dtype)


def _attn_prompt(proj, nb, t, col_q, col_k, col_v, slopes, lams, norm_g, lam_init):
    heads, dv = norm_g.shape
    dh = dv // 2
    tq = _pick(t, (512, 256, 128, 64))
    nq = t // tq
    assert tq % ATTN_CHUNK == 0 and col_q % dv == 0 and col_k % dv == 0 and col_v % dv == 0
    cq, ck, cv = col_q // dv, col_k // dv, col_v // dv
    kern = functools.partial(_attn_prompt_kernel, dh=dh, lam_init=lam_init)
    lam_spec = pl.BlockSpec((1, dh), lambda b, h, i, j: (0, 0))
    return pl.pallas_call(
        kern,
        grid=(nb, heads, nq, nq),
        in_specs=[pl.BlockSpec(memory_space=pltpu.SMEM),
                  pl.BlockSpec((tq, dv), lambda b, h, i, j: (b * nq + i, cq + h)),
                  pl.BlockSpec((tq, dv), lambda b, h, i, j: (b * nq + jnp.minimum(j, i), ck + h)),
                  pl.BlockSpec((tq, dv), lambda b, h, i, j: (b * nq + jnp.minimum(j, i), cv + h)),
                  lam_spec, lam_spec, lam_spec, lam_spec,
                  pl.BlockSpec((None, 1, dv), lambda b, h, i, j: (h, 0, 0))],
        out_specs=pl.BlockSpec((tq, dv), lambda b, h, i, j: (b * nq + i, h)),
        out_shape=jax.ShapeDtypeStruct((nb * t, heads * dv), BF16),
        scratch_shapes=[pltpu.VMEM((tq, dv), BF16),
                        pltpu.VMEM((2, tq, 1), F32),
                        pltpu.VMEM((2, tq, 1), F32),
                        pltpu.VMEM((2, tq, dv), F32)],
        compiler_params=_params(("parallel", "parallel", "parallel", "arbitrary")),
        name="attn_prompt",
    )(slopes, proj, proj, proj, *lams, norm_g.reshape(heads, 1, dv))


def _attn_sample_kernel(slope_ref, q_ref, kn_ref, vn_ref, kp_ref, vp_ref, lq1, lk1, lq2, lk2, ng_ref,
                        o_ref, *, dh, lam_init):
    t = q_ref.shape[0]
    past = kp_ref.shape[1]
    h = pl.program_id(1)
    qb = (q_ref[...] * (dh ** -0.5)).astype(BF16)
    knb = kn_ref[...].astype(BF16)
    vnb = vn_ref[...].astype(BF16)
    kpb = kp_ref[0].astype(BF16)
    vpb = vp_ref[0].astype(BF16)
    rowp = past + lax.broadcasted_iota(jnp.int32, (t, 1), 0)
    bias_p, vis_p = _attn_bias(slope_ref[h], rowp, lax.broadcasted_iota(jnp.int32, (1, past), 1))
    bias_n, vis_n = _attn_bias(slope_ref[h], rowp, past + lax.broadcasted_iota(jnp.int32, (1, t), 1))
    outs = []
    for c in range(2):
        qc = qb[:, c * dh:(c + 1) * dh]
        sp = jnp.where(vis_p, _dot_nt(qc, kpb[:, c * dh:(c + 1) * dh]) + bias_p, -jnp.inf)
        sn = jnp.where(vis_n, _dot_nt(qc, knb[:, c * dh:(c + 1) * dh]) + bias_n, -jnp.inf)
        m = jnp.maximum(jnp.max(sp, axis=1, keepdims=True), jnp.max(sn, axis=1, keepdims=True))
        pp = jnp.exp(sp - m)
        pn = jnp.exp(sn - m)
        l = jnp.sum(pp, axis=1, keepdims=True) + jnp.sum(pn, axis=1, keepdims=True)
        outs.append((_dot(pp.astype(BF16), vpb) + _dot(pn.astype(BF16), vnb)) / l)
    lam = _lambda(lq1, lk1, lq2, lk2, lam_init)
    o = outs[0] - lam * outs[1]
    o_ref[...] = _head_norm_out(o, ng_ref[...], lam_init).astype(o_ref.dtype)


def _attn_sample(proj, row0, nb, t, col_q, col_k, col_v, k_past, v_past, slopes, lams, norm_g, lam_init):
    heads, dv = norm_g.shape
    dh = dv // 2
    past = k_past.shape[1]
    assert row0 % t == 0 and t % 8 == 0
    rb0 = row0 // t
    cq, ck, cv = col_q // dv, col_k // dv, col_v // dv
    kp = k_past.reshape(nb, past, heads * dv)
    vp = v_past.reshape(nb, past, heads * dv)
    kern = functools.partial(_attn_sample_kernel, dh=dh, lam_init=lam_init)
    lam_spec = pl.BlockSpec((1, dh), lambda b, h: (0, 0))
    return pl.pallas_call(
        kern,
        grid=(nb, heads),
        in_specs=[pl.BlockSpec(memory_space=pltpu.SMEM),
                  pl.BlockSpec((t, dv), lambda b, h: (rb0 + b, cq + h)),
                  pl.BlockSpec((t, dv), lambda b, h: (rb0 + b, ck + h)),
                  pl.BlockSpec((t, dv), lambda b, h: (rb0 + b, cv + h)),
                  pl.BlockSpec((1, past, dv), lambda b, h: (b, 0, h)),
                  pl.BlockSpec((1, past, dv), lambda b, h: (b, 0, h)),
                  lam_spec, lam_spec, lam_spec, lam_spec,
                  pl.BlockSpec((None, 1, dv), lambda b, h: (h, 0, 0))],
        out_specs=pl.BlockSpec((t, dv), lambda b, h: (b, h)),
        out_shape=jax.ShapeDtypeStruct((nb * t, heads * dv), BF16),
        compiler_params=_params(("parallel", "parallel")),
        name="attn_sample",
    )(slopes, proj, proj, proj, kp, vp, *lams, norm_g.reshape(heads, 1, dv))


def _merge_kernel(hm_ref, ha_ref, wa_ref, wb_ref, g0_ref, g1_ref, o_ref):
    ya = _dot(hm_ref[...], wa_ref[...])
    yb = _dot(ha_ref[...], wb_ref[...])
    o_ref[...] = (jax.nn.sigmoid(g0_ref[...]) * ya + jax.nn.sigmoid(g1_ref[...]) * yb).astype(o_ref.dtype)


def _merge(hm, ha, w_a, w_b, proj, col_g):
    n, ka = hm.shape
    kb = ha.shape[1]
    d = w_a.shape[1]
    tm = _pick(n, (1024, 512, 256, 128))
    tn = _pick(d, (512, 256, 128))
    assert col_g % tn == 0
    c0 = col_g // tn
    c1 = (col_g + d) // tn
    return pl.pallas_call(
        _merge_kernel,
        grid=(n // tm, d // tn),
        in_specs=[pl.BlockSpec((tm, ka), lambda i, j: (i, 0)),
                  pl.BlockSpec((tm, kb), lambda i, j: (i, 0)),
                  pl.BlockSpec((ka, tn), lambda i, j: (0, j)),
                  pl.BlockSpec((kb, tn), lambda i, j: (0, j)),
                  pl.BlockSpec((tm, tn), lambda i, j: (i, c0 + j)),
                  pl.BlockSpec((tm, tn), lambda i, j: (i, c1 + j))],
        out_specs=pl.BlockSpec((tm, tn), lambda i, j: (i, j)),
        out_shape=jax.ShapeDtypeStruct((n, d), BF16),
        compiler_params=_params(("parallel", "parallel")),
        name="merge",
    )(hm, ha, w_a, w_b, proj, proj)


def _outproj_kernel(a_ref, w_ref, x_ref, gt_ref, o_ref):
    tm, tn = x_ref.shape
    ng = gt_ref.shape[0]
    y = _dot(a_ref[...], w_ref[...]).reshape(ng, tm // ng, tn)
    o_ref[...] = (x_ref[...].reshape(ng, tm // ng, tn) + gt_ref[...] * y).reshape(tm, tn)


def _outproj(a, w, x, modg, i_gate):
    n, k = a.shape
    d = w.shape[1]
    tm = _pick(n, (1024, 512, 256, 128))
    tn = _pick(d, (512, 256, 128))
    ng = tm // ROW_GROUP
    return pl.pallas_call(
        _outproj_kernel,
        grid=(n // tm, d // tn),
        in_specs=[pl.BlockSpec((tm, k), lambda i, j: (i, 0)),
                  pl.BlockSpec((k, tn), lambda i, j: (0, j)),
                  pl.BlockSpec((tm, tn), lambda i, j: (i, j)),
                  pl.BlockSpec((None, ng, 1, tn), lambda i, j: (i_gate, i, 0, j))],
        out_specs=pl.BlockSpec((tm, tn), lambda i, j: (i, j)),
        out_shape=jax.ShapeDtypeStruct((n, d), F32),
        compiler_params=_params(("parallel", "parallel")),
        name="outproj",
    )(a, w, x, modg)


def _top_values(s, k):
    vals = []
    for r in range(k):
        m = jnp.max(s, axis=0, keepdims=True)
        vals.append(m)
        if r + 1 < k:
            s = jnp.where(s == m, -jnp.inf, s)
    return vals


def _peer_select_kernel(q_ref, sk_ref, s_ref, aux_ref, *, heads, dsub):
    tm = q_ref.shape[0]
    for h in range(heads):
        tops = []
        for c in range(2):
            qh = q_ref[:, (2 * h + c) * dsub:(2 * h + c + 1) * dsub]
            sk = sk_ref[h, c]
            qh_hi = qh.astype(BF16)
            qh_lo = (qh - qh_hi.astype(F32)).astype(BF16)
            sk_hi = sk.astype(BF16)
            sk_lo = (sk - sk_hi.astype(F32)).astype(BF16)
            s = _dot_nt(sk_hi, qh_hi) + _dot_nt(sk_hi, qh_lo) + _dot_nt(sk_lo, qh_hi)
            s_ref[h, c] = s
            tops.append(_top_values(s, P_TOPK))
        t2 = jnp.concatenate(tops[1], axis=0)
        cand = jnp.concatenate([tops[0][a] + t2 for a in range(P_TOPK)], axis=0)
        best = _top_values(cand, P_TOPK)
        z = jnp.ones((1, tm), F32)
        for r in range(1, P_TOPK):
            z = z + jnp.exp(best[r] - best[0])
        rows = [best[P_TOPK - 1], tops[0][0], tops[1][0], 1.0 / z]
        aux_ref[h] = jnp.concatenate(rows + [jnp.zeros((8 - len(rows), tm), F32)], axis=0)


def _peer_select(qp, sub_keys):
    n = qp.shape[0]
    heads, _, nkeys, dsub = sub_keys.shape
    tm = _pick(n, (512, 256, 128))
    kern = functools.partial(_peer_select_kernel, heads=heads, dsub=dsub)
    return pl.pallas_call(
        kern,
        grid=(n // tm,),
        in_specs=[pl.BlockSpec((tm, qp.shape[1]), lambda i: (i, 0)),
                  pl.BlockSpec(sub_keys.shape, lambda i: (0, 0, 0, 0))],
        out_specs=[pl.BlockSpec((heads, 2, nkeys, tm), lambda i: (0, 0, 0, i)),
                   pl.BlockSpec((heads, 8, tm), lambda i: (0, 0, i))],
        out_shape=[jax.ShapeDtypeStruct((heads, 2, nkeys, n), F32),
                   jax.ShapeDtypeStruct((heads, 8, n), F32)],
        compiler_params=_params(("parallel",)),
        name="peer_select",
    )(qp, sub_keys)


def _peer_dense_kernel(x_ref, u_ref, v_ref, s_ref, aux_ref, o_ref, e1_ref, e2_ref, *, heads, nkeys):
    te = u_ref.shape[0]
    e = pl.program_id(1)

    @pl.when(e == 0)
    def _():
        o_ref[...] = jnp.zeros(o_ref.shape, F32)
        for h in range(heads):
            e1_ref[h] = jnp.exp(s_ref[h, 0] - aux_ref[h, 1:2, :])
            e2_ref[h] = jnp.exp(s_ref[h, 1] - aux_ref[h, 2:3, :]) * aux_ref[h, 3:4, :]

    act = jax.nn.gelu(_dot_nt(u_ref[...], x_ref[...]), approximate=True)
    parts = []
    for a in range(te // nkeys):
        i1 = e * (te // nkeys) + a
        w = None
        for h in range(heads):
            s_sum = s_ref[h, 0, pl.ds(i1, 1), :] + s_ref[h, 1]
            g = jnp.where(s_sum >= aux_ref[h, 0:1, :], e1_ref[h, pl.ds(i1, 1), :] * e2_ref[h], 0.0)
            w = g if w is None else w + g
        parts.append((w * act[a * nkeys:(a + 1) * nkeys]).astype(BF16))
    p = jnp.concatenate(parts, axis=0)
    o_ref[...] += _dot_tn(p, v_ref[...])


def _peer_dense(x, u_tab, v_tab, s_t, aux):
    n, d = x.shape
    ne = u_tab.shape[0]
    heads, _, nkeys, _ = s_t.shape
    tm = _pick(n, (512, 256, 128))
    te = _pick(ne, (512, 256, 128))
    assert te % nkeys == 0 and ne == nkeys * nkeys
    kern = functools.partial(_peer_dense_kernel, heads=heads, nkeys=nkeys)
    once = pl.Buffered(1)
    return pl.pallas_call(
        kern,
        grid=(n // tm, ne // te),
        in_specs=[pl.BlockSpec((tm, d), lambda i, e: (i, 0), pipeline_mode=once),
                  pl.BlockSpec((te, d), lambda i, e: (e, 0)),
                  pl.BlockSpec((te, d), lambda i, e: (e, 0)),
                  pl.BlockSpec((heads, 2, nkeys, tm), lambda i, e: (0, 0, 0, i), pipeline_mode=once),
                  pl.BlockSpec((heads, 8, tm), lambda i, e: (0, 0, i), pipeline_mode=once)],
        out_specs=pl.BlockSpec((tm, d), lambda i, e: (i, 0), pipeline_mode=once),
        out_shape=jax.ShapeDtypeStruct((n, d), F32),
        scratch_shapes=[pltpu.VMEM((heads, nkeys, tm), F32),
                        pltpu.VMEM((heads, nkeys, tm), F32)],
        compiler_params=_params(("parallel", "arbitrary")),
        name="peer_dense",
    )(x, u_tab, v_tab, s_t, aux)


def _final_kernel(x_ref, p_ref, gt_ref, g_ref, o_ref):
    tm, d = x_ref.shape
    ng = gt_ref.shape[0]
    x = x_ref[...].reshape(ng, tm // ng, d) + gt_ref[...] * p_ref[...].reshape(ng, tm // ng, d)
    x = x.reshape(tm, d)
    o_ref[...] = x * lax.rsqrt(jnp.mean(x * x, -1, keepdims=True) + EPS) * g_ref[...]


def _final(x, peer, modg, i_gate, g_final):
    n, d = x.shape
    tm = _pick(n, (256, 128))
    ng = tm // ROW_GROUP
    return pl.pallas_call(
        _final_kernel,
        grid=(n // tm,),
        in_specs=[pl.BlockSpec((tm, d), lambda i: (i, 0)),
                  pl.BlockSpec((tm, d), lambda i: (i, 0)),
                  pl.BlockSpec((None, ng, 1, d), lambda i: (i_gate, i, 0, 0)),
                  pl.BlockSpec((1, d), lambda i: (0, 0))],
        out_specs=pl.BlockSpec((tm, d), lambda i: (i, 0)),
        out_shape=jax.ShapeDtypeStruct((n, d), F32),
        compiler_params=_params(("parallel",)),
        name="final_norm",
    )(x, peer, modg, g_final.reshape(1, d))


def kernel(x_prompt, x_sample, cache_k, cache_v, state_C, state_n, state_m, state_conv, c_prompt, c_sample, w_ada, b_ada, g_mix, w_in, conv_w, conv_b, b_gates, m_norm_g, lam_q1, lam_k1, lam_q2, lam_k2, a_norm_g, w_a, w_b, w_out, g_ffn, w_pq, sub_keys, u_tab, v_tab, g_final):
    bp, tp, d = x_prompt.shape
    bs, ts, _ = x_sample.shape
    depth = w_in.shape[0]
    assert depth == 1, "the per-layer state plumbing below is written for a single layer"
    l = 0
    heads, dv = m_norm_g.shape[1:]
    dk = state_C.shape[-1]
    a_heads, a_dv = a_norm_g.shape[1:]
    n_p, n_s = bp * tp, bs * ts
    n_all = n_p + n_s
    assert tp % ROW_GROUP == 0 and ts % ROW_GROUP == 0

    qk_w, v_w = 2 * heads * dk, heads * dv
    a_qk_w, a_v_w = a_heads * a_dv, a_heads * a_dv
    col_gate = qk_w + 2 * v_w
    w_l = w_in[l]
    w_main = jnp.concatenate([w_l[:, :col_gate], w_l[:, col_gate + 2 * heads:]], axis=1).astype(BF16)
    w_gate = jnp.pad(w_l[:, col_gate:col_gate + 2 * heads], ((0, 0), (0, 128 - 2 * heads))).astype(BF16)
    col_aq = col_gate
    col_ak = col_aq + a_qk_w
    col_av = col_ak + a_qk_w
    col_bg = col_av + a_v_w

    x_all = jnp.concatenate([x_prompt.reshape(n_p, d), x_sample.reshape(n_s, d)], axis=0)

    c_all = jnp.concatenate([c_prompt, c_sample], axis=0)
    c_all = jnp.pad(c_all, ((0, (-c_all.shape[0]) % 8), (0, 0)))
    mod = _ada(c_all, w_ada[l], b_ada[l])
    modg = jnp.concatenate([jnp.repeat(mod[:bp], tp // ROW_GROUP, axis=0),
                            jnp.repeat(mod[bp:bp + bs], ts // ROW_GROUP, axis=0)], axis=0)
    modg = modg.reshape(n_all // ROW_GROUP, 6, 1, d).transpose(1, 0, 2, 3)

    hn = _norm_mod(x_all, g_mix[l], modg, 1, 0)
    proj = _matmul(hn, w_main, F32, "proj")
    gates = _matmul(hn, w_gate, F32, "gate_proj")

    zc = jnp.zeros((bp, heads, dv, dk), F32)
    zn = jnp.zeros((bp, heads, dk), F32)
    zm = jnp.zeros((bp, heads), F32)
    zconv = jnp.zeros((bp, CONV_W - 1, qk_w), F32)
    mp = _mlstm(proj, gates, 0, bp, tp, zconv, conv_w[l], conv_b[l], b_gates[l], m_norm_g[l], zc, zn, zm)
    ms = _mlstm(proj, gates, n_p, bs, ts, state_conv[l], conv_w[l], conv_b[l], b_gates[l], m_norm_g[l],
                state_C[l], state_n[l], state_m[l])
    hm = jnp.concatenate([mp[0], ms[0]], axis=0)

    lam_init = 0.8 - 0.6 * math.exp(-0.3 * l)
    slopes = jnp.array([2.0 ** (-8.0 * (h + 1) / a_heads) for h in range(a_heads)], F32)
    lams = [a[l].reshape(1, -1) for a in (lam_q1, lam_k1, lam_q2, lam_k2)]
    ha_p = _attn_prompt(proj, bp, tp, col_aq, col_ak, col_av, slopes, lams, a_norm_g[l], lam_init)
    ha_s = _attn_sample(proj, n_p, bs, ts, col_aq, col_ak, col_av, cache_k[l], cache_v[l], slopes, lams,
                        a_norm_g[l], lam_init)
    ha = jnp.concatenate([ha_p, ha_s], axis=0)

    merged = _merge(hm, ha, w_a[l].astype(BF16), w_b[l].astype(BF16), proj, col_bg)
    x1 = _outproj(merged, w_out[l].astype(BF16), x_all, modg, 2)

    hn2 = _norm_mod(x1, g_ffn[l], modg, 4, 3)
    qp = _matmul(hn2, w_pq[l].astype(BF16), F32, "peer_query")
    s_t, aux = _peer_select(qp, sub_keys[l])
    peer = _peer_dense(hn2, u_tab[l].astype(BF16), v_tab[l].astype(BF16), s_t, aux)
    y = _final(x1, peer, modg, 5, g_final)

    k_new = proj[:, col_ak:col_ak + a_qk_w]
    v_new = proj[:, col_av:col_av + a_v_w]
    dh = a_dv // 2
    return (y[:n_p].reshape(bp, tp, d), y[n_p:].reshape(bs, ts, d),
            k_new[:n_p].reshape(1, bp, tp, a_heads, 2, dh), v_new[:n_p].reshape(1, bp, tp, a_heads, a_dv),
            mp[1][None], mp[2][None], mp[3][None], mp[4][None],
            k_new[n_p:].reshape(1, bs, ts, a_heads, 2, dh), v_new[n_p:].reshape(1, bs, ts, a_heads, a_dv),
            ms[1][None], ms[2][None], ms[3][None], ms[4][None])
```

```python
import functools
import math

import jax
import jax.numpy as jnp
from jax import lax
from jax.experimental import pallas as pl
from jax.experimental.pallas import tpu as pltpu

F32 = jnp.float32
BF16 = jnp.bfloat16

EPS = 1e-6
ATTN_CHUNK = 64
CONV_W = 4
P_TOPK = 16
PEER_KEY_BLOCK = 32
LANES = 128
MLSTM_CHUNK = 256
VMEM_LIMIT_MB = 56
LOG2E = math.log2(math.e)


def _params(sem, vmem_mb=VMEM_LIMIT_MB):
    return pltpu.CompilerParams(dimension_semantics=sem, vmem_limit_bytes=vmem_mb << 20)


def _pick(n, cands):
    for c in cands:
        if n % c == 0:
            return c
    raise ValueError(f"no tile in {cands} divides {n}")


def _dot(a, b):
    return jnp.dot(a, b, preferred_element_type=F32)


def _dot_nt(a, b):
    return lax.dot_general(a, b, (((1,), (1,)), ((), ())), preferred_element_type=F32)


def _dot_tn(a, b):
    return lax.dot_general(a, b, (((0,), (0,)), ((), ())), preferred_element_type=F32)


def _split3(x):
    hi = x.astype(BF16)
    r = x - hi.astype(F32)
    mid = r.astype(BF16)
    lo = (r - mid.astype(F32)).astype(BF16)
    return hi, mid, lo


def _dot_exact_lhs(a01, x):
    hi, mid, lo = _split3(x)
    return _dot(a01, hi) + _dot(a01, mid) + _dot(a01, lo)


class _Rows:
    def __init__(self, bp, tp, bs, ts, tm):
        assert (bp * tp) % tm == 0 and (bs * ts) % tm == 0
        assert tp % tm == 0 and tm % ts == 0, "a prompt tile sits in one sequence, a sample tile holds whole ones"
        self.bp, self.tp, self.bs, self.ts, self.tm = bp, tp, bs, ts, tm
        self.nbp = bp * tp // tm
        self.nbs = bs * ts // tm
        self.nb = self.nbp + self.nbs
        self.ng = tm // ts

    def p_tile(self, i):
        return jnp.minimum(i, self.nbp - 1)

    def s_tile(self, i):
        return jnp.maximum(i - self.nbp, 0)

    def mod_specs(self, k, width, col=None):
        cj = (lambda *g: 0) if col is None else col
        p = pl.BlockSpec((None, None, 1, width),
                         lambda *g: (self.bs + (self.p_tile(g[0]) * self.tm) // self.tp, k, 0, cj(*g)))
        s = pl.BlockSpec((self.ng, None, 1, width), lambda *g: (self.s_tile(g[0]), k, 0, cj(*g)))
        return p, s


def _mod_apply(x, gate_rows, fn):
    tm, w = x.shape
    rows = [r.reshape((-1, 1, w)) for r in gate_rows]
    ng = rows[0].shape[0]
    return fn(x.reshape(ng, tm // ng, w), *rows).reshape(tm, w)


def _cast_kernel(w_ref, o_ref):
    o_ref[...] = w_ref[...].astype(o_ref.dtype)


def _cast_bf16(w3, l):
    _, r, c = w3.shape
    tr = _pick(r, (512, 256, 128))
    tc = _pick(c, (2048, 1024, 512, 256, 128))
    return pl.pallas_call(
        _cast_kernel,
        grid=(r // tr, c // tc),
        in_specs=[pl.BlockSpec((None, tr, tc), lambda i, j: (l, i, j))],
        out_specs=pl.BlockSpec((tr, tc), lambda i, j: (i, j)),
        out_shape=jax.ShapeDtypeStruct((r, c), BF16),
        compiler_params=_params(("parallel", "parallel")),
        name="cast_bf16",
    )(w3)


def _ada_kernel(c_ref, w_ref, b_ref, o_ref):
    c = c_ref[...]
    a = (c * jax.nn.sigmoid(c)).astype(BF16)
    o_ref[...] = _dot(a, w_ref[...].astype(BF16)) + b_ref[...]


def _ada(c_all, w_ada, b_ada, l):
    m, d = c_all.shape
    n = w_ada.shape[2]
    tn = _pick(n, (512, 256, 128))
    return pl.pallas_call(
        _ada_kernel,
        grid=(n // tn,),
        in_specs=[pl.BlockSpec((m, d), lambda j: (0, 0)),
                  pl.BlockSpec((None, d, tn), lambda j: (l, 0, j)),
                  pl.BlockSpec((None, 1, tn), lambda j: (l, 0, j))],
        out_specs=pl.BlockSpec((m, tn), lambda j: (0, j)),
        out_shape=jax.ShapeDtypeStruct((m, n), F32),
        compiler_params=_params(("parallel",)),
        name="ada_mod",
    )(c_all, w_ada, b_ada.reshape(b_ada.shape[0], 1, n))


def _norm_mod_kernel(xp_ref, xs_ref, g_ref, scp_ref, shp_ref, scs_ref, shs_ref, o_ref, *, nbp):
    i = pl.program_id(0)

    def body(x_ref, sc_ref, sh_ref):
        x = x_ref[...]
        y = x * lax.rsqrt(jnp.mean(x * x, -1, keepdims=True) + EPS) * g_ref[...]
        y = _mod_apply(y, [sc_ref[...], sh_ref[...]], lambda y3, sc, sh: y3 * (1.0 + sc) + sh)
        o_ref[...] = y.astype(o_ref.dtype)

    @pl.when(i < nbp)
    def _():
        body(xp_ref, scp_ref, shp_ref)

    @pl.when(i >= nbp)
    def _():
        body(xs_ref, scs_ref, shs_ref)


def _norm_mod(xp, xs, s_off, rows, g, mod, k_scale, k_shift):
    d = xp.shape[1]
    tm = rows.tm
    scp, scs = rows.mod_specs(k_scale, d)
    shp, shs = rows.mod_specs(k_shift, d)
    return pl.pallas_call(
        functools.partial(_norm_mod_kernel, nbp=rows.nbp),
        grid=(rows.nb,),
        in_specs=[pl.BlockSpec((tm, d), lambda i: (rows.p_tile(i), 0)),
                  pl.BlockSpec((tm, d), lambda i: (s_off + rows.s_tile(i), 0)),
                  pl.BlockSpec((1, d), lambda i: (0, 0)),
                  scp, shp, scs, shs],
        out_specs=pl.BlockSpec((tm, d), lambda i: (i, 0)),
        out_shape=jax.ShapeDtypeStruct((rows.nb * tm, d), BF16),
        compiler_params=_params(("arbitrary",)),
        name="norm_mod",
    )(xp, xs, g.reshape(1, d), mod, mod, mod, mod)


def _proj_kernel(x_ref, w_ref, wn_ref, *refs, shift, scale, mode, nip):
    outs, wb_ref = refs[:-1], refs[-1]
    i = pl.program_id(1)

    @pl.when(i == 0)
    def _():
        k = w_ref.shape[0]
        ck = _pick(k, (512, 256, 128))
        for r in range(0, k, ck):
            cur = w_ref[r:r + ck, :]
            if shift:
                cur = jnp.concatenate([cur[:, shift:], wn_ref[r:r + ck, :shift]], axis=1)
            wb_ref[r:r + ck, :] = cur.astype(BF16)

    acc = _dot(x_ref[...], wb_ref[...])
    if scale != 1.0:
        acc = acc * scale
    if mode == "kv":
        outs[0][...] = acc.astype(BF16)

        @pl.when(i < nip)
        def _():
            outs[1][...] = acc

        @pl.when(i >= nip)
        def _():
            outs[2][...] = acc
    else:
        outs[0][...] = acc.astype(outs[0].dtype)


def _proj(hn, w_in, l, col0, ncols, mode, n_p, scale=1.0):
    n, k = hn.shape
    tm = _pick(math.gcd(n_p, n - n_p), (1024, 512, 256, 128))
    tn = _pick(ncols, ((1024,) if mode == "bf16" else ()) + (512, 256, 128))
    shift = col0 % tn
    assert (col0 - shift) % tn == 0 and shift < LANES and tn % LANES == 0
    cb0 = (col0 - shift) // tn
    nip = n_p // tm
    ni, nj = n // tm, ncols // tn
    once = pl.Buffered(1)
    in_specs = [pl.BlockSpec((tm, k), lambda j, i: (i, 0)),
                pl.BlockSpec((None, k, tn), lambda j, i: (l, 0, cb0 + j), pipeline_mode=once),
                pl.BlockSpec((None, k, LANES), lambda j, i: (l, 0, (cb0 + j + 1) * (tn // LANES)),
                             pipeline_mode=once)]
    all_spec = pl.BlockSpec((tm, tn), lambda j, i: (i, j))
    if mode == "kv":
        out_specs = [all_spec,
                     pl.BlockSpec((tm, tn), lambda j, i: (jnp.minimum(i, nip - 1), j)),
                     pl.BlockSpec((tm, tn), lambda j, i: (jnp.maximum(i - nip, 0), j))]
        out_shape = [jax.ShapeDtypeStruct((n, ncols), BF16),
                     jax.ShapeDtypeStruct((n_p, ncols), F32),
                     jax.ShapeDtypeStruct((n - n_p, ncols), F32)]
    else:
        out_specs = [all_spec]
        out_shape = [jax.ShapeDtypeStruct((n, ncols), F32 if mode == "f32" else BF16)]
    kern = functools.partial(_proj_kernel, shift=shift, scale=scale, mode=mode, nip=nip)
    return pl.pallas_call(
        kern,
        grid=(nj, ni),
        in_specs=in_specs,
        out_specs=out_specs,
        out_shape=out_shape,
        scratch_shapes=[pltpu.VMEM((k, tn), BF16)],
        compiler_params=_params(("arbitrary", "arbitrary")),
        name="proj_" + mode,
    )(hn, w_in, w_in)


def _gate_kernel(x_ref, w_ref, o_ref):
    o_ref[...] = _dot(x_ref[...], w_ref[...].astype(BF16))


def _gate_proj(hn, w_in, l, col0):
    n, k = hn.shape
    assert col0 % LANES == 0
    tm = _pick(n, (1024, 512, 256, 128))
    return pl.pallas_call(
        _gate_kernel,
        grid=(n // tm,),
        in_specs=[pl.BlockSpec((tm, k), lambda i: (i, 0)),
                  pl.BlockSpec((None, k, LANES), lambda i: (l, 0, col0 // LANES))],
        out_specs=pl.BlockSpec((tm, LANES), lambda i: (i, 0)),
        out_shape=jax.ShapeDtypeStruct((n, LANES), F32),
        compiler_params=_params(("parallel",)),
        name="gate_proj",
    )(hn, w_in)


def _mm_kernel(a_ref, b_ref, o_ref):
    o_ref[...] = _dot(a_ref[...], b_ref[...]).astype(o_ref.dtype)


def _matmul(a, b, out_dtype, name):
    m, k = a.shape
    n = b.shape[1]
    tm = _pick(m, (1024, 512, 256, 128))
    tn = _pick(n, (1024, 512, 256, 128))
    return pl.pallas_call(
        _mm_kernel,
        grid=(m // tm, n // tn),
        in_specs=[pl.BlockSpec((tm, k), lambda i, j: (i, 0)),
                  pl.BlockSpec((k, tn), lambda i, j: (0, j))],
        out_specs=pl.BlockSpec((tm, tn), lambda i, j: (i, j)),
        out_shape=jax.ShapeDtypeStruct((m, n), out_dtype),
        compiler_params=_params(("parallel", "parallel")),
        name=name,
    )(a, b)


def _mlstm_kernel(qk_ref, v_ref, og_ref, gt_ref, conv0_ref, cw_ref, cb_ref, bg_ref, ng_ref,
                  c0_ref, n0_ref, m0_ref,
                  hm_ref, c_ref, n_ref, m_ref, convn_ref, full_ref, *, heads, dk, dv):
    L = qk_ref.shape[0]
    pad = 8
    hist = CONV_W - 1
    ci = pl.program_id(1)

    @pl.when(ci == 0)
    def _():
        c_ref[...] = c0_ref[...]
        n_ref[...] = n0_ref[...]
        m_ref[...] = m0_ref[...]
        full_ref[pad - hist:pad, :] = conv0_ref[0]

    u = qk_ref[...]
    full_ref[pad:pad + L, :] = u
    acc = cb_ref[...] + u * cw_ref[hist:hist + 1, :]
    for j in range(hist):
        acc = acc + full_ref[pad - hist + j:pad - hist + j + L, :] * cw_ref[j:j + 1, :]
    tail = full_ref[pad + L - hist:pad + L, :]
    convn_ref[0] = tail
    full_ref[pad - hist:pad, :] = tail
    qk = acc * jax.nn.sigmoid(acc)

    gates = gt_ref[...]
    row = lax.broadcasted_iota(jnp.int32, (L, L), 0)
    col = lax.broadcasted_iota(jnp.int32, (L, L), 1)
    causal = col <= row
    tril01 = jnp.where(causal, 1.0, 0.0).astype(BF16)
    ones01 = jnp.ones((L, L), BF16)

    for h in range(heads):
        q = qk[:, h * dk:(h + 1) * dk]
        k = qk[:, (heads + h) * dk:(heads + h + 1) * dk] * (dk ** -0.5)
        vb = v_ref[:, h * dv:(h + 1) * dv]
        v = vb.astype(F32)
        qb, kb = q.astype(BF16), k.astype(BF16)
        i_col = gates[:, h:h + 1] + bg_ref[0:1, h:h + 1]
        f_col = gates[:, heads + h:heads + h + 1] + bg_ref[1:2, h:h + 1]
        logf = jnp.minimum(f_col, 0.0) - jnp.log1p(jnp.exp(-jnp.abs(f_col)))
        lf_b = jnp.broadcast_to(logf, (L, L))
        ig_b = jnp.broadcast_to(i_col, (L, L))
        d1 = _dot_exact_lhs(tril01, lf_b)
        d2 = _dot_exact_lhs(ones01, jnp.where(row == col, ig_b, 0.0) - jnp.where(row <= col, lf_b, 0.0))
        dmat = jnp.where(causal, d1 + d2, -jnp.inf)
        m_prev = m_ref[0, h:h + 1, 0:1]
        b_col = d1[:, 0:1]
        inter = b_col + m_prev
        m_t = jnp.maximum(inter, jnp.max(dmat, axis=1, keepdims=True))
        s = _dot_nt(qb, kb) * jnp.exp(dmat - m_t)
        w_prev = jnp.exp(inter - m_t)
        c_old = c_ref[0, h]
        n_old = n_ref[0, h:h + 1, :]
        num = _dot(s.astype(BF16), vb) + w_prev * _dot_nt(qb, c_old.astype(BF16))
        den = jnp.sum(s, axis=1, keepdims=True) + w_prev * jnp.sum(q * n_old, axis=1, keepdims=True)
        hh = num / jnp.maximum(jnp.abs(den), jnp.exp(-m_t))
        m_new = m_t[L - 1:L, :]
        b_last = b_col[L - 1:L, :]
        decay = jnp.exp(b_last + m_prev - m_new)
        w_src = jnp.exp(b_last - b_col + i_col - m_new)
        c_ref[0, h] = decay * c_old + _dot_tn((w_src * v).astype(BF16), kb)
        n_ref[0, h:h + 1, :] = decay * n_old + jnp.sum(w_src * k, axis=0, keepdims=True)
        m_ref[0, h:h + 1, :] = jnp.broadcast_to(m_new, (1, m_ref.shape[2]))
        hn = hh * lax.rsqrt(jnp.mean(hh * hh, -1, keepdims=True) + EPS) * ng_ref[h:h + 1, :]
        hn = hn * jax.nn.sigmoid(og_ref[:, h * dv:(h + 1) * dv].astype(F32))
        hm_ref[:, h * dv:(h + 1) * dv] = hn.astype(hm_ref.dtype)


def _mlstm(qk_raw, vo, gates, row0, nb, t, conv0, conv_w, conv_b, b_gates, norm_g, c0, n0, m0):
    heads, dv = norm_g.shape
    dk = c0.shape[-1]
    qkw = 2 * heads * dk
    vw = heads * dv
    L = _pick(t, (MLSTM_CHUNK, 128, 64, 32, 16, 8))
    nc = t // L
    assert row0 % L == 0 and L >= CONV_W - 1
    rb0 = row0 // L
    m0b = jnp.broadcast_to(m0[:, :, None], (nb, heads, LANES))
    rmap = lambda b, c: rb0 + b * nc + c
    kern = functools.partial(_mlstm_kernel, heads=heads, dk=dk, dv=dv)
    hm, c1, n1, m1, convn = pl.pallas_call(
        kern,
        grid=(nb, nc),
        in_specs=[pl.BlockSpec((L, qkw), lambda b, c: (rmap(b, c), 0)),
                  pl.BlockSpec((L, vw), lambda b, c: (rmap(b, c), 0)),
                  pl.BlockSpec((L, vw), lambda b, c: (rmap(b, c), 1)),
                  pl.BlockSpec((L, gates.shape[1]), lambda b, c: (rmap(b, c), 0)),
                  pl.BlockSpec((1, CONV_W - 1, qkw), lambda b, c: (b, 0, 0)),
                  pl.BlockSpec((CONV_W, qkw), lambda b, c: (0, 0)),
                  pl.BlockSpec((1, qkw), lambda b, c: (0, 0)),
                  pl.BlockSpec((2, heads), lambda b, c: (0, 0)),
                  pl.BlockSpec((heads, dv), lambda b, c: (0, 0)),
                  pl.BlockSpec((1, heads, dv, dk), lambda b, c: (b, 0, 0, 0)),
                  pl.BlockSpec((1, heads, dk), lambda b, c: (b, 0, 0)),
                  pl.BlockSpec((1, heads, LANES), lambda b, c: (b, 0, 0))],
        out_specs=[pl.BlockSpec((L, vw), lambda b, c: (b * nc + c, 0)),
                   pl.BlockSpec((1, heads, dv, dk), lambda b, c: (b, 0, 0, 0)),
                   pl.BlockSpec((1, heads, dk), lambda b, c: (b, 0, 0)),
                   pl.BlockSpec((1, heads, LANES), lambda b, c: (b, 0, 0)),
                   pl.BlockSpec((1, CONV_W - 1, qkw), lambda b, c: (b, 0, 0))],
        out_shape=[jax.ShapeDtypeStruct((nb * t, vw), BF16),
                   jax.ShapeDtypeStruct((nb, heads, dv, dk), F32),
                   jax.ShapeDtypeStruct((nb, heads, dk), F32),
                   jax.ShapeDtypeStruct((nb, heads, LANES), F32),
                   jax.ShapeDtypeStruct((nb, CONV_W - 1, qkw), F32)],
        scratch_shapes=[pltpu.VMEM((8 + L, qkw), F32)],
        compiler_params=_params(("parallel", "arbitrary")),
        name="mlstm",
    )(qk_raw, vo, vo, gates, conv0, conv_w, conv_b.reshape(1, qkw), b_gates, norm_g, c0, n0, m0b)
    return hm, c1, n1, m1[:, :, 0], convn


def _lambda(lq1, lk1, lq2, lk2, lam_init):
    return (jnp.exp(jnp.sum(lq1[...] * lk1[...], axis=1, keepdims=True))
            - jnp.exp(jnp.sum(lq2[...] * lk2[...], axis=1, keepdims=True)) + lam_init)


def _head_norm_out(o, g_row, lam_init):
    return o * lax.rsqrt(jnp.mean(o * o, -1, keepdims=True) + EPS) * g_row * (1.0 - lam_init)


def _attn_prompt_kernel(slope_ref, q_ref, k_ref, v_ref, lq1, lk1, lq2, lk2, ng_ref, o_ref,
                        m_ref, l_ref, acc_ref, *, dh, lam_init):
    tq = q_ref.shape[0]
    tk = tq
    h = pl.program_id(1)
    qi = pl.program_id(2)
    slope2 = slope_ref[h] * LOG2E
    m_ref[...] = jnp.full(m_ref.shape, -jnp.inf, F32)
    l_ref[...] = jnp.zeros(l_ref.shape, F32)
    acc_ref[...] = jnp.zeros(acc_ref.shape, F32)
    rel0 = lax.broadcasted_iota(jnp.int32, (1, tk), 1)

    def update(kj, diag):
        k0 = pl.multiple_of(kj * tk, tk)
        kb = k_ref[pl.ds(k0, tk), :]
        vb = v_ref[pl.ds(k0, tk), :]
        rel = rel0 + (kj - qi) * tk
        bias = slope2 * rel.astype(F32)
        if diag:
            rowi = lax.broadcasted_iota(jnp.int32, (tq, 1), 0)
            ahead = jnp.maximum(rel - rowi, 0).astype(F32)
            bias = bias - (2.0 * slope2) * ahead
            vis = (rel0 // ATTN_CHUNK) <= (rowi // ATTN_CHUNK)
        for c in range(2):
            s = _dot_nt(q_ref[:, c * dh:(c + 1) * dh], kb[:, c * dh:(c + 1) * dh]) + bias
            if diag:
                s = jnp.where(vis, s, -jnp.inf)
            m_old = m_ref[c]
            m_new = jnp.maximum(m_old, jnp.max(s, axis=1, keepdims=True))
            alpha = jnp.exp2(m_old - m_new)
            p = jnp.exp2(s - m_new)
            l_ref[c] = alpha * l_ref[c] + jnp.sum(p, axis=1, keepdims=True)
            acc_ref[c] = alpha * acc_ref[c] + _dot(p.astype(BF16), vb)
            m_ref[c] = m_new

    def body(kj, carry):
        update(kj, False)
        return carry

    lax.fori_loop(0, qi, body, 0)
    update(qi, True)
    lam = _lambda(lq1, lk1, lq2, lk2, lam_init)
    o = acc_ref[0] / l_ref[0] - lam * (acc_ref[1] / l_ref[1])
    o_ref[...] = _head_norm_out(o, ng_ref[...], lam_init).astype(o_ref.dtype)


def _attn_prompt(aq, ak, av, nb, t, slopes, lams, norm_g, lam_init):
    heads, dv = norm_g.shape
    dh = dv // 2
    tq = _pick(t, (512, 256, 128, 64))
    nq = t // tq
    assert tq % ATTN_CHUNK == 0
    kern = functools.partial(_attn_prompt_kernel, dh=dh, lam_init=lam_init)
    lam_spec = pl.BlockSpec((1, dh), lambda b, h, i: (0, 0))
    return pl.pallas_call(
        kern,
        grid=(nb, heads, nq),
        in_specs=[pl.BlockSpec(memory_space=pltpu.SMEM),
                  pl.BlockSpec((tq, dv), lambda b, h, i: (b * nq + i, h)),
                  pl.BlockSpec((t, dv), lambda b, h, i: (b, h)),
                  pl.BlockSpec((t, dv), lambda b, h, i: (b, h)),
                  lam_spec, lam_spec, lam_spec, lam_spec,
                  pl.BlockSpec((None, 1, dv), lambda b, h, i: (h, 0, 0))],
        out_specs=pl.BlockSpec((tq, dv), lambda b, h, i: (b * nq + i, h)),
        out_shape=jax.ShapeDtypeStruct((nb * t, heads * dv), BF16),
        scratch_shapes=[pltpu.VMEM((2, tq, 1), F32),
                        pltpu.VMEM((2, tq, 1), F32),
                        pltpu.VMEM((2, tq, dv), F32)],
        compiler_params=_params(("parallel", "parallel", "arbitrary")),
        name="attn_prompt",
    )(slopes, aq, ak, av, *lams, norm_g.reshape(heads, 1, dv))


def _attn_sample_kernel(slope_ref, q_ref, kn_ref, vn_ref, kp_ref, vp_ref, lq1, lk1, lq2, lk2, ng_ref,
                        o_ref, *, dh, lam_init):
    t = q_ref.shape[0]
    past = kp_ref.shape[1]
    h = pl.program_id(1)
    slope2 = slope_ref[h] * LOG2E
    qb = q_ref[...]
    knb = kn_ref[...]
    vnb = vn_ref[...]
    kpb = kp_ref[0].astype(BF16)
    vpb = vp_ref[0].astype(BF16)
    rowp = past + lax.broadcasted_iota(jnp.int32, (t, 1), 0)

    def bias_vis(colp):
        dist = jnp.abs((rowp - colp).astype(F32))
        return -slope2 * dist, (colp // ATTN_CHUNK) <= (rowp // ATTN_CHUNK)

    bias_p, vis_p = bias_vis(lax.broadcasted_iota(jnp.int32, (1, past), 1))
    bias_n, vis_n = bias_vis(past + lax.broadcasted_iota(jnp.int32, (1, t), 1))
    outs = []
    for c in range(2):
        qc = qb[:, c * dh:(c + 1) * dh]
        sp = jnp.where(vis_p, _dot_nt(qc, kpb[:, c * dh:(c + 1) * dh]) + bias_p, -jnp.inf)
        sn = jnp.where(vis_n, _dot_nt(qc, knb[:, c * dh:(c + 1) * dh]) + bias_n, -jnp.inf)
        m = jnp.maximum(jnp.max(sp, axis=1, keepdims=True), jnp.max(sn, axis=1, keepdims=True))
        pp = jnp.exp2(sp - m)
        pn = jnp.exp2(sn - m)
        l = jnp.sum(pp, axis=1, keepdims=True) + jnp.sum(pn, axis=1, keepdims=True)
        outs.append((_dot(pp.astype(BF16), vpb) + _dot(pn.astype(BF16), vnb)) / l)
    lam = _lambda(lq1, lk1, lq2, lk2, lam_init)
    o = outs[0] - lam * outs[1]
    o_ref[...] = _head_norm_out(o, ng_ref[...], lam_init).astype(o_ref.dtype)


def _attn_sample(aq, ak, av, row0, nb, t, k_past, v_past, l, slopes, lams, norm_g, lam_init):
    heads, dv = norm_g.shape
    dh = dv // 2
    past = k_past.shape[2]
    assert row0 % t == 0 and t % 8 == 0
    rb0 = row0 // t
    kp = k_past.reshape(k_past.shape[0], nb, past, heads * dv)
    vp = v_past.reshape(v_past.shape[0], nb, past, heads * dv)
    kern = functools.partial(_attn_sample_kernel, dh=dh, lam_init=lam_init)
    lam_spec = pl.BlockSpec((1, dh), lambda b, h: (0, 0))
    new_spec = pl.BlockSpec((t, dv), lambda b, h: (rb0 + b, h))
    return pl.pallas_call(
        kern,
        grid=(nb, heads),
        in_specs=[pl.BlockSpec(memory_space=pltpu.SMEM),
                  new_spec, new_spec, new_spec,
                  pl.BlockSpec((None, 1, past, dv), lambda b, h: (l, b, 0, h)),
                  pl.BlockSpec((None, 1, past, dv), lambda b, h: (l, b, 0, h)),
                  lam_spec, lam_spec, lam_spec, lam_spec,
                  pl.BlockSpec((None, 1, dv), lambda b, h: (h, 0, 0))],
        out_specs=pl.BlockSpec((t, dv), lambda b, h: (b, h)),
        out_shape=jax.ShapeDtypeStruct((nb * t, heads * dv), BF16),
        compiler_params=_params(("parallel", "parallel")),
        name="attn_sample",
    )(slopes, aq, ak, av, kp, vp, *lams, norm_g.reshape(heads, 1, dv))


def _merge_kernel(hmp_ref, hms_ref, hap_ref, has_ref, wa_ref, wb_ref, g0_ref, g1_ref, o_ref, *, nbp):
    i = pl.program_id(0)

    def body(hm_ref, ha_ref):
        ya = _dot(hm_ref[...], wa_ref[...])
        yb = _dot(ha_ref[...], wb_ref[...])
        g0 = jax.nn.sigmoid(g0_ref[...].astype(F32))
        g1 = jax.nn.sigmoid(g1_ref[...].astype(F32))
        o_ref[...] = (g0 * ya + g1 * yb).astype(o_ref.dtype)

    @pl.when(i < nbp)
    def _():
        body(hmp_ref, hap_ref)

    @pl.when(i >= nbp)
    def _():
        body(hms_ref, has_ref)


def _merge(hm_p, hm_s, ha_p, ha_s, w_a, w_b, gates, rows):
    ka, kb = hm_p.shape[1], ha_p.shape[1]
    d = w_a.shape[1]
    tm = rows.tm
    tn = _pick(d, (512, 256, 128))
    nj = d // tn
    pmap = lambda i, j: (rows.p_tile(i), 0)
    smap = lambda i, j: (rows.s_tile(i), 0)
    return pl.pallas_call(
        functools.partial(_merge_kernel, nbp=rows.nbp),
        grid=(rows.nb, nj),
        in_specs=[pl.BlockSpec((tm, ka), pmap), pl.BlockSpec((tm, ka), smap, pipeline_mode=pl.Buffered(1)),
                  pl.BlockSpec((tm, kb), pmap), pl.BlockSpec((tm, kb), smap, pipeline_mode=pl.Buffered(1)),
                  pl.BlockSpec((ka, tn), lambda i, j: (0, j)),
                  pl.BlockSpec((kb, tn), lambda i, j: (0, j)),
                  pl.BlockSpec((tm, tn), lambda i, j: (i, j)),
                  pl.BlockSpec((tm, tn), lambda i, j: (i, nj + j))],
        out_specs=pl.BlockSpec((tm, tn), lambda i, j: (i, j)),
        out_shape=jax.ShapeDtypeStruct((rows.nb * tm, d), BF16),
        compiler_params=_params(("arbitrary", "arbitrary")),
        name="merge",
    )(hm_p, hm_s, ha_p, ha_s, w_a, w_b, gates, gates)


def _outproj_kernel(a_ref, w_ref, xp_ref, xs_ref, gp_ref, gs_ref, o_ref, *, nbp):
    i = pl.program_id(0)
    y = _dot(a_ref[...], w_ref[...])

    def body(x_ref, gt_ref):
        o_ref[...] = x_ref[...] + _mod_apply(y, [gt_ref[...]], lambda y3, gt: gt * y3)

    @pl.when(i < nbp)
    def _():
        body(xp_ref, gp_ref)

    @pl.when(i >= nbp)
    def _():
        body(xs_ref, gs_ref)


def _outproj(a, w, xp, xs, mod, k_gate, rows):
    k = a.shape[1]
    d = w.shape[1]
    tm = rows.tm
    tn = _pick(d, (512, 256, 128))
    gp, gs = rows.mod_specs(k_gate, tn, col=lambda i, j: j)
    return pl.pallas_call(
        functools.partial(_outproj_kernel, nbp=rows.nbp),
        grid=(rows.nb, d // tn),
        in_specs=[pl.BlockSpec((tm, k), lambda i, j: (i, 0)),
                  pl.BlockSpec((k, tn), lambda i, j: (0, j)),
                  pl.BlockSpec((tm, tn), lambda i, j: (rows.p_tile(i), j)),
                  pl.BlockSpec((tm, tn), lambda i, j: (rows.s_tile(i), j)),
                  gp, gs],
        out_specs=pl.BlockSpec((tm, tn), lambda i, j: (i, j)),
        out_shape=jax.ShapeDtypeStruct((rows.nb * tm, d), F32),
        compiler_params=_params(("arbitrary", "arbitrary")),
        name="outproj",
    )(a, w, xp, xs, mod, mod)


def _top_values(s, k):
    vals = []
    for r in range(k):
        m = jnp.max(s, axis=0, keepdims=True)
        vals.append(m)
        if r + 1 < k:
            s = jnp.where(s == m, -jnp.inf, s)
    return vals


def _peer_select_kernel(q_ref, sk_ref, s_ref, aux_ref, *, heads, dsub):
    tm = q_ref.shape[0]
    for h in range(heads):
        tops = []
        for c in range(2):
            qh = q_ref[:, (2 * h + c) * dsub:(2 * h + c + 1) * dsub]
            sk = sk_ref[h, c]
            qh_hi = qh.astype(BF16)
            qh_lo = (qh - qh_hi.astype(F32)).astype(BF16)
            sk_hi = sk.astype(BF16)
            sk_lo = (sk - sk_hi.astype(F32)).astype(BF16)
            s = _dot_nt(sk_hi, qh_hi) + _dot_nt(sk_hi, qh_lo) + _dot_nt(sk_lo, qh_hi)
            s_ref[h, c] = s
            tops.append(_top_values(s, P_TOPK))
        t2 = jnp.concatenate(tops[1], axis=0)
        cand = jnp.concatenate([tops[0][a] + t2 for a in range(P_TOPK)], axis=0)
        best = _top_values(cand, P_TOPK)
        z = jnp.ones((1, tm), F32)
        for r in range(1, P_TOPK):
            z = z + jnp.exp(best[r] - best[0])
        rows = [best[P_TOPK - 1], tops[0][0], tops[1][0], 1.0 / z]
        aux_ref[h] = jnp.concatenate(rows + [jnp.zeros((8 - len(rows), tm), F32)], axis=0)


def _peer_select(qp, sub_keys, l):
    n = qp.shape[0]
    _, heads, _, nkeys, dsub = sub_keys.shape
    tm = _pick(n, (512, 256, 128))
    kern = functools.partial(_peer_select_kernel, heads=heads, dsub=dsub)
    return pl.pallas_call(
        kern,
        grid=(n // tm,),
        in_specs=[pl.BlockSpec((tm, qp.shape[1]), lambda i: (i, 0)),
                  pl.BlockSpec((None,) + sub_keys.shape[1:], lambda i: (l, 0, 0, 0, 0))],
        out_specs=[pl.BlockSpec((heads, 2, nkeys, tm), lambda i: (0, 0, 0, i)),
                   pl.BlockSpec((heads, 8, tm), lambda i: (0, 0, i))],
        out_shape=[jax.ShapeDtypeStruct((heads, 2, nkeys, n), F32),
                   jax.ShapeDtypeStruct((heads, 8, n), F32)],
        compiler_params=_params(("parallel",)),
        name="peer_select",
    )(qp, sub_keys)


def _peer_dense_kernel(x_ref, u_ref, v_ref, s1_ref, s2_ref, aux_ref, o_ref,
                       e1_ref, e2_ref, pcur_ref, sprev_ref, pblk_ref, *, heads, nkeys):
    te = u_ref.shape[0]
    e = pl.program_id(1)

    @pl.when(e == 0)
    def _():
        o_ref[...] = jnp.zeros(o_ref.shape, F32)
        sprev_ref[...] = jnp.zeros(sprev_ref.shape, F32)
        for h in range(heads):
            e2_ref[h] = jnp.exp(s2_ref[h] - aux_ref[h, 2:3, :]) * aux_ref[h, 3:4, :]

    nsub = te // nkeys
    tm = x_ref.shape[0]
    for h in range(heads):
        e1_ref[h, 0:nsub, :] = jnp.exp(s1_ref[h] - aux_ref[h, 1:2, :])
    for tb in range(tm // LANES):
        cs = slice(tb * LANES, (tb + 1) * LANES)
        for kb in range(nkeys // PEER_KEY_BLOCK):
            rs = slice(kb * PEER_KEY_BLOCK, (kb + 1) * PEER_KEY_BLOCK)
            w = [None] * nsub
            for h in range(heads):
                s2_blk = s2_ref[h, rs, cs]
                e2_blk = e2_ref[h, rs, cs]
                tau = aux_ref[h, 0:1, cs]
                for a in range(nsub):
                    g = jnp.where(s1_ref[h, a:a + 1, cs] + s2_blk >= tau, e1_ref[h, a:a + 1, cs] * e2_blk, 0.0)
                    w[a] = g if w[a] is None else w[a] + g
            for a in range(nsub):
                r0 = a * nkeys + kb * PEER_KEY_BLOCK
                act = jax.nn.gelu(sprev_ref[r0:r0 + PEER_KEY_BLOCK, cs], approximate=True)
                pblk_ref[tb, a, rs, :] = w[a] * act
        for a in range(nsub):
            pcur_ref[cs, a * nkeys:(a + 1) * nkeys] = pblk_ref[tb, a].T.astype(BF16)
        ntb = tm // LANES
        if ntb % 2 == 0 and tb % (ntb // 2) == ntb // 2 - 1:
            t0 = (tb // (ntb // 2)) * (tm // 2)
            sprev_ref[:, t0:t0 + tm // 2] = _dot_nt(u_ref[...], x_ref[t0:t0 + tm // 2, :])
        o_ref[cs, :] += _dot(pcur_ref[cs, :], v_ref[...])
    if (tm // LANES) % 2:
        sprev_ref[...] = _dot_nt(u_ref[...], x_ref[...])


def _peer_dense(x, u_tab, v_tab, s_t, aux):
    n, d = x.shape
    ne = u_tab.shape[0]
    heads, _, nkeys, _ = s_t.shape
    tm = _pick(n, (512, 256, 128))
    te = _pick(ne, (512, 256, 128))
    assert te % nkeys == 0 and ne == nkeys * nkeys
    n_tiles = ne // te
    nsub = te // nkeys
    last = n_tiles - 1
    tile = lambda e, lag: jnp.clip(e - lag, 0, last)
    kern = functools.partial(_peer_dense_kernel, heads=heads, nkeys=nkeys)
    once = pl.Buffered(1)
    return pl.pallas_call(
        kern,
        grid=(n // tm, n_tiles + 1),
        in_specs=[pl.BlockSpec((tm, d), lambda i, e: (i, 0), pipeline_mode=once),
                  pl.BlockSpec((te, d), lambda i, e: (tile(e, 0), 0)),
                  pl.BlockSpec((te, d), lambda i, e: (tile(e, 1), 0)),
                  pl.BlockSpec((heads, None, None, nsub, tm), lambda i, e: (0, 0, tile(e, 1), 0, i)),
                  pl.BlockSpec((heads, None, nkeys, tm), lambda i, e: (0, 1, 0, i), pipeline_mode=once),
                  pl.BlockSpec((heads, 8, tm), lambda i, e: (0, 0, i), pipeline_mode=once)],
        out_specs=pl.BlockSpec((tm, d), lambda i, e: (i, 0), pipeline_mode=once),
        out_shape=jax.ShapeDtypeStruct((n, d), F32),
        scratch_shapes=[pltpu.VMEM((heads, 8, tm), F32),
                        pltpu.VMEM((heads, nkeys, tm), F32),
                        pltpu.VMEM((tm, te), BF16),
                        pltpu.VMEM((te, tm), F32),
                        pltpu.VMEM((tm // LANES, nsub, nkeys, LANES), F32)],
        compiler_params=_params(("parallel", "arbitrary")),
        name="peer_dense",
    )(x, u_tab, v_tab, s_t.reshape(heads, 2, n_tiles, nsub, n), s_t, aux)


def _final_kernel(x_ref, p_ref, gp_ref, gs_ref, g_ref, yp_ref, ys_ref, *, nbp):
    i = pl.program_id(0)

    def body(gt_ref, y_ref):
        x = x_ref[...] + _mod_apply(p_ref[...], [gt_ref[...]], lambda p3, gt: gt * p3)
        y_ref[...] = x * lax.rsqrt(jnp.mean(x * x, -1, keepdims=True) + EPS) * g_ref[...]

    @pl.when(i < nbp)
    def _():
        body(gp_ref, yp_ref)

    @pl.when(i >= nbp)
    def _():
        body(gs_ref, ys_ref)


def _final(x, peer, mod, k_gate, g_final, rows):
    d = x.shape[1]
    tm = rows.tm
    gp, gs = rows.mod_specs(k_gate, d)
    return pl.pallas_call(
        functools.partial(_final_kernel, nbp=rows.nbp),
        grid=(rows.nb,),
        in_specs=[pl.BlockSpec((tm, d), lambda i: (i, 0)),
                  pl.BlockSpec((tm, d), lambda i: (i, 0)),
                  gp, gs,
                  pl.BlockSpec((1, d), lambda i: (0, 0))],
        out_specs=[pl.BlockSpec((tm, d), lambda i: (rows.p_tile(i), 0)),
                   pl.BlockSpec((tm, d), lambda i: (rows.s_tile(i), 0))],
        out_shape=[jax.ShapeDtypeStruct((rows.nbp * tm, d), F32),
                   jax.ShapeDtypeStruct((rows.nbs * tm, d), F32)],
        compiler_params=_params(("arbitrary",)),
        name="final_norm",
    )(x, peer, mod, mod, g_final.reshape(1, d))


def kernel(x_prompt, x_sample, cache_k, cache_v, state_C, state_n, state_m, state_conv, c_prompt, c_sample, w_ada, b_ada, g_mix, w_in, conv_w, conv_b, b_gates, m_norm_g, lam_q1, lam_k1, lam_q2, lam_k2, a_norm_g, w_a, w_b, w_out, g_ffn, w_pq, sub_keys, u_tab, v_tab, g_final):
    bp, tp, d = x_prompt.shape
    bs, ts, _ = x_sample.shape
    depth = w_in.shape[0]
    assert depth == 1, "the per-layer state plumbing below is written for a single layer"
    l = 0
    heads, dv = m_norm_g.shape[1:]
    dk = state_C.shape[-1]
    a_heads, a_dv = a_norm_g.shape[1:]
    dh = a_dv // 2
    n_p, n_s = bp * tp, bs * ts
    xp = x_prompt.reshape(n_p, d)
    xs = x_sample.reshape(n_s, d)
    rows_ew = _Rows(bp, tp, bs, ts, _pick(math.gcd(tp, n_s), (256, 128, 64, 32)))
    rows_mm = _Rows(bp, tp, bs, ts, _pick(math.gcd(tp, n_s), (1024, 512, 256, 128, 64, 32)))

    qk_w, v_w = 2 * heads * dk, heads * dv
    a_w = a_heads * a_dv
    col_mv = qk_w
    col_gate = qk_w + 2 * v_w
    col_aq = col_gate + 2 * heads
    col_ak = col_aq + a_w
    col_av = col_ak + a_w
    col_bg = col_av + a_w

    c_all = jnp.concatenate([c_sample, c_prompt], axis=0)
    c_all = jnp.pad(c_all, ((0, (-c_all.shape[0]) % 8), (0, 0)))
    mod = _ada(c_all, w_ada, b_ada, l).reshape(c_all.shape[0], 6, 1, d)

    hn = _norm_mod(xp, xs, 0, rows_ew, g_mix[l], mod, 1, 0)
    qk_raw, = _proj(hn, w_in, l, 0, qk_w, "f32", n_p)
    vo, = _proj(hn, w_in, l, col_mv, 2 * v_w, "bf16", n_p)
    aq, = _proj(hn, w_in, l, col_aq, a_w, "bf16", n_p, scale=LOG2E * dh ** -0.5)
    ak, k_p, k_s = _proj(hn, w_in, l, col_ak, a_w, "kv", n_p)
    av, v_p, v_s = _proj(hn, w_in, l, col_av, a_w, "kv", n_p)
    bgates, = _proj(hn, w_in, l, col_bg, 2 * d, "bf16", n_p)
    gates = _gate_proj(hn, w_in, l, col_gate)

    zc = jnp.zeros((bp, heads, dv, dk), F32)
    zn = jnp.zeros((bp, heads, dk), F32)
    zm = jnp.zeros((bp, heads), F32)
    zconv = jnp.zeros((bp, CONV_W - 1, qk_w), F32)
    mp = _mlstm(qk_raw, vo, gates, 0, bp, tp, zconv, conv_w[l], conv_b[l], b_gates[l], m_norm_g[l], zc, zn, zm)
    ms = _mlstm(qk_raw, vo, gates, n_p, bs, ts, state_conv[l], conv_w[l], conv_b[l], b_gates[l], m_norm_g[l],
                state_C[l], state_n[l], state_m[l])

    lam_init = 0.8 - 0.6 * math.exp(-0.3 * l)
    slopes = jnp.array([2.0 ** (-8.0 * (h + 1) / a_heads) for h in range(a_heads)], F32)
    lams = [a[l].reshape(1, -1) for a in (lam_q1, lam_k1, lam_q2, lam_k2)]
    ha_p = _attn_prompt(aq, ak, av, bp, tp, slopes, lams, a_norm_g[l], lam_init)
    ha_s = _attn_sample(aq, ak, av, n_p, bs, ts, cache_k, cache_v, l, slopes, lams, a_norm_g[l], lam_init)

    merged = _merge(mp[0], ms[0], ha_p, ha_s, _cast_bf16(w_a, l), _cast_bf16(w_b, l), bgates, rows_mm)
    x1 = _outproj(merged, _cast_bf16(w_out, l), xp, xs, mod, 2, rows_mm)

    hn2 = _norm_mod(x1, x1, rows_ew.nbp, rows_ew, g_ffn[l], mod, 4, 3)
    qp = _matmul(hn2, _cast_bf16(w_pq, l), F32, "peer_query")
    s_t, aux = _peer_select(qp, sub_keys, l)
    peer = _peer_dense(hn2, _cast_bf16(u_tab, l), _cast_bf16(v_tab, l), s_t, aux)
    y_p, y_s = _final(x1, peer, mod, 5, g_final, rows_ew)

    return (y_p.reshape(bp, tp, d), y_s.reshape(bs, ts, d),
            k_p.reshape(1, bp, tp, a_heads, 2, dh), v_p.reshape(1, bp, tp, a_heads, a_dv),
            mp[1][None], mp[2][None], mp[3][None], mp[4][None],
            k_s.reshape(1, bs, ts, a_heads, 2, dh), v_s.reshape(1, bs, ts, a_heads, a_dv),
            ms[1][None], ms[2][None], ms[3][None], ms[4][None])
```

```python
import functools
import math

import jax
import jax.numpy as jnp
from jax import lax
from jax.experimental import pallas as pl
from jax.experimental.pallas import tpu as pltpu

F32 = jnp.float32
BF16 = jnp.bfloat16

EPS = 1e-6
ATTN_CHUNK = 64
ATTN_KEY_BLOCK = 512
CONV_W = 4
P_TOPK = 16
PEER_KEY_BLOCK = 32
PEER_SPLIT = 2
LANES = 128
MLSTM_CHUNK = 256
VMEM_LIMIT_MB = 56
LOG2E = math.log2(math.e)


def _params(sem, vmem_mb=VMEM_LIMIT_MB):
    return pltpu.CompilerParams(dimension_semantics=sem, vmem_limit_bytes=vmem_mb << 20)


def _pick(n, cands):
    for c in cands:
        if n % c == 0:
            return c
    raise ValueError(f"no tile in {cands} divides {n}")


def _dot(a, b):
    return jnp.dot(a, b, preferred_element_type=F32)


def _dot_nt(a, b):
    return lax.dot_general(a, b, (((1,), (1,)), ((), ())), preferred_element_type=F32)


def _dot_tn(a, b):
    return lax.dot_general(a, b, (((0,), (0,)), ((), ())), preferred_element_type=F32)


def _split3(x):
    hi = x.astype(BF16)
    r = x - hi.astype(F32)
    mid = r.astype(BF16)
    lo = (r - mid.astype(F32)).astype(BF16)
    return hi, mid, lo


def _dot_exact_lhs(a01, x):
    hi, mid, lo = _split3(x)
    return _dot(a01, hi) + _dot(a01, mid) + _dot(a01, lo)


class _Rows:
    def __init__(self, bp, tp, bs, ts, tm):
        assert (bp * tp) % tm == 0 and (bs * ts) % tm == 0
        assert tp % tm == 0 and tm % ts == 0, "a prompt tile sits in one sequence, a sample tile holds whole ones"
        self.bp, self.tp, self.bs, self.ts, self.tm = bp, tp, bs, ts, tm
        self.nbp = bp * tp // tm
        self.nbs = bs * ts // tm
        self.nb = self.nbp + self.nbs
        self.ng = tm // ts

    def p_tile(self, i):
        return jnp.minimum(i, self.nbp - 1)

    def s_tile(self, i):
        return jnp.maximum(i - self.nbp, 0)

    def mod_specs(self, k, width, col=None):
        cj = (lambda *g: 0) if col is None else col
        p = pl.BlockSpec((None, None, 1, width),
                         lambda *g: (self.bs + (self.p_tile(g[0]) * self.tm) // self.tp, k, 0, cj(*g)))
        s = pl.BlockSpec((self.ng, None, 1, width), lambda *g: (self.s_tile(g[0]), k, 0, cj(*g)))
        return p, s


def _mod_apply(x, gate_rows, fn):
    tm, w = x.shape
    rows = [r.reshape((-1, 1, w)) for r in gate_rows]
    ng = rows[0].shape[0]
    return fn(x.reshape(ng, tm // ng, w), *rows).reshape(tm, w)


def _cast_kernel(w_ref, o_ref):
    o_ref[...] = w_ref[...].astype(o_ref.dtype)


def _cast_bf16(w3, l):
    _, r, c = w3.shape
    tr = _pick(r, (512, 256, 128))
    tc = _pick(c, (2048, 1024, 512, 256, 128))
    return pl.pallas_call(
        _cast_kernel,
        grid=(r // tr, c // tc),
        in_specs=[pl.BlockSpec((None, tr, tc), lambda i, j: (l, i, j))],
        out_specs=pl.BlockSpec((tr, tc), lambda i, j: (i, j)),
        out_shape=jax.ShapeDtypeStruct((r, c), BF16),
        compiler_params=_params(("parallel", "parallel")),
        name="cast_bf16",
    )(w3)


def _ada_kernel(c_ref, w_ref, b_ref, o_ref):
    c = c_ref[...]
    a = (c * jax.nn.sigmoid(c)).astype(BF16)
    o_ref[...] = _dot(a, w_ref[...].astype(BF16)) + b_ref[...]


def _ada(c_all, w_ada, b_ada, l):
    m, d = c_all.shape
    n = w_ada.shape[2]
    tn = _pick(n, (512, 256, 128))
    return pl.pallas_call(
        _ada_kernel,
        grid=(n // tn,),
        in_specs=[pl.BlockSpec((m, d), lambda j: (0, 0)),
                  pl.BlockSpec((None, d, tn), lambda j: (l, 0, j)),
                  pl.BlockSpec((None, 1, tn), lambda j: (l, 0, j))],
        out_specs=pl.BlockSpec((m, tn), lambda j: (0, j)),
        out_shape=jax.ShapeDtypeStruct((m, n), F32),
        compiler_params=_params(("parallel",)),
        name="ada_mod",
    )(c_all, w_ada, b_ada.reshape(b_ada.shape[0], 1, n))


def _norm_mod_kernel(xp_ref, xs_ref, g_ref, scp_ref, shp_ref, scs_ref, shs_ref, o_ref, *, nbp):
    i = pl.program_id(0)

    def body(x_ref, sc_ref, sh_ref):
        x = x_ref[...]
        y = x * lax.rsqrt(jnp.mean(x * x, -1, keepdims=True) + EPS) * g_ref[...]
        y = _mod_apply(y, [sc_ref[...], sh_ref[...]], lambda y3, sc, sh: y3 * (1.0 + sc) + sh)
        o_ref[...] = y.astype(o_ref.dtype)

    @pl.when(i < nbp)
    def _():
        body(xp_ref, scp_ref, shp_ref)

    @pl.when(i >= nbp)
    def _():
        body(xs_ref, scs_ref, shs_ref)


def _norm_mod(xp, xs, s_off, rows, g, mod, k_scale, k_shift):
    d = xp.shape[1]
    tm = rows.tm
    scp, scs = rows.mod_specs(k_scale, d)
    shp, shs = rows.mod_specs(k_shift, d)
    return pl.pallas_call(
        functools.partial(_norm_mod_kernel, nbp=rows.nbp),
        grid=(rows.nb,),
        in_specs=[pl.BlockSpec((tm, d), lambda i: (rows.p_tile(i), 0)),
                  pl.BlockSpec((tm, d), lambda i: (s_off + rows.s_tile(i), 0)),
                  pl.BlockSpec((1, d), lambda i: (0, 0)),
                  scp, shp, scs, shs],
        out_specs=pl.BlockSpec((tm, d), lambda i: (i, 0)),
        out_shape=jax.ShapeDtypeStruct((rows.nb * tm, d), BF16),
        compiler_params=_params(("arbitrary",)),
        name="norm_mod",
    )(xp, xs, g.reshape(1, d), mod, mod, mod, mod)


def _proj_kernel(x_ref, w_ref, wn_ref, *refs, shift, scale, mode, nip):
    outs, wb_ref = refs[:-1], refs[-1]
    i = pl.program_id(1)

    @pl.when(i == 0)
    def _():
        tn = w_ref.shape[0]
        cr = _pick(tn, (256, 128))
        for r in range(0, tn, cr):
            if r + cr + shift <= tn:
                wb_ref[r:r + cr, :] = w_ref[r + shift:r + cr + shift, :].astype(BF16)
            else:
                wb_ref[r:r + cr - shift, :] = w_ref[r + shift:r + cr, :].astype(BF16)
                wb_ref[r + cr - shift:r + cr, :] = wn_ref[...].astype(BF16)

    acc = _dot_nt(x_ref[...], wb_ref[...])
    if scale != 1.0:
        acc = acc * scale
    if mode in ("kv", "kvt"):
        outs[0][...] = (acc.T if mode == "kvt" else acc).astype(BF16)

        @pl.when(i < nip)
        def _():
            outs[1][...] = acc

        @pl.when(i >= nip)
        def _():
            outs[2][...] = acc
    else:
        outs[0][...] = acc.astype(outs[0].dtype)


def _proj(hn, w_t, l, col0, ncols, mode, n_p, scale=1.0):
    n, k = hn.shape
    tm = _pick(math.gcd(n_p, n - n_p), ((1024,) if mode != "kvt" else ()) + (ATTN_KEY_BLOCK, 256, 128))
    tn = _pick(ncols, ((1024,) if mode == "bf16" else ()) + (512, 256, 128))
    shift = col0 % tn
    assert (col0 - shift) % tn == 0 and shift % 16 == 0 and shift < 256
    cb0 = (col0 - shift) // tn
    nip = n_p // tm
    ni, nj = n // tm, ncols // tn
    once = pl.Buffered(1)
    nxt = shift if shift else 16
    in_specs = [pl.BlockSpec((tm, k), lambda j, i: (i, 0)),
                pl.BlockSpec((None, tn, k), lambda j, i: (l, cb0 + j, 0), pipeline_mode=once),
                pl.BlockSpec((None, nxt, k), lambda j, i: (l, (cb0 + j + 1) * (tn // nxt) if shift else 0, 0),
                             pipeline_mode=once)]
    all_spec = pl.BlockSpec((tm, tn), lambda j, i: (i, j))
    if mode in ("kv", "kvt"):
        first = (pl.BlockSpec((None, tn, tm), lambda j, i: (i, j, 0)) if mode == "kvt" else all_spec)
        out_specs = [first,
                     pl.BlockSpec((tm, tn), lambda j, i: (jnp.minimum(i, nip - 1), j)),
                     pl.BlockSpec((tm, tn), lambda j, i: (jnp.maximum(i - nip, 0), j))]
        out_shape = [jax.ShapeDtypeStruct((ni, ncols, tm) if mode == "kvt" else (n, ncols), BF16),
                     jax.ShapeDtypeStruct((n_p, ncols), F32),
                     jax.ShapeDtypeStruct((n - n_p, ncols), F32)]
    else:
        out_specs = [all_spec]
        out_shape = [jax.ShapeDtypeStruct((n, ncols), F32 if mode == "f32" else BF16)]
    kern = functools.partial(_proj_kernel, shift=shift, scale=scale, mode=mode, nip=nip)
    return pl.pallas_call(
        kern,
        grid=(nj, ni),
        in_specs=in_specs,
        out_specs=out_specs,
        out_shape=out_shape,
        scratch_shapes=[pltpu.VMEM((tn, k), BF16)],
        compiler_params=_params(("arbitrary", "arbitrary")),
        name="proj_" + mode,
    )(hn, w_t, w_t)


def _gate_kernel(x_ref, w_ref, o_ref):
    o_ref[...] = _dot_nt(x_ref[...], w_ref[...].astype(BF16))


def _gate_proj(hn, w_t, l, col0):
    n, k = hn.shape
    assert col0 % LANES == 0
    tm = _pick(n, (1024, 512, 256, 128))
    return pl.pallas_call(
        _gate_kernel,
        grid=(n // tm,),
        in_specs=[pl.BlockSpec((tm, k), lambda i: (i, 0)),
                  pl.BlockSpec((None, LANES, k), lambda i: (l, col0 // LANES, 0))],
        out_specs=pl.BlockSpec((tm, LANES), lambda i: (i, 0)),
        out_shape=jax.ShapeDtypeStruct((n, LANES), F32),
        compiler_params=_params(("parallel",)),
        name="gate_proj",
    )(hn, w_t)


def _mm_kernel(a_ref, b_ref, o_ref):
    o_ref[...] = _dot(a_ref[...], b_ref[...]).astype(o_ref.dtype)


def _matmul(a, b, out_dtype, name):
    m, k = a.shape
    n = b.shape[1]
    tm = _pick(m, (1024, 512, 256, 128))
    tn = _pick(n, (1024, 512, 256, 128))
    return pl.pallas_call(
        _mm_kernel,
        grid=(m // tm, n // tn),
        in_specs=[pl.BlockSpec((tm, k), lambda i, j: (i, 0)),
                  pl.BlockSpec((k, tn), lambda i, j: (0, j))],
        out_specs=pl.BlockSpec((tm, tn), lambda i, j: (i, j)),
        out_shape=jax.ShapeDtypeStruct((m, n), out_dtype),
        compiler_params=_params(("parallel", "parallel")),
        name=name,
    )(a, b)


def _mlstm_kernel(qk_ref, v_ref, og_ref, gt_ref, conv0_ref, cw_ref, cb_ref, bg_ref, ng_ref,
                  c0_ref, n0_ref, m0_ref,
                  hm_ref, c_ref, n_ref, m_ref, convn_ref, full_ref, *, heads, dk, dv):
    L = qk_ref.shape[0]
    pad = 8
    hist = CONV_W - 1
    ci = pl.program_id(1)

    @pl.when(ci == 0)
    def _():
        c_ref[...] = c0_ref[...]
        n_ref[...] = n0_ref[...]
        m_ref[...] = m0_ref[...]
        full_ref[pad - hist:pad, :] = conv0_ref[0]

    u = qk_ref[...]
    full_ref[pad:pad + L, :] = u
    acc = cb_ref[...] + u * cw_ref[hist:hist + 1, :]
    for j in range(hist):
        acc = acc + full_ref[pad - hist + j:pad - hist + j + L, :] * cw_ref[j:j + 1, :]
    tail = full_ref[pad + L - hist:pad + L, :]
    convn_ref[0] = tail
    full_ref[pad - hist:pad, :] = tail
    qk = acc * jax.nn.sigmoid(acc)

    gates = gt_ref[...]
    row = lax.broadcasted_iota(jnp.int32, (L, L), 0)
    col = lax.broadcasted_iota(jnp.int32, (L, L), 1)
    causal = col <= row
    tril01 = jnp.where(causal, 1.0, 0.0).astype(BF16)
    ones01 = jnp.ones((L, L), BF16)

    for h in range(heads):
        q = qk[:, h * dk:(h + 1) * dk]
        k = qk[:, (heads + h) * dk:(heads + h + 1) * dk] * (dk ** -0.5)
        vb = v_ref[:, h * dv:(h + 1) * dv]
        v = vb.astype(F32)
        qb, kb = q.astype(BF16), k.astype(BF16)
        i_col = gates[:, h:h + 1] + bg_ref[0:1, h:h + 1]
        f_col = gates[:, heads + h:heads + h + 1] + bg_ref[1:2, h:h + 1]
        logf = jnp.minimum(f_col, 0.0) - jnp.log1p(jnp.exp(-jnp.abs(f_col)))
        lf_b = jnp.broadcast_to(logf, (L, L))
        ig_b = jnp.broadcast_to(i_col, (L, L))
        d1 = _dot_exact_lhs(tril01, lf_b)
        d2 = _dot_exact_lhs(ones01, jnp.where(row == col, ig_b, 0.0) - jnp.where(row <= col, lf_b, 0.0))
        dmat = jnp.where(causal, d1 + d2, -jnp.inf)
        m_prev = m_ref[0, h:h + 1, 0:1]
        b_col = d1[:, 0:1]
        inter = b_col + m_prev
        m_t = jnp.maximum(inter, jnp.max(dmat, axis=1, keepdims=True))
        s = _dot_nt(qb, kb) * jnp.exp(dmat - m_t)
        w_prev = jnp.exp(inter - m_t)
        c_old = c_ref[0, h]
        n_old = n_ref[0, h:h + 1, :]
        num = _dot(s.astype(BF16), vb) + w_prev * _dot_nt(qb, c_old.astype(BF16))
        den = jnp.sum(s, axis=1, keepdims=True) + w_prev * jnp.sum(q * n_old, axis=1, keepdims=True)
        hh = num / jnp.maximum(jnp.abs(den), jnp.exp(-m_t))
        m_new = m_t[L - 1:L, :]
        b_last = b_col[L - 1:L, :]
        decay = jnp.exp(b_last + m_prev - m_new)
        w_src = jnp.exp(b_last - b_col + i_col - m_new)
        c_ref[0, h] = decay * c_old + _dot_tn((w_src * v).astype(BF16), kb)
        n_ref[0, h:h + 1, :] = decay * n_old + jnp.sum(w_src * k, axis=0, keepdims=True)
        m_ref[0, h:h + 1, :] = jnp.broadcast_to(m_new, (1, m_ref.shape[2]))
        hn = hh * lax.rsqrt(jnp.mean(hh * hh, -1, keepdims=True) + EPS) * ng_ref[h:h + 1, :]
        hn = hn * jax.nn.sigmoid(og_ref[:, h * dv:(h + 1) * dv].astype(F32))
        hm_ref[:, h * dv:(h + 1) * dv] = hn.astype(hm_ref.dtype)


def _mlstm(qk_raw, vo, gates, row0, nb, t, conv0, conv_w, conv_b, b_gates, norm_g, c0, n0, m0):
    heads, dv = norm_g.shape
    dk = c0.shape[-1]
    qkw = 2 * heads * dk
    vw = heads * dv
    L = _pick(t, (MLSTM_CHUNK, 128, 64, 32, 16, 8))
    nc = t // L
    assert row0 % L == 0 and L >= CONV_W - 1
    rb0 = row0 // L
    m0b = jnp.broadcast_to(m0[:, :, None], (nb, heads, LANES))
    rmap = lambda b, c: rb0 + b * nc + c
    kern = functools.partial(_mlstm_kernel, heads=heads, dk=dk, dv=dv)
    hm, c1, n1, m1, convn = pl.pallas_call(
        kern,
        grid=(nb, nc),
        in_specs=[pl.BlockSpec((L, qkw), lambda b, c: (rmap(b, c), 0)),
                  pl.BlockSpec((L, vw), lambda b, c: (rmap(b, c), 0)),
                  pl.BlockSpec((L, vw), lambda b, c: (rmap(b, c), 1)),
                  pl.BlockSpec((L, gates.shape[1]), lambda b, c: (rmap(b, c), 0)),
                  pl.BlockSpec((1, CONV_W - 1, qkw), lambda b, c: (b, 0, 0)),
                  pl.BlockSpec((CONV_W, qkw), lambda b, c: (0, 0)),
                  pl.BlockSpec((1, qkw), lambda b, c: (0, 0)),
                  pl.BlockSpec((2, heads), lambda b, c: (0, 0)),
                  pl.BlockSpec((heads, dv), lambda b, c: (0, 0)),
                  pl.BlockSpec((1, heads, dv, dk), lambda b, c: (b, 0, 0, 0)),
                  pl.BlockSpec((1, heads, dk), lambda b, c: (b, 0, 0)),
                  pl.BlockSpec((1, heads, LANES), lambda b, c: (b, 0, 0))],
        out_specs=[pl.BlockSpec((L, vw), lambda b, c: (b * nc + c, 0)),
                   pl.BlockSpec((1, heads, dv, dk), lambda b, c: (b, 0, 0, 0)),
                   pl.BlockSpec((1, heads, dk), lambda b, c: (b, 0, 0)),
                   pl.BlockSpec((1, heads, LANES), lambda b, c: (b, 0, 0)),
                   pl.BlockSpec((1, CONV_W - 1, qkw), lambda b, c: (b, 0, 0))],
        out_shape=[jax.ShapeDtypeStruct((nb * t, vw), BF16),
                   jax.ShapeDtypeStruct((nb, heads, dv, dk), F32),
                   jax.ShapeDtypeStruct((nb, heads, dk), F32),
                   jax.ShapeDtypeStruct((nb, heads, LANES), F32),
                   jax.ShapeDtypeStruct((nb, CONV_W - 1, qkw), F32)],
        scratch_shapes=[pltpu.VMEM((8 + L, qkw), F32)],
        compiler_params=_params(("parallel", "arbitrary")),
        name="mlstm",
    )(qk_raw, vo, vo, gates, conv0, conv_w, conv_b.reshape(1, qkw), b_gates, norm_g, c0, n0, m0b)
    return hm, c1, n1, m1[:, :, 0], convn


def _lambda(lq1, lk1, lq2, lk2, lam_init):
    return (jnp.exp(jnp.sum(lq1[...] * lk1[...], axis=1, keepdims=True))
            - jnp.exp(jnp.sum(lq2[...] * lk2[...], axis=1, keepdims=True)) + lam_init)


def _head_norm_out(o, g_row, lam_init):
    return o * lax.rsqrt(jnp.mean(o * o, -1, keepdims=True) + EPS) * g_row * (1.0 - lam_init)


def _attn_prompt_kernel(slope_ref, q_ref, k_ref, vt_ref, lq1, lk1, lq2, lk2, ng_ref, o_ref,
                        m_ref, l_ref, acc_ref, bias_ref, dbias_ref, *, dh, lam_init):
    tq = q_ref.shape[0]
    tk = vt_ref.shape[2]
    h = pl.program_id(1)
    qi = pl.program_id(2)
    slope2 = slope_ref[h] * LOG2E

    @pl.when(qi == 0)
    def _():
        krow = lax.broadcasted_iota(jnp.int32, (tk, tq), 0)
        qcol = lax.broadcasted_iota(jnp.int32, (tk, tq), 1)
        base = slope2 * krow.astype(F32)
        bias_ref[...] = base
        ahead = jnp.maximum(krow - qcol, 0).astype(F32)
        vis = (krow // ATTN_CHUNK) <= (qcol // ATTN_CHUNK)
        dbias_ref[...] = jnp.where(vis, base - (2.0 * slope2) * ahead, -jnp.inf)

    m_ref[...] = jnp.full(m_ref.shape, -jnp.inf, F32)
    l_ref[...] = jnp.zeros(l_ref.shape, F32)
    acc_ref[...] = jnp.zeros(acc_ref.shape, F32)

    def update(kj, b_ref):
        k0 = pl.multiple_of(kj * tk, tk)
        kb = k_ref[pl.ds(k0, tk), :]
        vt = vt_ref[kj]
        shift = slope2 * ((kj * tk - qi * tq).astype(F32))
        ss = [_dot_nt(kb[:, c * dh:(c + 1) * dh], q_ref[:, c * dh:(c + 1) * dh]) for c in range(2)]
        ps, alphas = [], []
        for c in range(2):
            s = ss[c] + b_ref[...]
            m_old = m_ref[c]
            m_new = jnp.maximum(m_old, jnp.max(s, axis=0, keepdims=True) + shift)
            alpha = jnp.exp2(m_old - m_new)
            p = jnp.exp2(s - (m_new - shift))
            l_ref[c] = alpha * l_ref[c] + jnp.sum(p, axis=0, keepdims=True)
            m_ref[c] = m_new
            ps.append(p.astype(BF16))
            alphas.append(alpha)
        for c in range(2):
            acc_ref[c] = alphas[c] * acc_ref[c] + _dot(vt, ps[c])

    def body(kj, carry):
        update(kj, bias_ref)
        return carry

    lax.fori_loop(0, qi, body, 0)
    update(qi, dbias_ref)
    lam = _lambda(lq1, lk1, lq2, lk2, lam_init)
    o = acc_ref[0] / l_ref[0] - lam * (acc_ref[1] / l_ref[1])
    o = o * lax.rsqrt(jnp.mean(o * o, axis=0, keepdims=True) + EPS)
    o_ref[...] = (o.T * ng_ref[...] * (1.0 - lam_init)).astype(o_ref.dtype)


def _attn_prompt(aq, ak, av_t, nb, t, slopes, lams, norm_g, lam_init):
    heads, dv = norm_g.shape
    dh = dv // 2
    tk = av_t.shape[2]
    tq = tk
    nq = t // tq
    assert t % tq == 0 and tq % ATTN_CHUNK == 0 and tq % LANES == 0
    kern = functools.partial(_attn_prompt_kernel, dh=dh, lam_init=lam_init)
    lam_spec = pl.BlockSpec((1, dh), lambda b, h, i: (0, 0))
    return pl.pallas_call(
        kern,
        grid=(nb, heads, nq),
        in_specs=[pl.BlockSpec(memory_space=pltpu.SMEM),
                  pl.BlockSpec((tq, dv), lambda b, h, i: (b * nq + i, h)),
                  pl.BlockSpec((t, dv), lambda b, h, i: (b, h)),
                  pl.BlockSpec((nq, dv, tk), lambda b, h, i: (b, h, 0)),
                  lam_spec, lam_spec, lam_spec, lam_spec,
                  pl.BlockSpec((None, 1, dv), lambda b, h, i: (h, 0, 0))],
        out_specs=pl.BlockSpec((tq, dv), lambda b, h, i: (b * nq + i, h)),
        out_shape=jax.ShapeDtypeStruct((nb * t, heads * dv), BF16),
        scratch_shapes=[pltpu.VMEM((2, 1, tq), F32),
                        pltpu.VMEM((2, 1, tq), F32),
                        pltpu.VMEM((2, dv, tq), F32),
                        pltpu.VMEM((tk, tq), F32),
                        pltpu.VMEM((tk, tq), F32)],
        compiler_params=_params(("arbitrary", "arbitrary", "arbitrary")),
        name="attn_prompt",
    )(slopes, aq, ak, av_t, *lams, norm_g.reshape(heads, 1, dv))


def _attn_sample_kernel(slope_ref, q_ref, kn_ref, vn_ref, kp_ref, vp_ref, lq1, lk1, lq2, lk2, ng_ref,
                        o_ref, *, dh, lam_init):
    t = q_ref.shape[0]
    past = kp_ref.shape[1]
    h = pl.program_id(1)
    slope2 = slope_ref[h] * LOG2E
    qb = q_ref[...]
    knb = kn_ref[...].astype(BF16)
    vnb = vn_ref[...].astype(BF16)
    kpb = kp_ref[0]
    vpb = vp_ref[0]
    rowp = past + lax.broadcasted_iota(jnp.int32, (t, 1), 0)

    def bias_vis(colp):
        dist = jnp.abs((rowp - colp).astype(F32))
        return -slope2 * dist, (colp // ATTN_CHUNK) <= (rowp // ATTN_CHUNK)

    bias_p, vis_p = bias_vis(lax.broadcasted_iota(jnp.int32, (1, past), 1))
    bias_n, vis_n = bias_vis(past + lax.broadcasted_iota(jnp.int32, (1, t), 1))
    outs = []
    for c in range(2):
        qc = qb[:, c * dh:(c + 1) * dh]
        sp = jnp.where(vis_p, _dot_nt(qc, kpb[:, c * dh:(c + 1) * dh]) + bias_p, -jnp.inf)
        sn = jnp.where(vis_n, _dot_nt(qc, knb[:, c * dh:(c + 1) * dh]) + bias_n, -jnp.inf)
        m = jnp.maximum(jnp.max(sp, axis=1, keepdims=True), jnp.max(sn, axis=1, keepdims=True))
        pp = jnp.exp2(sp - m)
        pn = jnp.exp2(sn - m)
        l = jnp.sum(pp, axis=1, keepdims=True) + jnp.sum(pn, axis=1, keepdims=True)
        outs.append((_dot(pp.astype(BF16), vpb) + _dot(pn.astype(BF16), vnb)) / l)
    lam = _lambda(lq1, lk1, lq2, lk2, lam_init)
    o = outs[0] - lam * outs[1]
    o_ref[...] = _head_norm_out(o, ng_ref[...], lam_init).astype(o_ref.dtype)


def _attn_sample(aq, k_new, v_new, row0, nb, t, k_past, v_past, l, slopes, lams, norm_g, lam_init):
    heads, dv = norm_g.shape
    dh = dv // 2
    past = k_past.shape[2]
    assert row0 % t == 0 and t % 8 == 0
    rb0 = row0 // t
    kp = k_past.astype(BF16).reshape(k_past.shape[0], nb, past, heads * dv)
    vp = v_past.astype(BF16).reshape(v_past.shape[0], nb, past, heads * dv)
    kern = functools.partial(_attn_sample_kernel, dh=dh, lam_init=lam_init)
    lam_spec = pl.BlockSpec((1, dh), lambda b, h: (0, 0))
    new_spec = pl.BlockSpec((t, dv), lambda b, h: (b, h))
    return pl.pallas_call(
        kern,
        grid=(nb, heads),
        in_specs=[pl.BlockSpec(memory_space=pltpu.SMEM),
                  pl.BlockSpec((t, dv), lambda b, h: (rb0 + b, h)), new_spec, new_spec,
                  pl.BlockSpec((None, 1, past, dv), lambda b, h: (l, b, 0, h)),
                  pl.BlockSpec((None, 1, past, dv), lambda b, h: (l, b, 0, h)),
                  lam_spec, lam_spec, lam_spec, lam_spec,
                  pl.BlockSpec((None, 1, dv), lambda b, h: (h, 0, 0))],
        out_specs=pl.BlockSpec((t, dv), lambda b, h: (b, h)),
        out_shape=jax.ShapeDtypeStruct((nb * t, heads * dv), BF16),
        compiler_params=_params(("parallel", "parallel")),
        name="attn_sample",
    )(slopes, aq, k_new, v_new, kp, vp, *lams, norm_g.reshape(heads, 1, dv))


def _merge_kernel(hmp_ref, hms_ref, hap_ref, has_ref, wa_ref, wb_ref, g0_ref, g1_ref, o_ref, *, nbp):
    i = pl.program_id(0)

    def body(hm_ref, ha_ref):
        ya = _dot(hm_ref[...], wa_ref[...])
        yb = _dot(ha_ref[...], wb_ref[...])
        g0 = jax.nn.sigmoid(g0_ref[...].astype(F32))
        g1 = jax.nn.sigmoid(g1_ref[...].astype(F32))
        o_ref[...] = (g0 * ya + g1 * yb).astype(o_ref.dtype)

    @pl.when(i < nbp)
    def _():
        body(hmp_ref, hap_ref)

    @pl.when(i >= nbp)
    def _():
        body(hms_ref, has_ref)


def _merge(hm_p, hm_s, ha_p, ha_s, w_a, w_b, gates, rows):
    ka, kb = hm_p.shape[1], ha_p.shape[1]
    d = w_a.shape[1]
    tm = rows.tm
    tn = _pick(d, (512, 256, 128))
    nj = d // tn
    pmap = lambda i, j: (rows.p_tile(i), 0)
    smap = lambda i, j: (rows.s_tile(i), 0)
    return pl.pallas_call(
        functools.partial(_merge_kernel, nbp=rows.nbp),
        grid=(rows.nb, nj),
        in_specs=[pl.BlockSpec((tm, ka), pmap), pl.BlockSpec((tm, ka), smap, pipeline_mode=pl.Buffered(1)),
                  pl.BlockSpec((tm, kb), pmap), pl.BlockSpec((tm, kb), smap, pipeline_mode=pl.Buffered(1)),
                  pl.BlockSpec((ka, tn), lambda i, j: (0, j)),
                  pl.BlockSpec((kb, tn), lambda i, j: (0, j)),
                  pl.BlockSpec((tm, tn), lambda i, j: (i, j)),
                  pl.BlockSpec((tm, tn), lambda i, j: (i, nj + j))],
        out_specs=pl.BlockSpec((tm, tn), lambda i, j: (i, j)),
        out_shape=jax.ShapeDtypeStruct((rows.nb * tm, d), BF16),
        compiler_params=_params(("arbitrary", "arbitrary")),
        name="merge",
    )(hm_p, hm_s, ha_p, ha_s, w_a, w_b, gates, gates)


def _outproj_kernel(a_ref, w_ref, xp_ref, xs_ref, gp_ref, gs_ref, o_ref, *, nbp):
    i = pl.program_id(0)
    y = _dot(a_ref[...], w_ref[...])

    def body(x_ref, gt_ref):
        o_ref[...] = x_ref[...] + _mod_apply(y, [gt_ref[...]], lambda y3, gt: gt * y3)

    @pl.when(i < nbp)
    def _():
        body(xp_ref, gp_ref)

    @pl.when(i >= nbp)
    def _():
        body(xs_ref, gs_ref)


def _outproj(a, w, xp, xs, mod, k_gate, rows):
    k = a.shape[1]
    d = w.shape[1]
    tm = rows.tm
    tn = _pick(d, (512, 256, 128))
    gp, gs = rows.mod_specs(k_gate, tn, col=lambda i, j: j)
    return pl.pallas_call(
        functools.partial(_outproj_kernel, nbp=rows.nbp),
        grid=(rows.nb, d // tn),
        in_specs=[pl.BlockSpec((tm, k), lambda i, j: (i, 0)),
                  pl.BlockSpec((k, tn), lambda i, j: (0, j)),
                  pl.BlockSpec((tm, tn), lambda i, j: (rows.p_tile(i), j)),
                  pl.BlockSpec((tm, tn), lambda i, j: (rows.s_tile(i), j)),
                  gp, gs],
        out_specs=pl.BlockSpec((tm, tn), lambda i, j: (i, j)),
        out_shape=jax.ShapeDtypeStruct((rows.nb * tm, d), F32),
        compiler_params=_params(("arbitrary", "arbitrary")),
        name="outproj",
    )(a, w, xp, xs, mod, mod)


def _top_values(s, k):
    vals = []
    for r in range(k):
        m = jnp.max(s, axis=0, keepdims=True)
        vals.append(m)
        if r + 1 < k:
            s = jnp.where(s == m, -jnp.inf, s)
    return vals


def _peer_select_kernel(q_ref, sk_ref, s_ref, aux_ref, *, heads, dsub):
    tm = q_ref.shape[0]
    for h in range(heads):
        tops = []
        for c in range(2):
            qh = q_ref[:, (2 * h + c) * dsub:(2 * h + c + 1) * dsub]
            sk = sk_ref[h, c]
            qh_hi = qh.astype(BF16)
            qh_lo = (qh - qh_hi.astype(F32)).astype(BF16)
            sk_hi = sk.astype(BF16)
            sk_lo = (sk - sk_hi.astype(F32)).astype(BF16)
            s = _dot_nt(sk_hi, qh_hi) + _dot_nt(sk_hi, qh_lo) + _dot_nt(sk_lo, qh_hi)
            s_ref[h, c] = s
            tops.append(_top_values(s, P_TOPK))
        t2 = jnp.concatenate(tops[1], axis=0)
        cand = jnp.concatenate([tops[0][a] + t2 for a in range(P_TOPK)], axis=0)
        best = _top_values(cand, P_TOPK)
        z = jnp.ones((1, tm), F32)
        for r in range(1, P_TOPK):
            z = z + jnp.exp(best[r] - best[0])
        rows = [best[P_TOPK - 1], tops[0][0], tops[1][0], 1.0 / z]
        aux_ref[h] = jnp.concatenate(rows + [jnp.zeros((8 - len(rows), tm), F32)], axis=0)


def _peer_select(qp, sub_keys, l):
    n = qp.shape[0]
    _, heads, _, nkeys, dsub = sub_keys.shape
    tm = _pick(n, (512, 256, 128))
    kern = functools.partial(_peer_select_kernel, heads=heads, dsub=dsub)
    return pl.pallas_call(
        kern,
        grid=(n // tm,),
        in_specs=[pl.BlockSpec((tm, qp.shape[1]), lambda i: (i, 0)),
                  pl.BlockSpec((None,) + sub_keys.shape[1:], lambda i: (l, 0, 0, 0, 0))],
        out_specs=[pl.BlockSpec((heads, 2, nkeys, tm), lambda i: (0, 0, 0, i)),
                   pl.BlockSpec((heads, 8, tm), lambda i: (0, 0, i))],
        out_shape=[jax.ShapeDtypeStruct((heads, 2, nkeys, n), F32),
                   jax.ShapeDtypeStruct((heads, 8, n), F32)],
        compiler_params=_params(("parallel",)),
        name="peer_select",
    )(qp, sub_keys)


def _peer_dense_kernel(x_ref, u_ref, v_ref, s1_ref, s2_ref, aux_ref, o_ref,
                       s1t_ref, e1_ref, e2_ref, p_ref, pblk_ref, *, heads, nkeys):
    te = u_ref.shape[0]
    tm = x_ref.shape[0]
    e = pl.program_id(1)
    nsub = te // nkeys
    grows = te // PEER_SPLIT
    gsub = nsub // PEER_SPLIT

    @pl.when(e == 0)
    def _():
        o_ref[...] = jnp.zeros(o_ref.shape, F32)
        for h in range(heads):
            e2_ref[h] = jnp.exp(s2_ref[h] - aux_ref[h, 2:3, :]) * aux_ref[h, 3:4, :]

    scores = [_dot_nt(u_ref[g * grows:(g + 1) * grows, :], x_ref[...]) for g in range(PEER_SPLIT)]

    grp = e % (8 // nsub)
    for h in range(heads):
        rows = s1_ref[h, 0:nsub, :]
        for gi in range(1, 8 // nsub):
            rows = jnp.where(grp == gi, s1_ref[h, gi * nsub:(gi + 1) * nsub, :], rows)
        s1t_ref[h, 0:nsub, :] = rows
        e1_ref[h, 0:nsub, :] = jnp.exp(rows - aux_ref[h, 1:2, :])

    for g in range(PEER_SPLIT):
        for tb in range(tm // LANES):
            cs = slice(tb * LANES, (tb + 1) * LANES)
            for kb in range(nkeys // PEER_KEY_BLOCK):
                rs = slice(kb * PEER_KEY_BLOCK, (kb + 1) * PEER_KEY_BLOCK)
                w = [None] * gsub
                for h in range(heads):
                    s2_blk = s2_ref[h, rs, cs]
                    e2_blk = e2_ref[h, rs, cs]
                    tau = aux_ref[h, 0:1, cs]
                    for a in range(gsub):
                        ar = g * gsub + a
                        hit = s1t_ref[h, ar:ar + 1, cs] + s2_blk >= tau
                        gate = jnp.where(hit, e1_ref[h, ar:ar + 1, cs] * e2_blk, 0.0)
                        w[a] = gate if w[a] is None else w[a] + gate
                for a in range(gsub):
                    r0 = a * nkeys + kb * PEER_KEY_BLOCK
                    act = jax.nn.gelu(scores[g][r0:r0 + PEER_KEY_BLOCK, cs], approximate=True)
                    pblk_ref[tb, g * gsub + a, rs, :] = w[a] * act
            for a in range(gsub):
                ar = g * gsub + a
                p_ref[cs, ar * nkeys:(ar + 1) * nkeys] = pblk_ref[tb, ar].T.astype(BF16)
        o_ref[...] += _dot(p_ref[:, g * grows:(g + 1) * grows], v_ref[g * grows:(g + 1) * grows, :])


def _peer_dense(x, u_tab, v_tab, s_t, aux):
    n, d = x.shape
    ne = u_tab.shape[0]
    heads, _, nkeys, _ = s_t.shape
    tm = _pick(n, (512, 256, 128))
    te = _pick(ne, (512, 256, 128))
    assert te % nkeys == 0 and ne == nkeys * nkeys
    n_tiles = ne // te
    nsub = te // nkeys
    assert 8 % nsub == 0 and nkeys % 8 == 0 and nsub % PEER_SPLIT == 0
    kern = functools.partial(_peer_dense_kernel, heads=heads, nkeys=nkeys)
    once = pl.Buffered(1)
    return pl.pallas_call(
        kern,
        grid=(n // tm, n_tiles),
        in_specs=[pl.BlockSpec((tm, d), lambda i, e: (i, 0), pipeline_mode=once),
                  pl.BlockSpec((te, d), lambda i, e: (e, 0)),
                  pl.BlockSpec((te, d), lambda i, e: (e, 0)),
                  pl.BlockSpec((heads, None, 8, tm), lambda i, e: (0, 0, e * nsub // 8, i)),
                  pl.BlockSpec((heads, None, nkeys, tm), lambda i, e: (0, 1, 0, i), pipeline_mode=once),
                  pl.BlockSpec((heads, 8, tm), lambda i, e: (0, 0, i), pipeline_mode=once)],
        out_specs=pl.BlockSpec((tm, d), lambda i, e: (i, 0), pipeline_mode=once),
        out_shape=jax.ShapeDtypeStruct((n, d), F32),
        scratch_shapes=[pltpu.VMEM((heads, 8, tm), F32),
                        pltpu.VMEM((heads, 8, tm), F32),
                        pltpu.VMEM((heads, nkeys, tm), F32),
                        pltpu.VMEM((tm, te), BF16),
                        pltpu.VMEM((tm // LANES, nsub, nkeys, LANES), F32)],
        compiler_params=_params(("parallel", "arbitrary")),
        name="peer_dense",
    )(x, u_tab, v_tab, s_t, s_t, aux)


def _final_kernel(x_ref, p_ref, gp_ref, gs_ref, g_ref, yp_ref, ys_ref, *, nbp):
    i = pl.program_id(0)

    def body(gt_ref, y_ref):
        x = x_ref[...] + _mod_apply(p_ref[...], [gt_ref[...]], lambda p3, gt: gt * p3)
        y_ref[...] = x * lax.rsqrt(jnp.mean(x * x, -1, keepdims=True) + EPS) * g_ref[...]

    @pl.when(i < nbp)
    def _():
        body(gp_ref, yp_ref)

    @pl.when(i >= nbp)
    def _():
        body(gs_ref, ys_ref)


def _final(x, peer, mod, k_gate, g_final, rows):
    d = x.shape[1]
    tm = rows.tm
    gp, gs = rows.mod_specs(k_gate, d)
    return pl.pallas_call(
        functools.partial(_final_kernel, nbp=rows.nbp),
        grid=(rows.nb,),
        in_specs=[pl.BlockSpec((tm, d), lambda i: (i, 0)),
                  pl.BlockSpec((tm, d), lambda i: (i, 0)),
                  gp, gs,
                  pl.BlockSpec((1, d), lambda i: (0, 0))],
        out_specs=[pl.BlockSpec((tm, d), lambda i: (rows.p_tile(i), 0)),
                   pl.BlockSpec((tm, d), lambda i: (rows.s_tile(i), 0))],
        out_shape=[jax.ShapeDtypeStruct((rows.nbp * tm, d), F32),
                   jax.ShapeDtypeStruct((rows.nbs * tm, d), F32)],
        compiler_params=_params(("arbitrary",)),
        name="final_norm",
    )(x, peer, mod, mod, g_final.reshape(1, d))


def kernel(x_prompt, x_sample, cache_k, cache_v, state_C, state_n, state_m, state_conv, c_prompt, c_sample, w_ada, b_ada, g_mix, w_in, conv_w, conv_b, b_gates, m_norm_g, lam_q1, lam_k1, lam_q2, lam_k2, a_norm_g, w_a, w_b, w_out, g_ffn, w_pq, sub_keys, u_tab, v_tab, g_final):
    bp, tp, d = x_prompt.shape
    bs, ts, _ = x_sample.shape
    depth = w_in.shape[0]
    assert depth == 1, "the per-layer state plumbing below is written for a single layer"
    l = 0
    heads, dv = m_norm_g.shape[1:]
    dk = state_C.shape[-1]
    a_heads, a_dv = a_norm_g.shape[1:]
    dh = a_dv // 2
    n_p, n_s = bp * tp, bs * ts
    xp = x_prompt.reshape(n_p, d)
    xs = x_sample.reshape(n_s, d)
    rows_ew = _Rows(bp, tp, bs, ts, _pick(math.gcd(tp, n_s), (256, 128, 64, 32)))
    rows_mm = _Rows(bp, tp, bs, ts, _pick(math.gcd(tp, n_s), (1024, 512, 256, 128, 64, 32)))

    qk_w, v_w = 2 * heads * dk, heads * dv
    a_w = a_heads * a_dv
    col_mv = qk_w
    col_gate = qk_w + 2 * v_w
    col_aq = col_gate + 2 * heads
    col_ak = col_aq + a_w
    col_av = col_ak + a_w
    col_bg = col_av + a_w

    c_all = jnp.concatenate([c_sample, c_prompt], axis=0)
    c_all = jnp.pad(c_all, ((0, (-c_all.shape[0]) % 8), (0, 0)))
    mod = _ada(c_all, w_ada, b_ada, l).reshape(c_all.shape[0], 6, 1, d)

    hn = _norm_mod(xp, xs, 0, rows_ew, g_mix[l], mod, 1, 0)
    w_t = jnp.swapaxes(w_in, 1, 2)
    qk_raw, = _proj(hn, w_t, l, 0, qk_w, "f32", n_p)
    vo, = _proj(hn, w_t, l, col_mv, 2 * v_w, "bf16", n_p)
    aq, = _proj(hn, w_t, l, col_aq, a_w, "bf16", n_p, scale=LOG2E * dh ** -0.5)
    ak, k_p, k_s = _proj(hn, w_t, l, col_ak, a_w, "kv", n_p)
    av_t, v_p, v_s = _proj(hn, w_t, l, col_av, a_w, "kvt", n_p)
    bgates, = _proj(hn, w_t, l, col_bg, 2 * d, "bf16", n_p)
    gates = _gate_proj(hn, w_t, l, col_gate)

    zc = jnp.zeros((bp, heads, dv, dk), F32)
    zn = jnp.zeros((bp, heads, dk), F32)
    zm = jnp.zeros((bp, heads), F32)
    zconv = jnp.zeros((bp, CONV_W - 1, qk_w), F32)
    mp = _mlstm(qk_raw, vo, gates, 0, bp, tp, zconv, conv_w[l], conv_b[l], b_gates[l], m_norm_g[l], zc, zn, zm)
    ms = _mlstm(qk_raw, vo, gates, n_p, bs, ts, state_conv[l], conv_w[l], conv_b[l], b_gates[l], m_norm_g[l],
                state_C[l], state_n[l], state_m[l])

    lam_init = 0.8 - 0.6 * math.exp(-0.3 * l)
    slopes = jnp.array([2.0 ** (-8.0 * (h + 1) / a_heads) for h in range(a_heads)], F32)
    lams = [a[l].reshape(1, -1) for a in (lam_q1, lam_k1, lam_q2, lam_k2)]
    ha_p = _attn_prompt(aq, ak, av_t, bp, tp, slopes, lams, a_norm_g[l], lam_init)
    ha_s = _attn_sample(aq, k_s, v_s, n_p, bs, ts, cache_k, cache_v, l, slopes, lams, a_norm_g[l], lam_init)

    merged = _merge(mp[0], ms[0], ha_p, ha_s, _cast_bf16(w_a, l), _cast_bf16(w_b, l), bgates, rows_mm)
    x1 = _outproj(merged, _cast_bf16(w_out, l), xp, xs, mod, 2, rows_mm)

    hn2 = _norm_mod(x1, x1, rows_ew.nbp, rows_ew, g_ffn[l], mod, 4, 3)
    qp = _matmul(hn2, _cast_bf16(w_pq, l), F32, "peer_query")
    s_t, aux = _peer_select(qp, sub_keys, l)
    peer = _peer_dense(hn2, _cast_bf16(u_tab, l), _cast_bf16(v_tab, l), s_t, aux)
    y_p, y_s = _final(x1, peer, mod, 5, g_final, rows_ew)

    return (y_p.reshape(bp, tp, d), y_s.reshape(bs, ts, d),
            k_p.reshape(1, bp, tp, a_heads, 2, dh), v_p.reshape(1, bp, tp, a_heads, a_dv),
            mp[1][None], mp[2][None], mp[3][None], mp[4][None],
            k_s.reshape(1, bs, ts, a_heads, 2, dh), v_s.reshape(1, bs, ts, a_heads, a_dv),
            ms[1][None], ms[2][None], ms[3][None], ms[4][None])
```

```python
import functools
import math

import jax
import jax.numpy as jnp
from jax import lax
from jax.experimental import pallas as pl
from jax.experimental.pallas import tpu as pltpu

F32 = jnp.float32
BF16 = jnp.bfloat16

EPS = 1e-6
ATTN_CHUNK = 64
ATTN_KEY_BLOCK = 512
CONV_W = 4
P_TOPK = 16
PEER_KEY_BLOCK = 32
PEER_SPLIT = 2
LANES = 128
MLSTM_CHUNK = 256
VMEM_LIMIT_MB = 56
LOG2E = math.log2(math.e)


def _params(sem, vmem_mb=VMEM_LIMIT_MB):
    return pltpu.CompilerParams(dimension_semantics=sem, vmem_limit_bytes=vmem_mb << 20)


def _pick(n, cands):
    for c in cands:
        if n % c == 0:
            return c
    raise ValueError(f"no tile in {cands} divides {n}")


def _dot(a, b):
    return jnp.dot(a, b, preferred_element_type=F32)


def _dot_nt(a, b):
    return lax.dot_general(a, b, (((1,), (1,)), ((), ())), preferred_element_type=F32)


def _dot_tn(a, b):
    return lax.dot_general(a, b, (((0,), (0,)), ((), ())), preferred_element_type=F32)


def _split3(x):
    hi = x.astype(BF16)
    r = x - hi.astype(F32)
    mid = r.astype(BF16)
    lo = (r - mid.astype(F32)).astype(BF16)
    return hi, mid, lo


def _dot_exact_lhs(a01, x):
    hi, mid, lo = _split3(x)
    return _dot(a01, hi) + _dot(a01, mid) + _dot(a01, lo)


class _Rows:
    def __init__(self, bp, tp, bs, ts, tm):
        assert (bp * tp) % tm == 0 and (bs * ts) % tm == 0
        assert tp % tm == 0 and tm % ts == 0, "a prompt tile sits in one sequence, a sample tile holds whole ones"
        self.bp, self.tp, self.bs, self.ts, self.tm = bp, tp, bs, ts, tm
        self.nbp = bp * tp // tm
        self.nbs = bs * ts // tm
        self.nb = self.nbp + self.nbs
        self.ng = tm // ts

    def p_tile(self, i):
        return jnp.minimum(i, self.nbp - 1)

    def s_tile(self, i):
        return jnp.maximum(i - self.nbp, 0)

    def mod_specs(self, k, width, col=None):
        cj = (lambda *g: 0) if col is None else col
        p = pl.BlockSpec((None, None, 1, width),
                         lambda *g: (self.bs + (self.p_tile(g[0]) * self.tm) // self.tp, k, 0, cj(*g)))
        s = pl.BlockSpec((self.ng, None, 1, width), lambda *g: (self.s_tile(g[0]), k, 0, cj(*g)))
        return p, s


def _mod_apply(x, gate_rows, fn):
    tm, w = x.shape
    rows = [r.reshape((-1, 1, w)) for r in gate_rows]
    ng = rows[0].shape[0]
    return fn(x.reshape(ng, tm // ng, w), *rows).reshape(tm, w)


def _cast_kernel(w_ref, o_ref):
    o_ref[...] = w_ref[...].astype(o_ref.dtype)


def _cast_bf16(w3, l):
    _, r, c = w3.shape
    tr = _pick(r, (512, 256, 128))
    tc = _pick(c, (2048, 1024, 512, 256, 128))
    return pl.pallas_call(
        _cast_kernel,
        grid=(r // tr, c // tc),
        in_specs=[pl.BlockSpec((None, tr, tc), lambda i, j: (l, i, j))],
        out_specs=pl.BlockSpec((tr, tc), lambda i, j: (i, j)),
        out_shape=jax.ShapeDtypeStruct((r, c), BF16),
        compiler_params=_params(("parallel", "parallel")),
        name="cast_bf16",
    )(w3)


def _ada_kernel(c_ref, w_ref, b_ref, o_ref):
    c = c_ref[...]
    a = (c * jax.nn.sigmoid(c)).astype(BF16)
    o_ref[...] = _dot(a, w_ref[...].astype(BF16)) + b_ref[...]


def _ada(c_all, w_ada, b_ada, l):
    m, d = c_all.shape
    n = w_ada.shape[2]
    tn = _pick(n, (512, 256, 128))
    return pl.pallas_call(
        _ada_kernel,
        grid=(n // tn,),
        in_specs=[pl.BlockSpec((m, d), lambda j: (0, 0)),
                  pl.BlockSpec((None, d, tn), lambda j: (l, 0, j)),
                  pl.BlockSpec((None, 1, tn), lambda j: (l, 0, j))],
        out_specs=pl.BlockSpec((m, tn), lambda j: (0, j)),
        out_shape=jax.ShapeDtypeStruct((m, n), F32),
        compiler_params=_params(("parallel",)),
        name="ada_mod",
    )(c_all, w_ada, b_ada.reshape(b_ada.shape[0], 1, n))


def _norm_mod_kernel(xp_ref, xs_ref, g_ref, scp_ref, shp_ref, scs_ref, shs_ref, o_ref, *, nbp):
    i = pl.program_id(0)

    def body(x_ref, sc_ref, sh_ref):
        x = x_ref[...]
        y = x * lax.rsqrt(jnp.mean(x * x, -1, keepdims=True) + EPS) * g_ref[...]
        y = _mod_apply(y, [sc_ref[...], sh_ref[...]], lambda y3, sc, sh: y3 * (1.0 + sc) + sh)
        o_ref[...] = y.astype(o_ref.dtype)

    @pl.when(i < nbp)
    def _():
        body(xp_ref, scp_ref, shp_ref)

    @pl.when(i >= nbp)
    def _():
        body(xs_ref, scs_ref, shs_ref)


def _norm_mod(xp, xs, s_off, rows, g, mod, k_scale, k_shift):
    d = xp.shape[1]
    tm = rows.tm
    scp, scs = rows.mod_specs(k_scale, d)
    shp, shs = rows.mod_specs(k_shift, d)
    return pl.pallas_call(
        functools.partial(_norm_mod_kernel, nbp=rows.nbp),
        grid=(rows.nb,),
        in_specs=[pl.BlockSpec((tm, d), lambda i: (rows.p_tile(i), 0)),
                  pl.BlockSpec((tm, d), lambda i: (s_off + rows.s_tile(i), 0)),
                  pl.BlockSpec((1, d), lambda i: (0, 0)),
                  scp, shp, scs, shs],
        out_specs=pl.BlockSpec((tm, d), lambda i: (i, 0)),
        out_shape=jax.ShapeDtypeStruct((rows.nb * tm, d), BF16),
        compiler_params=_params(("arbitrary",)),
        name="norm_mod",
    )(xp, xs, g.reshape(1, d), mod, mod, mod, mod)


def _proj_kernel(x_ref, w_ref, wn_ref, *refs, shift, scale, mode, nip):
    outs, wb_ref = refs[:-1], refs[-1]
    i = pl.program_id(1)

    @pl.when(i == 0)
    def _():
        tn = w_ref.shape[0]
        cr = _pick(tn, (256, 128))
        for r in range(0, tn, cr):
            if r + cr + shift <= tn:
                wb_ref[r:r + cr, :] = w_ref[r + shift:r + cr + shift, :].astype(BF16)
            else:
                wb_ref[r:r + cr - shift, :] = w_ref[r + shift:r + cr, :].astype(BF16)
                wb_ref[r + cr - shift:r + cr, :] = wn_ref[...].astype(BF16)

    acc = _dot_nt(x_ref[...], wb_ref[...])
    if scale != 1.0:
        acc = acc * scale
    if mode in ("kv", "kvt"):
        outs[0][...] = (acc.T if mode == "kvt" else acc).astype(BF16)

        @pl.when(i < nip)
        def _():
            outs[1][...] = acc

        @pl.when(i >= nip)
        def _():
            outs[2][...] = acc
    else:
        outs[0][...] = acc.astype(outs[0].dtype)


def _proj(hn, w_t, l, col0, ncols, mode, n_p, scale=1.0):
    n, k = hn.shape
    tm = _pick(math.gcd(n_p, n - n_p), ((1024,) if mode != "kvt" else ()) + (ATTN_KEY_BLOCK, 256, 128))
    tn = _pick(ncols, ((1024,) if mode == "bf16" else ()) + (512, 256, 128))
    shift = col0 % tn
    assert (col0 - shift) % tn == 0 and shift % 16 == 0 and shift < 256
    cb0 = (col0 - shift) // tn
    nip = n_p // tm
    ni, nj = n // tm, ncols // tn
    once = pl.Buffered(1)
    nxt = shift if shift else 16
    in_specs = [pl.BlockSpec((tm, k), lambda j, i: (i, 0)),
                pl.BlockSpec((None, tn, k), lambda j, i: (l, cb0 + j, 0), pipeline_mode=once),
                pl.BlockSpec((None, nxt, k), lambda j, i: (l, (cb0 + j + 1) * (tn // nxt) if shift else 0, 0),
                             pipeline_mode=once)]
    all_spec = pl.BlockSpec((tm, tn), lambda j, i: (i, j))
    if mode in ("kv", "kvt"):
        first = (pl.BlockSpec((None, tn, tm), lambda j, i: (i, j, 0)) if mode == "kvt" else all_spec)
        out_specs = [first,
                     pl.BlockSpec((tm, tn), lambda j, i: (jnp.minimum(i, nip - 1), j)),
                     pl.BlockSpec((tm, tn), lambda j, i: (jnp.maximum(i - nip, 0), j))]
        out_shape = [jax.ShapeDtypeStruct((ni, ncols, tm) if mode == "kvt" else (n, ncols), BF16),
                     jax.ShapeDtypeStruct((n_p, ncols), F32),
                     jax.ShapeDtypeStruct((n - n_p, ncols), F32)]
    else:
        out_specs = [all_spec]
        out_shape = [jax.ShapeDtypeStruct((n, ncols), F32 if mode == "f32" else BF16)]
    kern = functools.partial(_proj_kernel, shift=shift, scale=scale, mode=mode, nip=nip)
    return pl.pallas_call(
        kern,
        grid=(nj, ni),
        in_specs=in_specs,
        out_specs=out_specs,
        out_shape=out_shape,
        scratch_shapes=[pltpu.VMEM((tn, k), BF16)],
        compiler_params=_params(("arbitrary", "arbitrary")),
        name="proj_" + mode,
    )(hn, w_t, w_t)


def _gate_kernel(x_ref, w_ref, o_ref):
    o_ref[...] = _dot_nt(x_ref[...], w_ref[...].astype(BF16))


def _gate_proj(hn, w_t, l, col0):
    n, k = hn.shape
    assert col0 % LANES == 0
    tm = _pick(n, (1024, 512, 256, 128))
    return pl.pallas_call(
        _gate_kernel,
        grid=(n // tm,),
        in_specs=[pl.BlockSpec((tm, k), lambda i: (i, 0)),
                  pl.BlockSpec((None, LANES, k), lambda i: (l, col0 // LANES, 0))],
        out_specs=pl.BlockSpec((tm, LANES), lambda i: (i, 0)),
        out_shape=jax.ShapeDtypeStruct((n, LANES), F32),
        compiler_params=_params(("parallel",)),
        name="gate_proj",
    )(hn, w_t)


def _mm_kernel(a_ref, b_ref, o_ref):
    o_ref[...] = _dot(a_ref[...], b_ref[...]).astype(o_ref.dtype)


def _matmul(a, b, out_dtype, name):
    m, k = a.shape
    n = b.shape[1]
    tm = _pick(m, (1024, 512, 256, 128))
    tn = _pick(n, (1024, 512, 256, 128))
    return pl.pallas_call(
        _mm_kernel,
        grid=(m // tm, n // tn),
        in_specs=[pl.BlockSpec((tm, k), lambda i, j: (i, 0)),
                  pl.BlockSpec((k, tn), lambda i, j: (0, j))],
        out_specs=pl.BlockSpec((tm, tn), lambda i, j: (i, j)),
        out_shape=jax.ShapeDtypeStruct((m, n), out_dtype),
        compiler_params=_params(("parallel", "parallel")),
        name=name,
    )(a, b)


def _mlstm_kernel(qk_ref, v_ref, og_ref, gt_ref, conv0_ref, cw_ref, cb_ref, bg_ref, ng_ref,
                  c0_ref, n0_ref, m0_ref,
                  hm_ref, c_ref, n_ref, m_ref, convn_ref, full_ref, *, heads, dk, dv):
    L = qk_ref.shape[0]
    pad = 8
    hist = CONV_W - 1
    ci = pl.program_id(1)

    @pl.when(ci == 0)
    def _():
        c_ref[...] = c0_ref[...]
        n_ref[...] = n0_ref[...]
        m_ref[...] = m0_ref[...]
        full_ref[pad - hist:pad, :] = conv0_ref[0]

    u = qk_ref[...]
    full_ref[pad:pad + L, :] = u
    acc = cb_ref[...] + u * cw_ref[hist:hist + 1, :]
    for j in range(hist):
        acc = acc + full_ref[pad - hist + j:pad - hist + j + L, :] * cw_ref[j:j + 1, :]
    tail = full_ref[pad + L - hist:pad + L, :]
    convn_ref[0] = tail
    full_ref[pad - hist:pad, :] = tail
    qk = acc * jax.nn.sigmoid(acc)

    gates = gt_ref[...]
    row = lax.broadcasted_iota(jnp.int32, (L, L), 0)
    col = lax.broadcasted_iota(jnp.int32, (L, L), 1)
    causal = col <= row
    tril01 = jnp.where(causal, 1.0, 0.0).astype(BF16)
    ones01 = jnp.ones((L, L), BF16)

    for h in range(heads):
        q = qk[:, h * dk:(h + 1) * dk]
        k = qk[:, (heads + h) * dk:(heads + h + 1) * dk] * (dk ** -0.5)
        vb = v_ref[:, h * dv:(h + 1) * dv]
        v = vb.astype(F32)
        qb, kb = q.astype(BF16), k.astype(BF16)
        i_col = gates[:, h:h + 1] + bg_ref[0:1, h:h + 1]
        f_col = gates[:, heads + h:heads + h + 1] + bg_ref[1:2, h:h + 1]
        logf = jnp.minimum(f_col, 0.0) - jnp.log1p(jnp.exp(-jnp.abs(f_col)))
        lf_b = jnp.broadcast_to(logf, (L, L))
        ig_b = jnp.broadcast_to(i_col, (L, L))
        d1 = _dot_exact_lhs(tril01, lf_b)
        d2 = _dot_exact_lhs(ones01, jnp.where(row == col, ig_b, 0.0) - jnp.where(row <= col, lf_b, 0.0))
        dmat = jnp.where(causal, d1 + d2, -jnp.inf)
        m_prev = m_ref[0, h:h + 1, 0:1]
        b_col = d1[:, 0:1]
        inter = b_col + m_prev
        m_t = jnp.maximum(inter, jnp.max(dmat, axis=1, keepdims=True))
        s = _dot_nt(qb, kb) * jnp.exp(dmat - m_t)
        w_prev = jnp.exp(inter - m_t)
        c_old = c_ref[0, h]
        n_old = n_ref[0, h:h + 1, :]
        num = _dot(s.astype(BF16), vb) + w_prev * _dot_nt(qb, c_old.astype(BF16))
        den = jnp.sum(s, axis=1, keepdims=True) + w_prev * jnp.sum(q * n_old, axis=1, keepdims=True)
        hh = num / jnp.maximum(jnp.abs(den), jnp.exp(-m_t))
        m_new = m_t[L - 1:L, :]
        b_last = b_col[L - 1:L, :]
        decay = jnp.exp(b_last + m_prev - m_new)
        w_src = jnp.exp(b_last - b_col + i_col - m_new)
        c_ref[0, h] = decay * c_old + _dot_tn((w_src * v).astype(BF16), kb)
        n_ref[0, h:h + 1, :] = decay * n_old + jnp.sum(w_src * k, axis=0, keepdims=True)
        m_ref[0, h:h + 1, :] = jnp.broadcast_to(m_new, (1, m_ref.shape[2]))
        hn = hh * lax.rsqrt(jnp.mean(hh * hh, -1, keepdims=True) + EPS) * ng_ref[h:h + 1, :]
        hn = hn * jax.nn.sigmoid(og_ref[:, h * dv:(h + 1) * dv].astype(F32))
        hm_ref[:, h * dv:(h + 1) * dv] = hn.astype(hm_ref.dtype)


def _mlstm(qk_raw, vo, gates, row0, nb, t, conv0, conv_w, conv_b, b_gates, norm_g, c0, n0, m0):
    heads, dv = norm_g.shape
    dk = c0.shape[-1]
    qkw = 2 * heads * dk
    vw = heads * dv
    L = _pick(t, (MLSTM_CHUNK, 128, 64, 32, 16, 8))
    nc = t // L
    assert row0 % L == 0 and L >= CONV_W - 1
    rb0 = row0 // L
    m0b = jnp.broadcast_to(m0[:, :, None], (nb, heads, LANES))
    rmap = lambda b, c: rb0 + b * nc + c
    kern = functools.partial(_mlstm_kernel, heads=heads, dk=dk, dv=dv)
    hm, c1, n1, m1, convn = pl.pallas_call(
        kern,
        grid=(nb, nc),
        in_specs=[pl.BlockSpec((L, qkw), lambda b, c: (rmap(b, c), 0)),
                  pl.BlockSpec((L, vw), lambda b, c: (rmap(b, c), 0)),
                  pl.BlockSpec((L, vw), lambda b, c: (rmap(b, c), 1)),
                  pl.BlockSpec((L, gates.shape[1]), lambda b, c: (rmap(b, c), 0)),
                  pl.BlockSpec((1, CONV_W - 1, qkw), lambda b, c: (b, 0, 0)),
                  pl.BlockSpec((CONV_W, qkw), lambda b, c: (0, 0)),
                  pl.BlockSpec((1, qkw), lambda b, c: (0, 0)),
                  pl.BlockSpec((2, heads), lambda b, c: (0, 0)),
                  pl.BlockSpec((heads, dv), lambda b, c: (0, 0)),
                  pl.BlockSpec((1, heads, dv, dk), lambda b, c: (b, 0, 0, 0)),
                  pl.BlockSpec((1, heads, dk), lambda b, c: (b, 0, 0)),
                  pl.BlockSpec((1, heads, LANES), lambda b, c: (b, 0, 0))],
        out_specs=[pl.BlockSpec((L, vw), lambda b, c: (b * nc + c, 0)),
                   pl.BlockSpec((1, heads, dv, dk), lambda b, c: (b, 0, 0, 0)),
                   pl.BlockSpec((1, heads, dk), lambda b, c: (b, 0, 0)),
                   pl.BlockSpec((1, heads, LANES), lambda b, c: (b, 0, 0)),
                   pl.BlockSpec((1, CONV_W - 1, qkw), lambda b, c: (b, 0, 0))],
        out_shape=[jax.ShapeDtypeStruct((nb * t, vw), BF16),
                   jax.ShapeDtypeStruct((nb, heads, dv, dk), F32),
                   jax.ShapeDtypeStruct((nb, heads, dk), F32),
                   jax.ShapeDtypeStruct((nb, heads, LANES), F32),
                   jax.ShapeDtypeStruct((nb, CONV_W - 1, qkw), F32)],
        scratch_shapes=[pltpu.VMEM((8 + L, qkw), F32)],
        compiler_params=_params(("parallel", "arbitrary")),
        name="mlstm",
    )(qk_raw, vo, vo, gates, conv0, conv_w, conv_b.reshape(1, qkw), b_gates, norm_g, c0, n0, m0b)
    return hm, c1, n1, m1[:, :, 0], convn


def _lambda(lq1, lk1, lq2, lk2, lam_init):
    return (jnp.exp(jnp.sum(lq1[...] * lk1[...], axis=1, keepdims=True))
            - jnp.exp(jnp.sum(lq2[...] * lk2[...], axis=1, keepdims=True)) + lam_init)


def _head_norm_out(o, g_row, lam_init):
    return o * lax.rsqrt(jnp.mean(o * o, -1, keepdims=True) + EPS) * g_row * (1.0 - lam_init)


def _attn_prompt_kernel(slope_ref, q_ref, k_ref, vt_ref, lq1, lk1, lq2, lk2, ng_ref, o_ref,
                        m_ref, l_ref, acc_ref, bias_ref, dbias_ref, *, dh, lam_init):
    tq = q_ref.shape[0]
    tk = vt_ref.shape[2]
    h = pl.program_id(1)
    qi = pl.program_id(2)
    slope2 = slope_ref[h] * LOG2E

    @pl.when(qi == 0)
    def _():
        krow = lax.broadcasted_iota(jnp.int32, (tk, tq), 0)
        qcol = lax.broadcasted_iota(jnp.int32, (tk, tq), 1)
        base = slope2 * krow.astype(F32)
        bias_ref[...] = base
        ahead = jnp.maximum(krow - qcol, 0).astype(F32)
        vis = (krow // ATTN_CHUNK) <= (qcol // ATTN_CHUNK)
        dbias_ref[...] = jnp.where(vis, base - (2.0 * slope2) * ahead, -jnp.inf)

    m_ref[...] = jnp.full(m_ref.shape, -jnp.inf, F32)
    l_ref[...] = jnp.zeros(l_ref.shape, F32)
    acc_ref[...] = jnp.zeros(acc_ref.shape, F32)

    def update(kj, b_ref):
        k0 = pl.multiple_of(kj * tk, tk)
        kb = k_ref[pl.ds(k0, tk), :]
        vt = vt_ref[kj]
        shift = slope2 * ((kj * tk - qi * tq).astype(F32))
        ss = [_dot_nt(kb[:, c * dh:(c + 1) * dh], q_ref[:, c * dh:(c + 1) * dh]) for c in range(2)]
        ps, alphas = [], []
        for c in range(2):
            s = ss[c] + b_ref[...]
            m_old = m_ref[c]
            m_new = jnp.maximum(m_old, jnp.max(s, axis=0, keepdims=True) + shift)
            alpha = jnp.exp2(m_old - m_new)
            p = jnp.exp2(s - (m_new - shift))
            l_ref[c] = alpha * l_ref[c] + jnp.sum(p, axis=0, keepdims=True)
            m_ref[c] = m_new
            ps.append(p.astype(BF16))
            alphas.append(alpha)
        for c in range(2):
            acc_ref[c] = alphas[c] * acc_ref[c] + _dot(vt, ps[c])

    def body(kj, carry):
        update(kj, bias_ref)
        return carry

    lax.fori_loop(0, qi, body, 0)
    update(qi, dbias_ref)
    lam = _lambda(lq1, lk1, lq2, lk2, lam_init)
    o = acc_ref[0] / l_ref[0] - lam * (acc_ref[1] / l_ref[1])
    o = o * lax.rsqrt(jnp.mean(o * o, axis=0, keepdims=True) + EPS)
    o_ref[...] = (o.T * ng_ref[...] * (1.0 - lam_init)).astype(o_ref.dtype)


def _attn_prompt(aq, ak, av_t, nb, t, slopes, lams, norm_g, lam_init):
    heads, dv = norm_g.shape
    dh = dv // 2
    tk = av_t.shape[2]
    tq = tk
    nq = t // tq
    assert t % tq == 0 and tq % ATTN_CHUNK == 0 and tq % LANES == 0
    kern = functools.partial(_attn_prompt_kernel, dh=dh, lam_init=lam_init)
    lam_spec = pl.BlockSpec((1, dh), lambda b, h, i: (0, 0))
    return pl.pallas_call(
        kern,
        grid=(nb, heads, nq),
        in_specs=[pl.BlockSpec(memory_space=pltpu.SMEM),
                  pl.BlockSpec((tq, dv), lambda b, h, i: (b * nq + i, h)),
                  pl.BlockSpec((t, dv), lambda b, h, i: (b, h)),
                  pl.BlockSpec((nq, dv, tk), lambda b, h, i: (b, h, 0)),
                  lam_spec, lam_spec, lam_spec, lam_spec,
                  pl.BlockSpec((None, 1, dv), lambda b, h, i: (h, 0, 0))],
        out_specs=pl.BlockSpec((tq, dv), lambda b, h, i: (b * nq + i, h)),
        out_shape=jax.ShapeDtypeStruct((nb * t, heads * dv), BF16),
        scratch_shapes=[pltpu.VMEM((2, 1, tq), F32),
                        pltpu.VMEM((2, 1, tq), F32),
                        pltpu.VMEM((2, dv, tq), F32),
                        pltpu.VMEM((tk, tq), F32),
                        pltpu.VMEM((tk, tq), F32)],
        compiler_params=_params(("arbitrary", "arbitrary", "arbitrary")),
        name="attn_prompt",
    )(slopes, aq, ak, av_t, *lams, norm_g.reshape(heads, 1, dv))


def _attn_sample_kernel(slope_ref, q_ref, kn_ref, vn_ref, kp_ref, vp_ref, lq1, lk1, lq2, lk2, ng_ref,
                        o_ref, m_ref, l_ref, acc_ref, *, heads, dh, past, lam_init):
    t = q_ref.shape[0]
    dv = 2 * dh
    tc = vp_ref.shape[0] // (2 * heads)
    ci = pl.program_id(1)
    rowp = past + lax.broadcasted_iota(jnp.int32, (t, 1), 0)

    @pl.when(ci == 0)
    def _():
        m_ref[...] = jnp.full(m_ref.shape, -jnp.inf, F32)
        l_ref[...] = jnp.zeros(l_ref.shape, F32)
        acc_ref[...] = jnp.zeros(acc_ref.shape, F32)

    def update(idx, q, k, v, bias):
        s = _dot_nt(q, k) + bias
        m_old = m_ref[idx]
        m_new = jnp.maximum(m_old, jnp.max(s, axis=1, keepdims=True))
        alpha = jnp.exp2(m_old - m_new)
        p = jnp.exp2(s - m_new)
        l_ref[idx] = alpha * l_ref[idx] + jnp.sum(p, axis=1, keepdims=True)
        acc_ref[idx] = alpha * acc_ref[idx] + _dot(p.astype(BF16), v)
        m_ref[idx] = m_new

    colp = ci * tc + lax.broadcasted_iota(jnp.int32, (1, tc), 1)
    dist = (rowp - colp).astype(F32)
    ss = []
    for h in range(heads):
        for c in range(2):
            k = kp_ref[pl.ds(2 * h + c, tc, stride=2 * heads), :].astype(BF16)
            ss.append(_dot_nt(q_ref[:, h * dv + c * dh:h * dv + (c + 1) * dh], k))
    ps, alphas = [], []
    for h in range(heads):
        bias = -(slope_ref[h] * LOG2E) * dist
        for c in range(2):
            idx = 2 * h + c
            s = ss[idx] + bias
            m_old = m_ref[idx]
            m_new = jnp.maximum(m_old, jnp.max(s, axis=1, keepdims=True))
            alpha = jnp.exp2(m_old - m_new)
            p = jnp.exp2(s - m_new)
            l_ref[idx] = alpha * l_ref[idx] + jnp.sum(p, axis=1, keepdims=True)
            m_ref[idx] = m_new
            ps.append(p.astype(BF16))
            alphas.append(alpha)
    for h in range(heads):
        v = jnp.concatenate([vp_ref[pl.ds(j * heads + h, tc, stride=2 * heads), :] for j in range(2)],
                            axis=1).astype(BF16)
        for c in range(2):
            idx = 2 * h + c
            acc_ref[idx] = alphas[idx] * acc_ref[idx] + _dot(ps[idx], v)

    @pl.when(ci == pl.num_programs(1) - 1)
    def _():
        coln = past + lax.broadcasted_iota(jnp.int32, (1, t), 1)
        dist_n = jnp.abs((rowp - coln).astype(F32))
        vis = (coln // ATTN_CHUNK) <= (rowp // ATTN_CHUNK)
        lam = _lambda(lq1, lk1, lq2, lk2, lam_init)
        for h in range(heads):
            slope2 = slope_ref[h] * LOG2E
            bias = jnp.where(vis, -slope2 * dist_n, -jnp.inf)
            v = vn_ref[:, h * dv:(h + 1) * dv].astype(BF16)
            outs = []
            for c in range(2):
                cols = slice(h * dv + c * dh, h * dv + (c + 1) * dh)
                update(2 * h + c, q_ref[:, cols], kn_ref[:, cols].astype(BF16), v, bias)
                outs.append(acc_ref[2 * h + c] / l_ref[2 * h + c])
            o = outs[0] - lam * outs[1]
            o_ref[:, h * dv:(h + 1) * dv] = _head_norm_out(o, ng_ref[h:h + 1, :], lam_init).astype(o_ref.dtype)


def _attn_sample(aq, k_new, v_new, row0, nb, t, k_past, v_past, l, slopes, lams, norm_g, lam_init):
    heads, dv = norm_g.shape
    dh = dv // 2
    past = k_past.shape[2]
    assert row0 % t == 0 and t % 8 == 0
    rb0 = row0 // t
    tc = _pick(past, (512, 256, 128, 64, 32, 16, 8))
    kp = k_past.reshape(k_past.shape[0], nb, past * heads * 2, dh)
    vp = v_past.reshape(v_past.shape[0], nb, past, heads, 2, dh).transpose(0, 1, 2, 4, 3, 5)
    vp = vp.reshape(v_past.shape[0], nb, past * 2 * heads, dh)
    kern = functools.partial(_attn_sample_kernel, heads=heads, dh=dh, past=past, lam_init=lam_init)
    lam_spec = pl.BlockSpec((1, dh), lambda b, c: (0, 0))
    new_spec = pl.BlockSpec((t, heads * dv), lambda b, c: (b, 0))
    return pl.pallas_call(
        kern,
        grid=(nb, past // tc),
        in_specs=[pl.BlockSpec(memory_space=pltpu.SMEM),
                  pl.BlockSpec((t, heads * dv), lambda b, c: (rb0 + b, 0)), new_spec, new_spec,
                  pl.BlockSpec((None, None, tc * heads * 2, dh), lambda b, c: (l, b, c, 0)),
                  pl.BlockSpec((None, None, tc * heads * 2, dh), lambda b, c: (l, b, c, 0)),
                  lam_spec, lam_spec, lam_spec, lam_spec,
                  pl.BlockSpec((heads, dv), lambda b, c: (0, 0))],
        out_specs=pl.BlockSpec((t, heads * dv), lambda b, c: (b, 0)),
        out_shape=jax.ShapeDtypeStruct((nb * t, heads * dv), BF16),
        scratch_shapes=[pltpu.VMEM((2 * heads, t, 1), F32),
                        pltpu.VMEM((2 * heads, t, 1), F32),
                        pltpu.VMEM((2 * heads, t, dv), F32)],
        compiler_params=_params(("arbitrary", "arbitrary")),
        name="attn_sample",
    )(slopes, aq, k_new, v_new, kp, vp, *lams, norm_g)


def _merge_kernel(hmp_ref, hms_ref, hap_ref, has_ref, wa_ref, wb_ref, g0_ref, g1_ref, o_ref, *, nbp):
    i = pl.program_id(0)

    def body(hm_ref, ha_ref):
        ya = _dot(hm_ref[...], wa_ref[...])
        yb = _dot(ha_ref[...], wb_ref[...])
        g0 = jax.nn.sigmoid(g0_ref[...].astype(F32))
        g1 = jax.nn.sigmoid(g1_ref[...].astype(F32))
        o_ref[...] = (g0 * ya + g1 * yb).astype(o_ref.dtype)

    @pl.when(i < nbp)
    def _():
        body(hmp_ref, hap_ref)

    @pl.when(i >= nbp)
    def _():
        body(hms_ref, has_ref)


def _merge(hm_p, hm_s, ha_p, ha_s, w_a, w_b, gates, rows):
    ka, kb = hm_p.shape[1], ha_p.shape[1]
    d = w_a.shape[1]
    tm = rows.tm
    tn = _pick(d, (512, 256, 128))
    nj = d // tn
    pmap = lambda i, j: (rows.p_tile(i), 0)
    smap = lambda i, j: (rows.s_tile(i), 0)
    return pl.pallas_call(
        functools.partial(_merge_kernel, nbp=rows.nbp),
        grid=(rows.nb, nj),
        in_specs=[pl.BlockSpec((tm, ka), pmap), pl.BlockSpec((tm, ka), smap, pipeline_mode=pl.Buffered(1)),
                  pl.BlockSpec((tm, kb), pmap), pl.BlockSpec((tm, kb), smap, pipeline_mode=pl.Buffered(1)),
                  pl.BlockSpec((ka, tn), lambda i, j: (0, j)),
                  pl.BlockSpec((kb, tn), lambda i, j: (0, j)),
                  pl.BlockSpec((tm, tn), lambda i, j: (i, j)),
                  pl.BlockSpec((tm, tn), lambda i, j: (i, nj + j))],
        out_specs=pl.BlockSpec((tm, tn), lambda i, j: (i, j)),
        out_shape=jax.ShapeDtypeStruct((rows.nb * tm, d), BF16),
        compiler_params=_params(("arbitrary", "arbitrary")),
        name="merge",
    )(hm_p, hm_s, ha_p, ha_s, w_a, w_b, gates, gates)


def _outproj_kernel(a_ref, w_ref, xp_ref, xs_ref, gp_ref, gs_ref, o_ref, *, nbp):
    i = pl.program_id(0)
    y = _dot(a_ref[...], w_ref[...])

    def body(x_ref, gt_ref):
        o_ref[...] = x_ref[...] + _mod_apply(y, [gt_ref[...]], lambda y3, gt: gt * y3)

    @pl.when(i < nbp)
    def _():
        body(xp_ref, gp_ref)

    @pl.when(i >= nbp)
    def _():
        body(xs_ref, gs_ref)


def _outproj(a, w, xp, xs, mod, k_gate, rows):
    k = a.shape[1]
    d = w.shape[1]
    tm = rows.tm
    tn = _pick(d, (512, 256, 128))
    gp, gs = rows.mod_specs(k_gate, tn, col=lambda i, j: j)
    return pl.pallas_call(
        functools.partial(_outproj_kernel, nbp=rows.nbp),
        grid=(rows.nb, d // tn),
        in_specs=[pl.BlockSpec((tm, k), lambda i, j: (i, 0)),
                  pl.BlockSpec((k, tn), lambda i, j: (0, j)),
                  pl.BlockSpec((tm, tn), lambda i, j: (rows.p_tile(i), j)),
                  pl.BlockSpec((tm, tn), lambda i, j: (rows.s_tile(i), j)),
                  gp, gs],
        out_specs=pl.BlockSpec((tm, tn), lambda i, j: (i, j)),
        out_shape=jax.ShapeDtypeStruct((rows.nb * tm, d), F32),
        compiler_params=_params(("arbitrary", "arbitrary")),
        name="outproj",
    )(a, w, xp, xs, mod, mod)


def _top_values(s, k):
    vals = []
    for r in range(k):
        m = jnp.max(s, axis=0, keepdims=True)
        vals.append(m)
        if r + 1 < k:
            s = jnp.where(s == m, -jnp.inf, s)
    return vals


def _peer_select_kernel(q_ref, sk_ref, s_ref, aux_ref, *, heads, dsub):
    tm = q_ref.shape[0]
    for h in range(heads):
        tops = []
        for c in range(2):
            qh = q_ref[:, (2 * h + c) * dsub:(2 * h + c + 1) * dsub]
            sk = sk_ref[h, c]
            qh_hi = qh.astype(BF16)
            qh_lo = (qh - qh_hi.astype(F32)).astype(BF16)
            sk_hi = sk.astype(BF16)
            sk_lo = (sk - sk_hi.astype(F32)).astype(BF16)
            s = _dot_nt(sk_hi, qh_hi) + _dot_nt(sk_hi, qh_lo) + _dot_nt(sk_lo, qh_hi)
            s_ref[h, c] = s
            tops.append(_top_values(s, P_TOPK))
        t2 = jnp.concatenate(tops[1], axis=0)
        cand = jnp.concatenate([tops[0][a] + t2 for a in range(P_TOPK)], axis=0)
        best = _top_values(cand, P_TOPK)
        z = jnp.ones((1, tm), F32)
        for r in range(1, P_TOPK):
            z = z + jnp.exp(best[r] - best[0])
        rows = [best[P_TOPK - 1], tops[0][0], tops[1][0], 1.0 / z]
        aux_ref[h] = jnp.concatenate(rows + [jnp.zeros((8 - len(rows), tm), F32)], axis=0)


def _peer_select(qp, sub_keys, l):
    n = qp.shape[0]
    _, heads, _, nkeys, dsub = sub_keys.shape
    tm = _pick(n, (512, 256, 128))
    kern = functools.partial(_peer_select_kernel, heads=heads, dsub=dsub)
    return pl.pallas_call(
        kern,
        grid=(n // tm,),
        in_specs=[pl.BlockSpec((tm, qp.shape[1]), lambda i: (i, 0)),
                  pl.BlockSpec((None,) + sub_keys.shape[1:], lambda i: (l, 0, 0, 0, 0))],
        out_specs=[pl.BlockSpec((heads, 2, nkeys, tm), lambda i: (0, 0, 0, i)),
                   pl.BlockSpec((heads, 8, tm), lambda i: (0, 0, i))],
        out_shape=[jax.ShapeDtypeStruct((heads, 2, nkeys, n), F32),
                   jax.ShapeDtypeStruct((heads, 8, n), F32)],
        compiler_params=_params(("parallel",)),
        name="peer_select",
    )(qp, sub_keys)


def _peer_dense_kernel(x_ref, u_ref, v_ref, s1_ref, s2_ref, aux_ref, o_ref,
                       s1t_ref, e1_ref, e2_ref, p_ref, pblk_ref, *, heads, nkeys):
    te = u_ref.shape[0]
    tm = x_ref.shape[0]
    e = pl.program_id(1)
    nsub = te // nkeys
    grows = te // PEER_SPLIT
    gsub = nsub // PEER_SPLIT

    @pl.when(e == 0)
    def _():
        o_ref[...] = jnp.zeros(o_ref.shape, F32)
        for h in range(heads):
            e2_ref[h] = jnp.exp(s2_ref[h] - aux_ref[h, 2:3, :]) * aux_ref[h, 3:4, :]

    scores = [_dot_nt(u_ref[g * grows:(g + 1) * grows, :], x_ref[...]) for g in range(PEER_SPLIT)]

    grp = e % (8 // nsub)
    for h in range(heads):
        rows = s1_ref[h, 0:nsub, :]
        for gi in range(1, 8 // nsub):
            rows = jnp.where(grp == gi, s1_ref[h, gi * nsub:(gi + 1) * nsub, :], rows)
        s1t_ref[h, 0:nsub, :] = rows
        e1_ref[h, 0:nsub, :] = jnp.exp(rows - aux_ref[h, 1:2, :])

    for g in range(PEER_SPLIT):
        for tb in range(tm // LANES):
            cs = slice(tb * LANES, (tb + 1) * LANES)
            for kb in range(nkeys // PEER_KEY_BLOCK):
                rs = slice(kb * PEER_KEY_BLOCK, (kb + 1) * PEER_KEY_BLOCK)
                w = [None] * gsub
                for h in range(heads):
                    s2_blk = s2_ref[h, rs, cs]
                    e2_blk = e2_ref[h, rs, cs]
                    tau = aux_ref[h, 0:1, cs]
                    for a in range(gsub):
                        ar = g * gsub + a
                        hit = s1t_ref[h, ar:ar + 1, cs] + s2_blk >= tau
                        gate = jnp.where(hit, e1_ref[h, ar:ar + 1, cs] * e2_blk, 0.0)
                        w[a] = gate if w[a] is None else w[a] + gate
                for a in range(gsub):
                    r0 = a * nkeys + kb * PEER_KEY_BLOCK
                    act = jax.nn.gelu(scores[g][r0:r0 + PEER_KEY_BLOCK, cs], approximate=True)
                    pblk_ref[tb, g * gsub + a, rs, :] = w[a] * act
            for a in range(gsub):
                ar = g * gsub + a
                p_ref[cs, ar * nkeys:(ar + 1) * nkeys] = pblk_ref[tb, ar].T.astype(BF16)
        o_ref[...] += _dot(p_ref[:, g * grows:(g + 1) * grows], v_ref[g * grows:(g + 1) * grows, :])


def _peer_dense(x, u_tab, v_tab, s_t, aux):
    n, d = x.shape
    ne = u_tab.shape[0]
    heads, _, nkeys, _ = s_t.shape
    tm = _pick(n, (512, 256, 128))
    te = _pick(ne, (512, 256, 128))
    assert te % nkeys == 0 and ne == nkeys * nkeys
    n_tiles = ne // te
    nsub = te // nkeys
    assert 8 % nsub == 0 and nkeys % 8 == 0 and nsub % PEER_SPLIT == 0
    kern = functools.partial(_peer_dense_kernel, heads=heads, nkeys=nkeys)
    once = pl.Buffered(1)
    return pl.pallas_call(
        kern,
        grid=(n // tm, n_tiles),
        in_specs=[pl.BlockSpec((tm, d), lambda i, e: (i, 0), pipeline_mode=once),
                  pl.BlockSpec((te, d), lambda i, e: (e, 0)),
                  pl.BlockSpec((te, d), lambda i, e: (e, 0)),
                  pl.BlockSpec((heads, None, 8, tm), lambda i, e: (0, 0, e * nsub // 8, i)),
                  pl.BlockSpec((heads, None, nkeys, tm), lambda i, e: (0, 1, 0, i), pipeline_mode=once),
                  pl.BlockSpec((heads, 8, tm), lambda i, e: (0, 0, i), pipeline_mode=once)],
        out_specs=pl.BlockSpec((tm, d), lambda i, e: (i, 0), pipeline_mode=once),
        out_shape=jax.ShapeDtypeStruct((n, d), F32),
        scratch_shapes=[pltpu.VMEM((heads, 8, tm), F32),
                        pltpu.VMEM((heads, 8, tm), F32),
                        pltpu.VMEM((heads, nkeys, tm), F32),
                        pltpu.VMEM((tm, te), BF16),
                        pltpu.VMEM((tm // LANES, nsub, nkeys, LANES), F32)],
        compiler_params=_params(("parallel", "arbitrary")),
        name="peer_dense",
    )(x, u_tab, v_tab, s_t, s_t, aux)


def _final_kernel(x_ref, p_ref, gp_ref, gs_ref, g_ref, yp_ref, ys_ref, *, nbp):
    i = pl.program_id(0)

    def body(gt_ref, y_ref):
        x = x_ref[...] + _mod_apply(p_ref[...], [gt_ref[...]], lambda p3, gt: gt * p3)
        y_ref[...] = x * lax.rsqrt(jnp.mean(x * x, -1, keepdims=True) + EPS) * g_ref[...]

    @pl.when(i < nbp)
    def _():
        body(gp_ref, yp_ref)

    @pl.when(i >= nbp)
    def _():
        body(gs_ref, ys_ref)


def _final(x, peer, mod, k_gate, g_final, rows):
    d = x.shape[1]
    tm = rows.tm
    gp, gs = rows.mod_specs(k_gate, d)
    return pl.pallas_call(
        functools.partial(_final_kernel, nbp=rows.nbp),
        grid=(rows.nb,),
        in_specs=[pl.BlockSpec((tm, d), lambda i: (i, 0)),
                  pl.BlockSpec((tm, d), lambda i: (i, 0)),
                  gp, gs,
                  pl.BlockSpec((1, d), lambda i: (0, 0))],
        out_specs=[pl.BlockSpec((tm, d), lambda i: (rows.p_tile(i), 0)),
                   pl.BlockSpec((tm, d), lambda i: (rows.s_tile(i), 0))],
        out_shape=[jax.ShapeDtypeStruct((rows.nbp * tm, d), F32),
                   jax.ShapeDtypeStruct((rows.nbs * tm, d), F32)],
        compiler_params=_params(("arbitrary",)),
        name="final_norm",
    )(x, peer, mod, mod, g_final.reshape(1, d))


def kernel(x_prompt, x_sample, cache_k, cache_v, state_C, state_n, state_m, state_conv, c_prompt, c_sample, w_ada, b_ada, g_mix, w_in, conv_w, conv_b, b_gates, m_norm_g, lam_q1, lam_k1, lam_q2, lam_k2, a_norm_g, w_a, w_b, w_out, g_ffn, w_pq, sub_keys, u_tab, v_tab, g_final):
    bp, tp, d = x_prompt.shape
    bs, ts, _ = x_sample.shape
    depth = w_in.shape[0]
    assert depth == 1, "the per-layer state plumbing below is written for a single layer"
    l = 0
    heads, dv = m_norm_g.shape[1:]
    dk = state_C.shape[-1]
    a_heads, a_dv = a_norm_g.shape[1:]
    dh = a_dv // 2
    n_p, n_s = bp * tp, bs * ts
    xp = x_prompt.reshape(n_p, d)
    xs = x_sample.reshape(n_s, d)
    rows_ew = _Rows(bp, tp, bs, ts, _pick(math.gcd(tp, n_s), (256, 128, 64, 32)))
    rows_mm = _Rows(bp, tp, bs, ts, _pick(math.gcd(tp, n_s), (1024, 512, 256, 128, 64, 32)))

    qk_w, v_w = 2 * heads * dk, heads * dv
    a_w = a_heads * a_dv
    col_mv = qk_w
    col_gate = qk_w + 2 * v_w
    col_aq = col_gate + 2 * heads
    col_ak = col_aq + a_w
    col_av = col_ak + a_w
    col_bg = col_av + a_w

    c_all = jnp.concatenate([c_sample, c_prompt], axis=0)
    c_all = jnp.pad(c_all, ((0, (-c_all.shape[0]) % 8), (0, 0)))
    mod = _ada(c_all, w_ada, b_ada, l).reshape(c_all.shape[0], 6, 1, d)

    hn = _norm_mod(xp, xs, 0, rows_ew, g_mix[l], mod, 1, 0)
    w_t = jnp.swapaxes(w_in, 1, 2)
    qk_raw, = _proj(hn, w_t, l, 0, qk_w, "f32", n_p)
    vo, = _proj(hn, w_t, l, col_mv, 2 * v_w, "bf16", n_p)
    aq, = _proj(hn, w_t, l, col_aq, a_w, "bf16", n_p, scale=LOG2E * dh ** -0.5)
    ak, k_p, k_s = _proj(hn, w_t, l, col_ak, a_w, "kv", n_p)
    av_t, v_p, v_s = _proj(hn, w_t, l, col_av, a_w, "kvt", n_p)
    bgates, = _proj(hn, w_t, l, col_bg, 2 * d, "bf16", n_p)
    gates = _gate_proj(hn, w_t, l, col_gate)

    zc = jnp.zeros((bp, heads, dv, dk), F32)
    zn = jnp.zeros((bp, heads, dk), F32)
    zm = jnp.zeros((bp, heads), F32)
    zconv = jnp.zeros((bp, CONV_W - 1, qk_w), F32)
    mp = _mlstm(qk_raw, vo, gates, 0, bp, tp, zconv, conv_w[l], conv_b[l], b_gates[l], m_norm_g[l], zc, zn, zm)
    ms = _mlstm(qk_raw, vo, gates, n_p, bs, ts, state_conv[l], conv_w[l], conv_b[l], b_gates[l], m_norm_g[l],
                state_C[l], state_n[l], state_m[l])

    lam_init = 0.8 - 0.6 * math.exp(-0.3 * l)
    slopes = jnp.array([2.0 ** (-8.0 * (h + 1) / a_heads) for h in range(a_heads)], F32)
    lams = [a[l].reshape(1, -1) for a in (lam_q1, lam_k1, lam_q2, lam_k2)]
    ha_p = _attn_prompt(aq, ak, av_t, bp, tp, slopes, lams, a_norm_g[l], lam_init)
    ha_s = _attn_sample(aq, k_s, v_s, n_p, bs, ts, cache_k, cache_v, l, slopes, lams, a_norm_g[l], lam_init)

    merged = _merge(mp[0], ms[0], ha_p, ha_s, _cast_bf16(w_a, l), _cast_bf16(w_b, l), bgates, rows_mm)
    x1 = _outproj(merged, _cast_bf16(w_out, l), xp, xs, mod, 2, rows_mm)

    hn2 = _norm_mod(x1, x1, rows_ew.nbp, rows_ew, g_ffn[l], mod, 4, 3)
    qp = _matmul(hn2, _cast_bf16(w_pq, l), F32, "peer_query")
    s_t, aux = _peer_select(qp, sub_keys, l)
    peer = _peer_dense(hn2, _cast_bf16(u_tab, l), _cast_bf16(v_tab, l), s_t, aux)
    y_p, y_s = _final(x1, peer, mod, 5, g_final, rows_ew)

    return (y_p.reshape(bp, tp, d), y_s.reshape(bs, ts, d),
            k_p.reshape(1, bp, tp, a_heads, 2, dh), v_p.reshape(1, bp, tp, a_heads, a_dv),
            mp[1][None], mp[2][None], mp[3][None], mp[4][None],
            k_s.reshape(1, bs, ts, a_heads, 2, dh), v_s.reshape(1, bs, ts, a_heads, a_dv),
            ms[1][None], ms[2][None], ms[3][None], ms[4][None])
```

```python
import functools
import math

import jax
import jax.numpy as jnp
from jax import lax
from jax.experimental import pallas as pl
from jax.experimental.pallas import tpu as pltpu

F32 = jnp.float32
BF16 = jnp.bfloat16

EPS = 1e-6
ATTN_CHUNK = 64
ATTN_KEY_BLOCK = 512
CONV_W = 4
P_TOPK = 16
PEER_KEY_BLOCK = 32
PEER_SPLIT = 2
LANES = 128
MLSTM_CHUNK = 256
VMEM_LIMIT_MB = 56
LOG2E = math.log2(math.e)


def _params(sem, vmem_mb=VMEM_LIMIT_MB):
    return pltpu.CompilerParams(dimension_semantics=sem, vmem_limit_bytes=vmem_mb << 20)


def _pick(n, cands):
    for c in cands:
        if n % c == 0:
            return c
    raise ValueError(f"no tile in {cands} divides {n}")


def _dot(a, b):
    return jnp.dot(a, b, preferred_element_type=F32)


def _dot_nt(a, b):
    return lax.dot_general(a, b, (((1,), (1,)), ((), ())), preferred_element_type=F32)


def _dot_tn(a, b):
    return lax.dot_general(a, b, (((0,), (0,)), ((), ())), preferred_element_type=F32)


def _split3(x):
    hi = x.astype(BF16)
    r = x - hi.astype(F32)
    mid = r.astype(BF16)
    lo = (r - mid.astype(F32)).astype(BF16)
    return hi, mid, lo


def _dot_exact_lhs(a01, x):
    hi, mid, lo = _split3(x)
    return _dot(a01, hi) + _dot(a01, mid) + _dot(a01, lo)


def _dot_exact_nt(a01, x):
    hi, mid, lo = _split3(x)
    return _dot_nt(a01, hi) + _dot_nt(a01, mid) + _dot_nt(a01, lo)


class _Rows:
    def __init__(self, bp, tp, bs, ts, tm):
        assert (bp * tp) % tm == 0 and (bs * ts) % tm == 0
        assert tp % tm == 0 and tm % ts == 0, "a prompt tile sits in one sequence, a sample tile holds whole ones"
        self.bp, self.tp, self.bs, self.ts, self.tm = bp, tp, bs, ts, tm
        self.nbp = bp * tp // tm
        self.nbs = bs * ts // tm
        self.nb = self.nbp + self.nbs
        self.ng = tm // ts

    def p_tile(self, i):
        return jnp.minimum(i, self.nbp - 1)

    def s_tile(self, i):
        return jnp.maximum(i - self.nbp, 0)

    def mod_specs(self, k, width, col=None):
        cj = (lambda *g: 0) if col is None else col
        p = pl.BlockSpec((None, None, 1, width),
                         lambda *g: (self.bs + (self.p_tile(g[0]) * self.tm) // self.tp, k, 0, cj(*g)))
        s = pl.BlockSpec((self.ng, None, 1, width), lambda *g: (self.s_tile(g[0]), k, 0, cj(*g)))
        return p, s


def _mod_apply(x, gate_rows, fn):
    tm, w = x.shape
    rows = [r.reshape((-1, 1, w)) for r in gate_rows]
    ng = rows[0].shape[0]
    return fn(x.reshape(ng, tm // ng, w), *rows).reshape(tm, w)


def _cast_kernel(w_ref, o_ref):
    o_ref[...] = w_ref[...].astype(o_ref.dtype)


def _cast_bf16(w3, l):
    _, r, c = w3.shape
    tr = _pick(r, (512, 256, 128))
    tc = _pick(c, (2048, 1024, 512, 256, 128))
    return pl.pallas_call(
        _cast_kernel,
        grid=(r // tr, c // tc),
        in_specs=[pl.BlockSpec((None, tr, tc), lambda i, j: (l, i, j))],
        out_specs=pl.BlockSpec((tr, tc), lambda i, j: (i, j)),
        out_shape=jax.ShapeDtypeStruct((r, c), BF16),
        compiler_params=_params(("parallel", "parallel")),
        name="cast_bf16",
    )(w3)


def _ada_kernel(c_ref, w_ref, b_ref, o_ref):
    c = c_ref[...]
    a = (c * jax.nn.sigmoid(c)).astype(BF16)
    o_ref[...] = _dot(a, w_ref[...].astype(BF16)) + b_ref[...]


def _ada(c_all, w_ada, b_ada, l):
    m, d = c_all.shape
    n = w_ada.shape[2]
    tn = _pick(n, (512, 256, 128))
    return pl.pallas_call(
        _ada_kernel,
        grid=(n // tn,),
        in_specs=[pl.BlockSpec((m, d), lambda j: (0, 0)),
                  pl.BlockSpec((None, d, tn), lambda j: (l, 0, j)),
                  pl.BlockSpec((None, 1, tn), lambda j: (l, 0, j))],
        out_specs=pl.BlockSpec((m, tn), lambda j: (0, j)),
        out_shape=jax.ShapeDtypeStruct((m, n), F32),
        compiler_params=_params(("parallel",)),
        name="ada_mod",
    )(c_all, w_ada, b_ada.reshape(b_ada.shape[0], 1, n))


def _norm_mod_kernel(xp_ref, xs_ref, g_ref, scp_ref, shp_ref, scs_ref, shs_ref, o_ref, *, nbp):
    i = pl.program_id(0)

    def body(x_ref, sc_ref, sh_ref):
        x = x_ref[...]
        y = x * lax.rsqrt(jnp.mean(x * x, -1, keepdims=True) + EPS) * g_ref[...]
        y = _mod_apply(y, [sc_ref[...], sh_ref[...]], lambda y3, sc, sh: y3 * (1.0 + sc) + sh)
        o_ref[...] = y.astype(o_ref.dtype)

    @pl.when(i < nbp)
    def _():
        body(xp_ref, scp_ref, shp_ref)

    @pl.when(i >= nbp)
    def _():
        body(xs_ref, scs_ref, shs_ref)


def _norm_mod(xp, xs, s_off, rows, g, mod, k_scale, k_shift):
    d = xp.shape[1]
    tm = rows.tm
    scp, scs = rows.mod_specs(k_scale, d)
    shp, shs = rows.mod_specs(k_shift, d)
    return pl.pallas_call(
        functools.partial(_norm_mod_kernel, nbp=rows.nbp),
        grid=(rows.nb,),
        in_specs=[pl.BlockSpec((tm, d), lambda i: (rows.p_tile(i), 0)),
                  pl.BlockSpec((tm, d), lambda i: (s_off + rows.s_tile(i), 0)),
                  pl.BlockSpec((1, d), lambda i: (0, 0)),
                  scp, shp, scs, shs],
        out_specs=pl.BlockSpec((tm, d), lambda i: (i, 0)),
        out_shape=jax.ShapeDtypeStruct((rows.nb * tm, d), BF16),
        compiler_params=_params(("arbitrary",)),
        name="norm_mod",
    )(xp, xs, g.reshape(1, d), mod, mod, mod, mod)


def _proj_kernel(x_ref, w_ref, wn_ref, *refs, shift, scale, mode, nip):
    outs, wb_ref = refs[:-1], refs[-1]
    i = pl.program_id(1)

    @pl.when(i == 0)
    def _():
        tn = w_ref.shape[0]
        cr = _pick(tn, (256, 128))
        for r in range(0, tn, cr):
            if r + cr + shift <= tn:
                wb_ref[r:r + cr, :] = w_ref[r + shift:r + cr + shift, :].astype(BF16)
            else:
                wb_ref[r:r + cr - shift, :] = w_ref[r + shift:r + cr, :].astype(BF16)
                wb_ref[r + cr - shift:r + cr, :] = wn_ref[...].astype(BF16)

    acc = _dot_nt(x_ref[...], wb_ref[...])
    if scale != 1.0:
        acc = acc * scale
    if mode in ("kv", "kvt"):
        outs[0][...] = (acc.T if mode == "kvt" else acc).astype(BF16)

        @pl.when(i < nip)
        def _():
            outs[1][...] = acc

        @pl.when(i >= nip)
        def _():
            outs[2][...] = acc
    else:
        outs[0][...] = acc.astype(outs[0].dtype)


def _proj(hn, w_t, l, col0, ncols, mode, n_p, scale=1.0):
    n, k = hn.shape
    tm = _pick(math.gcd(n_p, n - n_p), ((1024,) if mode != "kvt" else ()) + (ATTN_KEY_BLOCK, 256, 128))
    tn = _pick(ncols, ((1024,) if mode == "bf16" else ()) + (512, 256, 128))
    shift = col0 % tn
    assert (col0 - shift) % tn == 0 and shift % 16 == 0 and shift < 256
    cb0 = (col0 - shift) // tn
    nip = n_p // tm
    ni, nj = n // tm, ncols // tn
    once = pl.Buffered(1)
    nxt = shift if shift else 16
    in_specs = [pl.BlockSpec((tm, k), lambda j, i: (i, 0)),
                pl.BlockSpec((None, tn, k), lambda j, i: (l, cb0 + j, 0), pipeline_mode=once),
                pl.BlockSpec((None, nxt, k), lambda j, i: (l, (cb0 + j + 1) * (tn // nxt) if shift else 0, 0),
                             pipeline_mode=once)]
    all_spec = pl.BlockSpec((tm, tn), lambda j, i: (i, j))
    if mode in ("kv", "kvt"):
        first = (pl.BlockSpec((None, tn, tm), lambda j, i: (i, j, 0)) if mode == "kvt" else all_spec)
        out_specs = [first,
                     pl.BlockSpec((tm, tn), lambda j, i: (jnp.minimum(i, nip - 1), j)),
                     pl.BlockSpec((tm, tn), lambda j, i: (jnp.maximum(i - nip, 0), j))]
        out_shape = [jax.ShapeDtypeStruct((ni, ncols, tm) if mode == "kvt" else (n, ncols), BF16),
                     jax.ShapeDtypeStruct((n_p, ncols), F32),
                     jax.ShapeDtypeStruct((n - n_p, ncols), F32)]
    else:
        out_specs = [all_spec]
        out_shape = [jax.ShapeDtypeStruct((n, ncols), F32 if mode == "f32" else BF16)]
    kern = functools.partial(_proj_kernel, shift=shift, scale=scale, mode=mode, nip=nip)
    return pl.pallas_call(
        kern,
        grid=(nj, ni),
        in_specs=in_specs,
        out_specs=out_specs,
        out_shape=out_shape,
        scratch_shapes=[pltpu.VMEM((tn, k), BF16)],
        compiler_params=_params(("arbitrary", "arbitrary")),
        name="proj_" + mode,
    )(hn, w_t, w_t)


def _gate_kernel(x_ref, w_ref, o_ref):
    o_ref[...] = _dot_nt(x_ref[...], w_ref[...].astype(BF16))


def _gate_proj(hn, w_t, l, col0):
    n, k = hn.shape
    assert col0 % LANES == 0
    tm = _pick(n, (1024, 512, 256, 128))
    return pl.pallas_call(
        _gate_kernel,
        grid=(n // tm,),
        in_specs=[pl.BlockSpec((tm, k), lambda i: (i, 0)),
                  pl.BlockSpec((None, LANES, k), lambda i: (l, col0 // LANES, 0))],
        out_specs=pl.BlockSpec((tm, LANES), lambda i: (i, 0)),
        out_shape=jax.ShapeDtypeStruct((n, LANES), F32),
        compiler_params=_params(("parallel",)),
        name="gate_proj",
    )(hn, w_t)


def _mm_kernel(a_ref, b_ref, o_ref):
    o_ref[...] = _dot(a_ref[...], b_ref[...]).astype(o_ref.dtype)


def _matmul(a, b, out_dtype, name):
    m, k = a.shape
    n = b.shape[1]
    tm = _pick(m, (1024, 512, 256, 128))
    tn = _pick(n, (1024, 512, 256, 128))
    return pl.pallas_call(
        _mm_kernel,
        grid=(m // tm, n // tn),
        in_specs=[pl.BlockSpec((tm, k), lambda i, j: (i, 0)),
                  pl.BlockSpec((k, tn), lambda i, j: (0, j))],
        out_specs=pl.BlockSpec((tm, tn), lambda i, j: (i, j)),
        out_shape=jax.ShapeDtypeStruct((m, n), out_dtype),
        compiler_params=_params(("parallel", "parallel")),
        name=name,
    )(a, b)


def _mlstm_kernel(qk_ref, v_ref, og_ref, gt_ref, conv0_ref, cw_ref, cb_ref, bg_ref, ng_ref,
                  c0_ref, n0_ref, m0_ref,
                  hm_ref, c_ref, n_ref, m_ref, convn_ref, full_ref, *, heads, dk, dv):
    L = qk_ref.shape[0]
    pad = 8
    hist = CONV_W - 1
    ci = pl.program_id(1)

    @pl.when(ci == 0)
    def _():
        c_ref[...] = c0_ref[...]
        n_ref[...] = n0_ref[...]
        m_ref[...] = m0_ref[...]
        full_ref[pad - hist:pad, :] = conv0_ref[0]

    u = qk_ref[...]
    full_ref[pad:pad + L, :] = u
    acc = cb_ref[...] + u * cw_ref[hist:hist + 1, :]
    for j in range(hist):
        acc = acc + full_ref[pad - hist + j:pad - hist + j + L, :] * cw_ref[j:j + 1, :]
    tail = full_ref[pad + L - hist:pad + L, :]
    convn_ref[0] = tail
    full_ref[pad - hist:pad, :] = tail
    qk = acc * jax.nn.sigmoid(acc)

    row = lax.broadcasted_iota(jnp.int32, (L, L), 0)
    col = lax.broadcasted_iota(jnp.int32, (L, L), 1)
    causal = col <= row
    tril01 = jnp.where(causal, 1.0, 0.0).astype(BF16)
    glanes = gt_ref.shape[1]
    pre = gt_ref[...] + bg_ref[...]
    logf_all = jnp.minimum(pre, 0.0) - jnp.log1p(jnp.exp(-jnp.abs(pre)))
    b_all = _dot_exact_lhs(tril01, logf_all)
    eye01 = jnp.where(lax.broadcasted_iota(jnp.int32, (glanes, glanes), 0)
                      == lax.broadcasted_iota(jnp.int32, (glanes, glanes), 1), 1.0, 0.0).astype(BF16)
    pre_t = _dot_exact_nt(eye01, pre)
    b_t = _dot_exact_nt(eye01, b_all)

    for h in range(heads):
        q = qk[:, h * dk:(h + 1) * dk]
        k = qk[:, (heads + h) * dk:(heads + h + 1) * dk] * (dk ** -0.5)
        vb = v_ref[:, h * dv:(h + 1) * dv]
        v = vb.astype(F32)
        qb, kb = q.astype(BF16), k.astype(BF16)
        i_col = pre[:, h:h + 1]
        b_col = b_all[:, heads + h:heads + h + 1]
        dmat = jnp.where(causal, b_col + (pre_t[h:h + 1, :] - b_t[heads + h:heads + h + 1, :]), -jnp.inf)
        m_prev = m_ref[0, h:h + 1, 0:1]
        inter = b_col + m_prev
        m_t = jnp.maximum(inter, jnp.max(dmat, axis=1, keepdims=True))
        s = _dot_nt(qb, kb) * jnp.exp(dmat - m_t)
        w_prev = jnp.exp(inter - m_t)
        c_old = c_ref[0, h]
        n_old = n_ref[0, h:h + 1, :]
        num = _dot(s.astype(BF16), vb) + w_prev * _dot_nt(qb, c_old.astype(BF16))
        den = jnp.sum(s, axis=1, keepdims=True) + w_prev * jnp.sum(q * n_old, axis=1, keepdims=True)
        hh = num / jnp.maximum(jnp.abs(den), jnp.exp(-m_t))
        m_new = m_t[L - 1:L, :]
        b_last = b_col[L - 1:L, :]
        decay = jnp.exp(b_last + m_prev - m_new)
        w_src = jnp.exp(b_last - b_col + i_col - m_new)
        c_ref[0, h] = decay * c_old + _dot_tn((w_src * v).astype(BF16), kb)
        n_ref[0, h:h + 1, :] = decay * n_old + jnp.sum(w_src * k, axis=0, keepdims=True)
        m_ref[0, h:h + 1, :] = jnp.broadcast_to(m_new, (1, m_ref.shape[2]))
        hn = hh * lax.rsqrt(jnp.mean(hh * hh, -1, keepdims=True) + EPS) * ng_ref[h:h + 1, :]
        hn = hn * jax.nn.sigmoid(og_ref[:, h * dv:(h + 1) * dv].astype(F32))
        hm_ref[:, h * dv:(h + 1) * dv] = hn.astype(hm_ref.dtype)


def _mlstm(qk_raw, vo, gates, row0, nb, t, conv0, conv_w, conv_b, b_gates, norm_g, c0, n0, m0):
    heads, dv = norm_g.shape
    dk = c0.shape[-1]
    qkw = 2 * heads * dk
    vw = heads * dv
    L = _pick(t, (MLSTM_CHUNK, 128, 64, 32, 16, 8))
    nc = t // L
    assert row0 % L == 0 and L >= CONV_W - 1
    rb0 = row0 // L
    m0b = jnp.broadcast_to(m0[:, :, None], (nb, heads, LANES))
    glanes = gates.shape[1]
    assert 2 * heads <= glanes
    gate_bias = jnp.pad(b_gates.reshape(1, 2 * heads), ((0, 0), (0, glanes - 2 * heads)))
    rmap = lambda b, c: rb0 + b * nc + c
    kern = functools.partial(_mlstm_kernel, heads=heads, dk=dk, dv=dv)
    hm, c1, n1, m1, convn = pl.pallas_call(
        kern,
        grid=(nb, nc),
        in_specs=[pl.BlockSpec((L, qkw), lambda b, c: (rmap(b, c), 0)),
                  pl.BlockSpec((L, vw), lambda b, c: (rmap(b, c), 0)),
                  pl.BlockSpec((L, vw), lambda b, c: (rmap(b, c), 1)),
                  pl.BlockSpec((L, gates.shape[1]), lambda b, c: (rmap(b, c), 0)),
                  pl.BlockSpec((1, CONV_W - 1, qkw), lambda b, c: (b, 0, 0)),
                  pl.BlockSpec((CONV_W, qkw), lambda b, c: (0, 0)),
                  pl.BlockSpec((1, qkw), lambda b, c: (0, 0)),
                  pl.BlockSpec((1, glanes), lambda b, c: (0, 0)),
                  pl.BlockSpec((heads, dv), lambda b, c: (0, 0)),
                  pl.BlockSpec((1, heads, dv, dk), lambda b, c: (b, 0, 0, 0)),
                  pl.BlockSpec((1, heads, dk), lambda b, c: (b, 0, 0)),
                  pl.BlockSpec((1, heads, LANES), lambda b, c: (b, 0, 0))],
        out_specs=[pl.BlockSpec((L, vw), lambda b, c: (b * nc + c, 0)),
                   pl.BlockSpec((1, heads, dv, dk), lambda b, c: (b, 0, 0, 0)),
                   pl.BlockSpec((1, heads, dk), lambda b, c: (b, 0, 0)),
                   pl.BlockSpec((1, heads, LANES), lambda b, c: (b, 0, 0)),
                   pl.BlockSpec((1, CONV_W - 1, qkw), lambda b, c: (b, 0, 0))],
        out_shape=[jax.ShapeDtypeStruct((nb * t, vw), BF16),
                   jax.ShapeDtypeStruct((nb, heads, dv, dk), F32),
                   jax.ShapeDtypeStruct((nb, heads, dk), F32),
                   jax.ShapeDtypeStruct((nb, heads, LANES), F32),
                   jax.ShapeDtypeStruct((nb, CONV_W - 1, qkw), F32)],
        scratch_shapes=[pltpu.VMEM((8 + L, qkw), F32)],
        compiler_params=_params(("parallel", "arbitrary")),
        name="mlstm",
    )(qk_raw, vo, vo, gates, conv0, conv_w, conv_b.reshape(1, qkw), gate_bias, norm_g, c0, n0, m0b)
    return hm, c1, n1, m1[:, :, 0], convn


def _lambda(lq1, lk1, lq2, lk2, lam_init):
    return (jnp.exp(jnp.sum(lq1[...] * lk1[...], axis=1, keepdims=True))
            - jnp.exp(jnp.sum(lq2[...] * lk2[...], axis=1, keepdims=True)) + lam_init)


def _head_norm_out(o, g_row, lam_init):
    return o * lax.rsqrt(jnp.mean(o * o, -1, keepdims=True) + EPS) * g_row * (1.0 - lam_init)


def _attn_prompt_kernel(slope_ref, q_ref, k_ref, vt_ref, lq1, lk1, lq2, lk2, ng_ref, o_ref,
                        m_ref, l_ref, acc_ref, bias_ref, dbias_ref, *, dh, lam_init):
    tq = q_ref.shape[0]
    tk = vt_ref.shape[2]
    h = pl.program_id(1)
    qi = pl.program_id(2)
    slope2 = slope_ref[h] * LOG2E

    @pl.when(qi == 0)
    def _():
        krow = lax.broadcasted_iota(jnp.int32, (tk, tq), 0)
        qcol = lax.broadcasted_iota(jnp.int32, (tk, tq), 1)
        base = slope2 * krow.astype(F32)
        bias_ref[...] = base
        ahead = jnp.maximum(krow - qcol, 0).astype(F32)
        vis = (krow // ATTN_CHUNK) <= (qcol // ATTN_CHUNK)
        dbias_ref[...] = jnp.where(vis, base - (2.0 * slope2) * ahead, -jnp.inf)

    m_ref[...] = jnp.full(m_ref.shape, -jnp.inf, F32)
    l_ref[...] = jnp.zeros(l_ref.shape, F32)
    acc_ref[...] = jnp.zeros(acc_ref.shape, F32)

    def update(kjs, b_ref):
        kbs = [k_ref[pl.ds(pl.multiple_of(kj * tk, tk), tk), :] for kj in kjs]
        vts = [vt_ref[kj] for kj in kjs]
        shifts = [slope2 * ((kj * tk - qi * tq).astype(F32)) for kj in kjs]
        ss = [[_dot_nt(kb[:, c * dh:(c + 1) * dh], q_ref[:, c * dh:(c + 1) * dh]) for kb in kbs]
              for c in range(2)]
        ps, alphas = [], []
        for c in range(2):
            sb = [s + b_ref[...] for s in ss[c]]
            m_old = m_ref[c]
            m_new = m_old
            for s, shift in zip(sb, shifts):
                m_new = jnp.maximum(m_new, jnp.max(s, axis=0, keepdims=True) + shift)
            alpha = jnp.exp2(m_old - m_new)
            pc = [jnp.exp2(s - (m_new - shift)) for s, shift in zip(sb, shifts)]
            l_new = alpha * l_ref[c]
            for p in pc:
                l_new = l_new + jnp.sum(p, axis=0, keepdims=True)
            l_ref[c] = l_new
            m_ref[c] = m_new
            ps.append([p.astype(BF16) for p in pc])
            alphas.append(alpha)
        for c in range(2):
            acc = alphas[c] * acc_ref[c]
            for vt, p in zip(vts, ps[c]):
                acc = acc + _dot(vt, p)
            acc_ref[c] = acc

    def body(kj, carry):
        update([kj], bias_ref)
        return carry

    lax.fori_loop(0, qi, body, 0)
    update([qi], dbias_ref)
    lam = _lambda(lq1, lk1, lq2, lk2, lam_init)
    o = acc_ref[0] / l_ref[0] - lam * (acc_ref[1] / l_ref[1])
    o = o * lax.rsqrt(jnp.mean(o * o, axis=0, keepdims=True) + EPS)
    o_ref[...] = (o.T * ng_ref[...] * (1.0 - lam_init)).astype(o_ref.dtype)


def _attn_prompt(aq, ak, av_t, nb, t, slopes, lams, norm_g, lam_init):
    heads, dv = norm_g.shape
    dh = dv // 2
    tk = av_t.shape[2]
    tq = tk
    nq = t // tq
    assert t % tq == 0 and tq % ATTN_CHUNK == 0 and tq % LANES == 0
    kern = functools.partial(_attn_prompt_kernel, dh=dh, lam_init=lam_init)
    lam_spec = pl.BlockSpec((1, dh), lambda b, h, i: (0, 0))
    return pl.pallas_call(
        kern,
        grid=(nb, heads, nq),
        in_specs=[pl.BlockSpec(memory_space=pltpu.SMEM),
                  pl.BlockSpec((tq, dv), lambda b, h, i: (b * nq + i, h)),
                  pl.BlockSpec((t, dv), lambda b, h, i: (b, h)),
                  pl.BlockSpec((nq, dv, tk), lambda b, h, i: (b, h, 0)),
                  lam_spec, lam_spec, lam_spec, lam_spec,
                  pl.BlockSpec((None, 1, dv), lambda b, h, i: (h, 0, 0))],
        out_specs=pl.BlockSpec((tq, dv), lambda b, h, i: (b * nq + i, h)),
        out_shape=jax.ShapeDtypeStruct((nb * t, heads * dv), BF16),
        scratch_shapes=[pltpu.VMEM((2, 1, tq), F32),
                        pltpu.VMEM((2, 1, tq), F32),
                        pltpu.VMEM((2, dv, tq), F32),
                        pltpu.VMEM((tk, tq), F32),
                        pltpu.VMEM((tk, tq), F32)],
        compiler_params=_params(("arbitrary", "arbitrary", "arbitrary")),
        name="attn_prompt",
    )(slopes, aq, ak, av_t, *lams, norm_g.reshape(heads, 1, dv))


def _attn_sample_kernel(slope_ref, q_ref, kn_ref, vn_ref, kp_ref, vp_ref, lq1, lk1, lq2, lk2, ng_ref,
                        o_ref, m_ref, l_ref, acc_ref, *, heads, dh, past, lam_init):
    t = q_ref.shape[0]
    dv = 2 * dh
    tc = vp_ref.shape[0] // (2 * heads)
    ci = pl.program_id(1)
    rowp = past + lax.broadcasted_iota(jnp.int32, (t, 1), 0)

    @pl.when(ci == 0)
    def _():
        m_ref[...] = jnp.full(m_ref.shape, -jnp.inf, F32)
        l_ref[...] = jnp.zeros(l_ref.shape, F32)
        acc_ref[...] = jnp.zeros(acc_ref.shape, F32)

    def update(idx, q, k, v, bias):
        s = _dot_nt(q, k) + bias
        m_old = m_ref[idx]
        m_new = jnp.maximum(m_old, jnp.max(s, axis=1, keepdims=True))
        alpha = jnp.exp2(m_old - m_new)
        p = jnp.exp2(s - m_new)
        l_ref[idx] = alpha * l_ref[idx] + jnp.sum(p, axis=1, keepdims=True)
        acc_ref[idx] = alpha * acc_ref[idx] + _dot(p.astype(BF16), v)
        m_ref[idx] = m_new

    colp = ci * tc + lax.broadcasted_iota(jnp.int32, (1, tc), 1)
    dist = (rowp - colp).astype(F32)
    ss = []
    for h in range(heads):
        for c in range(2):
            k = kp_ref[pl.ds(2 * h + c, tc, stride=2 * heads), :].astype(BF16)
            ss.append(_dot_nt(q_ref[:, h * dv + c * dh:h * dv + (c + 1) * dh], k))
    ps, alphas = [], []
    for h in range(heads):
        bias = -(slope_ref[h] * LOG2E) * dist
        for c in range(2):
            idx = 2 * h + c
            s = ss[idx] + bias
            m_old = m_ref[idx]
            m_new = jnp.maximum(m_old, jnp.max(s, axis=1, keepdims=True))
            alpha = jnp.exp2(m_old - m_new)
            p = jnp.exp2(s - m_new)
            l_ref[idx] = alpha * l_ref[idx] + jnp.sum(p, axis=1, keepdims=True)
            m_ref[idx] = m_new
            ps.append(p.astype(BF16))
            alphas.append(alpha)
    for h in range(heads):
        v = jnp.concatenate([vp_ref[pl.ds(j * heads + h, tc, stride=2 * heads), :] for j in range(2)],
                            axis=1).astype(BF16)
        for c in range(2):
            idx = 2 * h + c
            acc_ref[idx] = alphas[idx] * acc_ref[idx] + _dot(ps[idx], v)

    @pl.when(ci == pl.num_programs(1) - 1)
    def _():
        coln = past + lax.broadcasted_iota(jnp.int32, (1, t), 1)
        dist_n = jnp.abs((rowp - coln).astype(F32))
        vis = (coln // ATTN_CHUNK) <= (rowp // ATTN_CHUNK)
        lam = _lambda(lq1, lk1, lq2, lk2, lam_init)
        for h in range(heads):
            slope2 = slope_ref[h] * LOG2E
            bias = jnp.where(vis, -slope2 * dist_n, -jnp.inf)
            v = vn_ref[:, h * dv:(h + 1) * dv].astype(BF16)
            outs = []
            for c in range(2):
                cols = slice(h * dv + c * dh, h * dv + (c + 1) * dh)
                update(2 * h + c, q_ref[:, cols], kn_ref[:, cols].astype(BF16), v, bias)
                outs.append(acc_ref[2 * h + c] / l_ref[2 * h + c])
            o = outs[0] - lam * outs[1]
            o_ref[:, h * dv:(h + 1) * dv] = _head_norm_out(o, ng_ref[h:h + 1, :], lam_init).astype(o_ref.dtype)


def _attn_sample(aq, k_new, v_new, row0, nb, t, k_past, v_past, l, slopes, lams, norm_g, lam_init):
    heads, dv = norm_g.shape
    dh = dv // 2
    past = k_past.shape[2]
    assert row0 % t == 0 and t % 8 == 0
    rb0 = row0 // t
    tc = _pick(past, (512, 256, 128, 64, 32, 16, 8))
    kp = k_past.reshape(k_past.shape[0], nb, past * heads * 2, dh)
    vp = v_past.reshape(v_past.shape[0], nb, past, heads, 2, dh).transpose(0, 1, 2, 4, 3, 5)
    vp = vp.reshape(v_past.shape[0], nb, past * 2 * heads, dh)
    kern = functools.partial(_attn_sample_kernel, heads=heads, dh=dh, past=past, lam_init=lam_init)
    lam_spec = pl.BlockSpec((1, dh), lambda b, c: (0, 0))
    new_spec = pl.BlockSpec((t, heads * dv), lambda b, c: (b, 0))
    return pl.pallas_call(
        kern,
        grid=(nb, past // tc),
        in_specs=[pl.BlockSpec(memory_space=pltpu.SMEM),
                  pl.BlockSpec((t, heads * dv), lambda b, c: (rb0 + b, 0)), new_spec, new_spec,
                  pl.BlockSpec((None, None, tc * heads * 2, dh), lambda b, c: (l, b, c, 0)),
                  pl.BlockSpec((None, None, tc * heads * 2, dh), lambda b, c: (l, b, c, 0)),
                  lam_spec, lam_spec, lam_spec, lam_spec,
                  pl.BlockSpec((heads, dv), lambda b, c: (0, 0))],
        out_specs=pl.BlockSpec((t, heads * dv), lambda b, c: (b, 0)),
        out_shape=jax.ShapeDtypeStruct((nb * t, heads * dv), BF16),
        scratch_shapes=[pltpu.VMEM((2 * heads, t, 1), F32),
                        pltpu.VMEM((2 * heads, t, 1), F32),
                        pltpu.VMEM((2 * heads, t, dv), F32)],
        compiler_params=_params(("arbitrary", "arbitrary")),
        name="attn_sample",
    )(slopes, aq, k_new, v_new, kp, vp, *lams, norm_g)


def _merge_kernel(hmp_ref, hms_ref, hap_ref, has_ref, wa_ref, wb_ref, g0_ref, g1_ref, o_ref, *, nbp):
    i = pl.program_id(0)

    def body(hm_ref, ha_ref):
        ya = _dot(hm_ref[...], wa_ref[...])
        yb = _dot(ha_ref[...], wb_ref[...])
        g0 = jax.nn.sigmoid(g0_ref[...].astype(F32))
        g1 = jax.nn.sigmoid(g1_ref[...].astype(F32))
        o_ref[...] = (g0 * ya + g1 * yb).astype(o_ref.dtype)

    @pl.when(i < nbp)
    def _():
        body(hmp_ref, hap_ref)

    @pl.when(i >= nbp)
    def _():
        body(hms_ref, has_ref)


def _merge(hm_p, hm_s, ha_p, ha_s, w_a, w_b, gates, rows):
    ka, kb = hm_p.shape[1], ha_p.shape[1]
    d = w_a.shape[1]
    tm = rows.tm
    tn = _pick(d, (512, 256, 128))
    nj = d // tn
    pmap = lambda i, j: (rows.p_tile(i), 0)
    smap = lambda i, j: (rows.s_tile(i), 0)
    return pl.pallas_call(
        functools.partial(_merge_kernel, nbp=rows.nbp),
        grid=(rows.nb, nj),
        in_specs=[pl.BlockSpec((tm, ka), pmap), pl.BlockSpec((tm, ka), smap, pipeline_mode=pl.Buffered(1)),
                  pl.BlockSpec((tm, kb), pmap), pl.BlockSpec((tm, kb), smap, pipeline_mode=pl.Buffered(1)),
                  pl.BlockSpec((ka, tn), lambda i, j: (0, j)),
                  pl.BlockSpec((kb, tn), lambda i, j: (0, j)),
                  pl.BlockSpec((tm, tn), lambda i, j: (i, j)),
                  pl.BlockSpec((tm, tn), lambda i, j: (i, nj + j))],
        out_specs=pl.BlockSpec((tm, tn), lambda i, j: (i, j)),
        out_shape=jax.ShapeDtypeStruct((rows.nb * tm, d), BF16),
        compiler_params=_params(("arbitrary", "arbitrary")),
        name="merge",
    )(hm_p, hm_s, ha_p, ha_s, w_a, w_b, gates, gates)


def _outproj_kernel(a_ref, w_ref, xp_ref, xs_ref, gp_ref, gs_ref, o_ref, *, nbp):
    i = pl.program_id(0)
    y = _dot(a_ref[...], w_ref[...])

    def body(x_ref, gt_ref):
        o_ref[...] = x_ref[...] + _mod_apply(y, [gt_ref[...]], lambda y3, gt: gt * y3)

    @pl.when(i < nbp)
    def _():
        body(xp_ref, gp_ref)

    @pl.when(i >= nbp)
    def _():
        body(xs_ref, gs_ref)


def _outproj(a, w, xp, xs, mod, k_gate, rows):
    k = a.shape[1]
    d = w.shape[1]
    tm = rows.tm
    tn = _pick(d, (512, 256, 128))
    gp, gs = rows.mod_specs(k_gate, tn, col=lambda i, j: j)
    return pl.pallas_call(
        functools.partial(_outproj_kernel, nbp=rows.nbp),
        grid=(rows.nb, d // tn),
        in_specs=[pl.BlockSpec((tm, k), lambda i, j: (i, 0)),
                  pl.BlockSpec((k, tn), lambda i, j: (0, j)),
                  pl.BlockSpec((tm, tn), lambda i, j: (rows.p_tile(i), j)),
                  pl.BlockSpec((tm, tn), lambda i, j: (rows.s_tile(i), j)),
                  gp, gs],
        out_specs=pl.BlockSpec((tm, tn), lambda i, j: (i, j)),
        out_shape=jax.ShapeDtypeStruct((rows.nb * tm, d), F32),
        compiler_params=_params(("arbitrary", "arbitrary")),
        name="outproj",
    )(a, w, xp, xs, mod, mod)


def _top_values(s, k):
    vals = []
    for r in range(k):
        m = jnp.max(s, axis=0, keepdims=True)
        vals.append(m)
        if r + 1 < k:
            s = jnp.where(s == m, -jnp.inf, s)
    return vals


def _peer_select_kernel(q_ref, sk_ref, s_ref, aux_ref, *, heads, dsub):
    tm = q_ref.shape[0]
    for h in range(heads):
        tops = []
        for c in range(2):
            qh = q_ref[:, (2 * h + c) * dsub:(2 * h + c + 1) * dsub]
            sk = sk_ref[h, c]
            qh_hi = qh.astype(BF16)
            qh_lo = (qh - qh_hi.astype(F32)).astype(BF16)
            sk_hi = sk.astype(BF16)
            sk_lo = (sk - sk_hi.astype(F32)).astype(BF16)
            s = _dot_nt(sk_hi, qh_hi) + _dot_nt(sk_hi, qh_lo) + _dot_nt(sk_lo, qh_hi)
            s_ref[h, c] = s
            tops.append(_top_values(s, P_TOPK))
        t2 = jnp.concatenate(tops[1], axis=0)
        half = P_TOPK // 2
        cand = jnp.concatenate([tops[0][0] + t2]
                               + [tops[0][a] + t2[0:half] for a in range(1, half)]
                               + [jnp.concatenate(tops[0][half:], axis=0) + tops[1][0]], axis=0)
        best = _top_values(cand, P_TOPK)
        z = jnp.ones((1, tm), F32)
        for r in range(1, P_TOPK):
            z = z + jnp.exp(best[r] - best[0])
        rows = [best[P_TOPK - 1], tops[0][0], tops[1][0], 1.0 / z]
        aux_ref[h] = jnp.concatenate(rows + [jnp.zeros((8 - len(rows), tm), F32)], axis=0)


def _peer_select(qp, sub_keys, l):
    n = qp.shape[0]
    _, heads, _, nkeys, dsub = sub_keys.shape
    tm = _pick(n, (512, 256, 128))
    kern = functools.partial(_peer_select_kernel, heads=heads, dsub=dsub)
    return pl.pallas_call(
        kern,
        grid=(n // tm,),
        in_specs=[pl.BlockSpec((tm, qp.shape[1]), lambda i: (i, 0)),
                  pl.BlockSpec((None,) + sub_keys.shape[1:], lambda i: (l, 0, 0, 0, 0))],
        out_specs=[pl.BlockSpec((heads, 2, nkeys, tm), lambda i: (0, 0, 0, i)),
                   pl.BlockSpec((heads, 8, tm), lambda i: (0, 0, i))],
        out_shape=[jax.ShapeDtypeStruct((heads, 2, nkeys, n), F32),
                   jax.ShapeDtypeStruct((heads, 8, n), F32)],
        compiler_params=_params(("parallel",)),
        name="peer_select",
    )(qp, sub_keys)


def _peer_dense_kernel(x_ref, u_ref, v_ref, s1_ref, s2_ref, aux_ref, o_ref,
                       s1t_ref, e1_ref, e2_ref, p_ref, pblk_ref, *, heads, nkeys):
    te = u_ref.shape[0]
    tm = x_ref.shape[0]
    e = pl.program_id(1)
    nsub = te // nkeys
    grows = te // PEER_SPLIT
    gsub = nsub // PEER_SPLIT

    @pl.when(e == 0)
    def _():
        o_ref[...] = jnp.zeros(o_ref.shape, F32)
        for h in range(heads):
            e2_ref[h] = jnp.exp(s2_ref[h] - aux_ref[h, 2:3, :]) * aux_ref[h, 3:4, :]

    scores = [_dot_nt(u_ref[g * grows:(g + 1) * grows, :], x_ref[...]) for g in range(PEER_SPLIT)]

    grp = e % (8 // nsub)
    for h in range(heads):
        rows = s1_ref[h, 0:nsub, :]
        for gi in range(1, 8 // nsub):
            rows = jnp.where(grp == gi, s1_ref[h, gi * nsub:(gi + 1) * nsub, :], rows)
        s1t_ref[h, 0:nsub, :] = rows
        e1_ref[h, 0:nsub, :] = jnp.exp(rows - aux_ref[h, 1:2, :])

    for g in range(PEER_SPLIT):
        for tb in range(tm // LANES):
            cs = slice(tb * LANES, (tb + 1) * LANES)
            for kb in range(nkeys // PEER_KEY_BLOCK):
                rs = slice(kb * PEER_KEY_BLOCK, (kb + 1) * PEER_KEY_BLOCK)
                w = [None] * gsub
                for h in range(heads):
                    s2_blk = s2_ref[h, rs, cs]
                    e2_blk = e2_ref[h, rs, cs]
                    tau = aux_ref[h, 0:1, cs]
                    for a in range(gsub):
                        ar = g * gsub + a
                        hit = s1t_ref[h, ar:ar + 1, cs] + s2_blk >= tau
                        gate = jnp.where(hit, e1_ref[h, ar:ar + 1, cs] * e2_blk, 0.0)
                        w[a] = gate if w[a] is None else w[a] + gate
                for a in range(gsub):
                    r0 = a * nkeys + kb * PEER_KEY_BLOCK
                    act = jax.nn.gelu(scores[g][r0:r0 + PEER_KEY_BLOCK, cs], approximate=True)
                    pblk_ref[tb, g * gsub + a, rs, :] = w[a] * act
            for a in range(gsub):
                ar = g * gsub + a
                p_ref[cs, ar * nkeys:(ar + 1) * nkeys] = pblk_ref[tb, ar].T.astype(BF16)
        o_ref[...] += _dot(p_ref[:, g * grows:(g + 1) * grows], v_ref[g * grows:(g + 1) * grows, :])


def _peer_dense(x, u_tab, v_tab, s_t, aux):
    n, d = x.shape
    ne = u_tab.shape[0]
    heads, _, nkeys, _ = s_t.shape
    tm = _pick(n, (512, 256, 128))
    te = _pick(ne, (512, 256, 128))
    assert te % nkeys == 0 and ne == nkeys * nkeys
    n_tiles = ne // te
    nsub = te // nkeys
    assert 8 % nsub == 0 and nkeys % 8 == 0 and nsub % PEER_SPLIT == 0
    kern = functools.partial(_peer_dense_kernel, heads=heads, nkeys=nkeys)
    once = pl.Buffered(1)
    return pl.pallas_call(
        kern,
        grid=(n // tm, n_tiles),
        in_specs=[pl.BlockSpec((tm, d), lambda i, e: (i, 0), pipeline_mode=once),
                  pl.BlockSpec((te, d), lambda i, e: (e, 0)),
                  pl.BlockSpec((te, d), lambda i, e: (e, 0)),
                  pl.BlockSpec((heads, None, 8, tm), lambda i, e: (0, 0, e * nsub // 8, i)),
                  pl.BlockSpec((heads, None, nkeys, tm), lambda i, e: (0, 1, 0, i), pipeline_mode=once),
                  pl.BlockSpec((heads, 8, tm), lambda i, e: (0, 0, i), pipeline_mode=once)],
        out_specs=pl.BlockSpec((tm, d), lambda i, e: (i, 0), pipeline_mode=once),
        out_shape=jax.ShapeDtypeStruct((n, d), F32),
        scratch_shapes=[pltpu.VMEM((heads, 8, tm), F32),
                        pltpu.VMEM((heads, 8, tm), F32),
                        pltpu.VMEM((heads, nkeys, tm), F32),
                        pltpu.VMEM((tm, te), BF16),
                        pltpu.VMEM((tm // LANES, nsub, nkeys, LANES), F32)],
        compiler_params=_params(("parallel", "arbitrary")),
        name="peer_dense",
    )(x, u_tab, v_tab, s_t, s_t, aux)


def _final_kernel(x_ref, p_ref, gp_ref, gs_ref, g_ref, yp_ref, ys_ref, *, nbp):
    i = pl.program_id(0)

    def body(gt_ref, y_ref):
        x = x_ref[...] + _mod_apply(p_ref[...], [gt_ref[...]], lambda p3, gt: gt * p3)
        y_ref[...] = x * lax.rsqrt(jnp.mean(x * x, -1, keepdims=True) + EPS) * g_ref[...]

    @pl.when(i < nbp)
    def _():
        body(gp_ref, yp_ref)

    @pl.when(i >= nbp)
    def _():
        body(gs_ref, ys_ref)


def _final(x, peer, mod, k_gate, g_final, rows):
    d = x.shape[1]
    tm = rows.tm
    gp, gs = rows.mod_specs(k_gate, d)
    return pl.pallas_call(
        functools.partial(_final_kernel, nbp=rows.nbp),
        grid=(rows.nb,),
        in_specs=[pl.BlockSpec((tm, d), lambda i: (i, 0)),
                  pl.BlockSpec((tm, d), lambda i: (i, 0)),
                  gp, gs,
                  pl.BlockSpec((1, d), lambda i: (0, 0))],
        out_specs=[pl.BlockSpec((tm, d), lambda i: (rows.p_tile(i), 0)),
                   pl.BlockSpec((tm, d), lambda i: (rows.s_tile(i), 0))],
        out_shape=[jax.ShapeDtypeStruct((rows.nbp * tm, d), F32),
                   jax.ShapeDtypeStruct((rows.nbs * tm, d), F32)],
        compiler_params=_params(("arbitrary",)),
        name="final_norm",
    )(x, peer, mod, mod, g_final.reshape(1, d))


def kernel(x_prompt, x_sample, cache_k, cache_v, state_C, state_n, state_m, state_conv, c_prompt, c_sample, w_ada, b_ada, g_mix, w_in, conv_w, conv_b, b_gates, m_norm_g, lam_q1, lam_k1, lam_q2, lam_k2, a_norm_g, w_a, w_b, w_out, g_ffn, w_pq, sub_keys, u_tab, v_tab, g_final):
    bp, tp, d = x_prompt.shape
    bs, ts, _ = x_sample.shape
    depth = w_in.shape[0]
    assert depth == 1, "the per-layer state plumbing below is written for a single layer"
    l = 0
    heads, dv = m_norm_g.shape[1:]
    dk = state_C.shape[-1]
    a_heads, a_dv = a_norm_g.shape[1:]
    dh = a_dv // 2
    n_p, n_s = bp * tp, bs * ts
    xp = x_prompt.reshape(n_p, d)
    xs = x_sample.reshape(n_s, d)
    rows_ew = _Rows(bp, tp, bs, ts, _pick(math.gcd(tp, n_s), (256, 128, 64, 32)))
    rows_mm = _Rows(bp, tp, bs, ts, _pick(math.gcd(tp, n_s), (1024, 512, 256, 128, 64, 32)))

    qk_w, v_w = 2 * heads * dk, heads * dv
    a_w = a_heads * a_dv
    col_mv = qk_w
    col_gate = qk_w + 2 * v_w
    col_aq = col_gate + 2 * heads
    col_ak = col_aq + a_w
    col_av = col_ak + a_w
    col_bg = col_av + a_w

    c_all = jnp.concatenate([c_sample, c_prompt], axis=0)
    c_all = jnp.pad(c_all, ((0, (-c_all.shape[0]) % 8), (0, 0)))
    mod = _ada(c_all, w_ada, b_ada, l).reshape(c_all.shape[0], 6, 1, d)

    hn = _norm_mod(xp, xs, 0, rows_ew, g_mix[l], mod, 1, 0)
    w_t = jnp.swapaxes(w_in, 1, 2)
    qk_raw, = _proj(hn, w_t, l, 0, qk_w, "f32", n_p)
    vo, = _proj(hn, w_t, l, col_mv, 2 * v_w, "bf16", n_p)
    aq, = _proj(hn, w_t, l, col_aq, a_w, "bf16", n_p, scale=LOG2E * dh ** -0.5)
    ak, k_p, k_s = _proj(hn, w_t, l, col_ak, a_w, "kv", n_p)
    av_t, v_p, v_s = _proj(hn, w_t, l, col_av, a_w, "kvt", n_p)
    bgates, = _proj(hn, w_t, l, col_bg, 2 * d, "bf16", n_p)
    gates = _gate_proj(hn, w_t, l, col_gate)

    zc = jnp.zeros((bp, heads, dv, dk), F32)
    zn = jnp.zeros((bp, heads, dk), F32)
    zm = jnp.zeros((bp, heads), F32)
    zconv = jnp.zeros((bp, CONV_W - 1, qk_w), F32)
    mp = _mlstm(qk_raw, vo, gates, 0, bp, tp, zconv, conv_w[l], conv_b[l], b_gates[l], m_norm_g[l], zc, zn, zm)
    ms = _mlstm(qk_raw, vo, gates, n_p, bs, ts, state_conv[l], conv_w[l], conv_b[l], b_gates[l], m_norm_g[l],
                state_C[l], state_n[l], state_m[l])

    lam_init = 0.8 - 0.6 * math.exp(-0.3 * l)
    slopes = jnp.array([2.0 ** (-8.0 * (h + 1) / a_heads) for h in range(a_heads)], F32)
    lams = [a[l].reshape(1, -1) for a in (lam_q1, lam_k1, lam_q2, lam_k2)]
    ha_p = _attn_prompt(aq, ak, av_t, bp, tp, slopes, lams, a_norm_g[l], lam_init)
    ha_s = _attn_sample(aq, k_s, v_s, n_p, bs, ts, cache_k, cache_v, l, slopes, lams, a_norm_g[l], lam_init)

    merged = _merge(mp[0], ms[0], ha_p, ha_s, _cast_bf16(w_a, l), _cast_bf16(w_b, l), bgates, rows_mm)
    x1 = _outproj(merged, _cast_bf16(w_out, l), xp, xs, mod, 2, rows_mm)

    hn2 = _norm_mod(x1, x1, rows_ew.nbp, rows_ew, g_ffn[l], mod, 4, 3)
    qp = _matmul(hn2, _cast_bf16(w_pq, l), F32, "peer_query")
    s_t, aux = _peer_select(qp, sub_keys, l)
    peer = _peer_dense(hn2, _cast_bf16(u_tab, l), _cast_bf16(v_tab, l), s_t, aux)
    y_p, y_s = _final(x1, peer, mod, 5, g_final, rows_ew)

    return (y_p.reshape(bp, tp, d), y_s.reshape(bs, ts, d),
            k_p.reshape(1, bp, tp, a_heads, 2, dh), v_p.reshape(1, bp, tp, a_heads, a_dv),
            mp[1][None], mp[2][None], mp[3][None], mp[4][None],
            k_s.reshape(1, bs, ts, a_heads, 2, dh), v_s.reshape(1, bs, ts, a_heads, a_dv),
            ms[1][None], ms[2][None], ms[3][None], ms[4][None])
```

```python
import functools
import math

import jax
import jax.numpy as jnp
from jax import lax
from jax.experimental import pallas as pl
from jax.experimental.pallas import tpu as pltpu

F32 = jnp.float32
BF16 = jnp.bfloat16

EPS = 1e-6
ATTN_CHUNK = 64
ATTN_KEY_BLOCK = 512
CONV_W = 4
P_TOPK = 16
PEER_KEY_BLOCK = 32
PEER_SPLIT = 1
LANES = 128
MLSTM_CHUNK = 256
VMEM_LIMIT_MB = 56
LOG2E = math.log2(math.e)


def _params(sem, vmem_mb=VMEM_LIMIT_MB):
    return pltpu.CompilerParams(dimension_semantics=sem, vmem_limit_bytes=vmem_mb << 20)


def _pick(n, cands):
    for c in cands:
        if n % c == 0:
            return c
    raise ValueError(f"no tile in {cands} divides {n}")


def _dot(a, b):
    return jnp.dot(a, b, preferred_element_type=F32)


def _dot_nt(a, b):
    return lax.dot_general(a, b, (((1,), (1,)), ((), ())), preferred_element_type=F32)


def _dot_tn(a, b):
    return lax.dot_general(a, b, (((0,), (0,)), ((), ())), preferred_element_type=F32)


def _split3(x):
    hi = x.astype(BF16)
    r = x - hi.astype(F32)
    mid = r.astype(BF16)
    lo = (r - mid.astype(F32)).astype(BF16)
    return hi, mid, lo


def _dot_exact_lhs(a01, x):
    hi, mid, lo = _split3(x)
    return _dot(a01, hi) + _dot(a01, mid) + _dot(a01, lo)


def _dot_exact_nt(a01, x):
    hi, mid, lo = _split3(x)
    return _dot_nt(a01, hi) + _dot_nt(a01, mid) + _dot_nt(a01, lo)


class _Rows:
    def __init__(self, bp, tp, bs, ts, tm):
        assert (bp * tp) % tm == 0 and (bs * ts) % tm == 0
        assert tp % tm == 0 and tm % ts == 0, "a prompt tile sits in one sequence, a sample tile holds whole ones"
        self.bp, self.tp, self.bs, self.ts, self.tm = bp, tp, bs, ts, tm
        self.nbp = bp * tp // tm
        self.nbs = bs * ts // tm
        self.nb = self.nbp + self.nbs
        self.ng = tm // ts

    def p_tile(self, i):
        return jnp.minimum(i, self.nbp - 1)

    def s_tile(self, i):
        return jnp.maximum(i - self.nbp, 0)

    def mod_specs(self, k, width, col=None):
        cj = (lambda *g: 0) if col is None else col
        p = pl.BlockSpec((None, None, 1, width),
                         lambda *g: (self.bs + (self.p_tile(g[0]) * self.tm) // self.tp, k, 0, cj(*g)))
        s = pl.BlockSpec((self.ng, None, 1, width), lambda *g: (self.s_tile(g[0]), k, 0, cj(*g)))
        return p, s


def _mod_apply(x, gate_rows, fn):
    tm, w = x.shape
    rows = [r.reshape((-1, 1, w)) for r in gate_rows]
    ng = rows[0].shape[0]
    return fn(x.reshape(ng, tm // ng, w), *rows).reshape(tm, w)


def _cast_kernel(w_ref, o_ref):
    o_ref[...] = w_ref[...].astype(o_ref.dtype)


def _cast_bf16(w3, l):
    _, r, c = w3.shape
    tr = _pick(r, (512, 256, 128))
    tc = _pick(c, (2048, 1024, 512, 256, 128))
    return pl.pallas_call(
        _cast_kernel,
        grid=(r // tr, c // tc),
        in_specs=[pl.BlockSpec((None, tr, tc), lambda i, j: (l, i, j))],
        out_specs=pl.BlockSpec((tr, tc), lambda i, j: (i, j)),
        out_shape=jax.ShapeDtypeStruct((r, c), BF16),
        compiler_params=_params(("parallel", "parallel")),
        name="cast_bf16",
    )(w3)


def _ada_kernel(c_ref, w_ref, b_ref, o_ref):
    c = c_ref[...]
    a = (c * jax.nn.sigmoid(c)).astype(BF16)
    o_ref[...] = _dot(a, w_ref[...].astype(BF16)) + b_ref[...]


def _ada(c_all, w_ada, b_ada, l):
    m, d = c_all.shape
    n = w_ada.shape[2]
    tn = _pick(n, (512, 256, 128))
    return pl.pallas_call(
        _ada_kernel,
        grid=(n // tn,),
        in_specs=[pl.BlockSpec((m, d), lambda j: (0, 0)),
                  pl.BlockSpec((None, d, tn), lambda j: (l, 0, j)),
                  pl.BlockSpec((None, 1, tn), lambda j: (l, 0, j))],
        out_specs=pl.BlockSpec((m, tn), lambda j: (0, j)),
        out_shape=jax.ShapeDtypeStruct((m, n), F32),
        compiler_params=_params(("parallel",)),
        name="ada_mod",
    )(c_all, w_ada, b_ada.reshape(b_ada.shape[0], 1, n))


def _norm_mod_kernel(xp_ref, xs_ref, g_ref, scp_ref, shp_ref, scs_ref, shs_ref, o_ref, *, nbp):
    i = pl.program_id(0)

    def body(x_ref, sc_ref, sh_ref):
        x = x_ref[...]
        y = x * lax.rsqrt(jnp.mean(x * x, -1, keepdims=True) + EPS) * g_ref[...]
        y = _mod_apply(y, [sc_ref[...], sh_ref[...]], lambda y3, sc, sh: y3 * (1.0 + sc) + sh)
        o_ref[...] = y.astype(o_ref.dtype)

    @pl.when(i < nbp)
    def _():
        body(xp_ref, scp_ref, shp_ref)

    @pl.when(i >= nbp)
    def _():
        body(xs_ref, scs_ref, shs_ref)


def _norm_mod(xp, xs, s_off, rows, g, mod, k_scale, k_shift):
    d = xp.shape[1]
    tm = rows.tm
    scp, scs = rows.mod_specs(k_scale, d)
    shp, shs = rows.mod_specs(k_shift, d)
    return pl.pallas_call(
        functools.partial(_norm_mod_kernel, nbp=rows.nbp),
        grid=(rows.nb,),
        in_specs=[pl.BlockSpec((tm, d), lambda i: (rows.p_tile(i), 0)),
                  pl.BlockSpec((tm, d), lambda i: (s_off + rows.s_tile(i), 0)),
                  pl.BlockSpec((1, d), lambda i: (0, 0)),
                  scp, shp, scs, shs],
        out_specs=pl.BlockSpec((tm, d), lambda i: (i, 0)),
        out_shape=jax.ShapeDtypeStruct((rows.nb * tm, d), BF16),
        compiler_params=_params(("arbitrary",)),
        name="norm_mod",
    )(xp, xs, g.reshape(1, d), mod, mod, mod, mod)


def _proj_kernel(x_ref, w_ref, wn_ref, *refs, shift, scale, mode, nip):
    outs, wb_ref = refs[:-1], refs[-1]
    i = pl.program_id(1)

    @pl.when(i == 0)
    def _():
        tn = w_ref.shape[0]
        cr = _pick(tn, (256, 128))
        for r in range(0, tn, cr):
            if r + cr + shift <= tn:
                wb_ref[r:r + cr, :] = w_ref[r + shift:r + cr + shift, :].astype(BF16)
            else:
                wb_ref[r:r + cr - shift, :] = w_ref[r + shift:r + cr, :].astype(BF16)
                wb_ref[r + cr - shift:r + cr, :] = wn_ref[...].astype(BF16)

    acc = _dot_nt(x_ref[...], wb_ref[...])
    if scale != 1.0:
        acc = acc * scale
    if mode in ("kv", "kvt"):
        outs[0][...] = (acc.T if mode == "kvt" else acc).astype(BF16)

        @pl.when(i < nip)
        def _():
            outs[1][...] = acc

        @pl.when(i >= nip)
        def _():
            outs[2][...] = acc
    else:
        outs[0][...] = acc.astype(outs[0].dtype)


def _proj(hn, w_t, l, col0, ncols, mode, n_p, scale=1.0):
    n, k = hn.shape
    tm = _pick(math.gcd(n_p, n - n_p), ((1024,) if mode != "kvt" else ()) + (ATTN_KEY_BLOCK, 256, 128))
    tn = _pick(ncols, ((1024,) if mode == "bf16" else ()) + (512, 256, 128))
    shift = col0 % tn
    assert (col0 - shift) % tn == 0 and shift % 16 == 0 and shift < 256
    cb0 = (col0 - shift) // tn
    nip = n_p // tm
    ni, nj = n // tm, ncols // tn
    once = pl.Buffered(1)
    nxt = shift if shift else 16
    in_specs = [pl.BlockSpec((tm, k), lambda j, i: (i, 0)),
                pl.BlockSpec((None, tn, k), lambda j, i: (l, cb0 + j, 0), pipeline_mode=once),
                pl.BlockSpec((None, nxt, k), lambda j, i: (l, (cb0 + j + 1) * (tn // nxt) if shift else 0, 0),
                             pipeline_mode=once)]
    all_spec = pl.BlockSpec((tm, tn), lambda j, i: (i, j))
    if mode in ("kv", "kvt"):
        first = (pl.BlockSpec((None, tn, tm), lambda j, i: (i, j, 0)) if mode == "kvt" else all_spec)
        out_specs = [first,
                     pl.BlockSpec((tm, tn), lambda j, i: (jnp.minimum(i, nip - 1), j)),
                     pl.BlockSpec((tm, tn), lambda j, i: (jnp.maximum(i - nip, 0), j))]
        out_shape = [jax.ShapeDtypeStruct((ni, ncols, tm) if mode == "kvt" else (n, ncols), BF16),
                     jax.ShapeDtypeStruct((n_p, ncols), F32),
                     jax.ShapeDtypeStruct((n - n_p, ncols), F32)]
    else:
        out_specs = [all_spec]
        out_shape = [jax.ShapeDtypeStruct((n, ncols), F32 if mode == "f32" else BF16)]
    kern = functools.partial(_proj_kernel, shift=shift, scale=scale, mode=mode, nip=nip)
    return pl.pallas_call(
        kern,
        grid=(nj, ni),
        in_specs=in_specs,
        out_specs=out_specs,
        out_shape=out_shape,
        scratch_shapes=[pltpu.VMEM((tn, k), BF16)],
        compiler_params=_params(("arbitrary", "arbitrary")),
        name="proj_" + mode,
    )(hn, w_t, w_t)


def _gate_kernel(x_ref, w_ref, o_ref):
    o_ref[...] = _dot_nt(x_ref[...], w_ref[...].astype(BF16))


def _gate_proj(hn, w_t, l, col0):
    n, k = hn.shape
    assert col0 % LANES == 0
    tm = _pick(n, (1024, 512, 256, 128))
    return pl.pallas_call(
        _gate_kernel,
        grid=(n // tm,),
        in_specs=[pl.BlockSpec((tm, k), lambda i: (i, 0)),
                  pl.BlockSpec((None, LANES, k), lambda i: (l, col0 // LANES, 0))],
        out_specs=pl.BlockSpec((tm, LANES), lambda i: (i, 0)),
        out_shape=jax.ShapeDtypeStruct((n, LANES), F32),
        compiler_params=_params(("parallel",)),
        name="gate_proj",
    )(hn, w_t)


def _mm_kernel(a_ref, b_ref, o_ref):
    o_ref[...] = _dot(a_ref[...], b_ref[...]).astype(o_ref.dtype)


def _matmul(a, b, out_dtype, name):
    m, k = a.shape
    n = b.shape[1]
    tm = _pick(m, (1024, 512, 256, 128))
    tn = _pick(n, (1024, 512, 256, 128))
    return pl.pallas_call(
        _mm_kernel,
        grid=(m // tm, n // tn),
        in_specs=[pl.BlockSpec((tm, k), lambda i, j: (i, 0)),
                  pl.BlockSpec((k, tn), lambda i, j: (0, j))],
        out_specs=pl.BlockSpec((tm, tn), lambda i, j: (i, j)),
        out_shape=jax.ShapeDtypeStruct((m, n), out_dtype),
        compiler_params=_params(("parallel", "parallel")),
        name=name,
    )(a, b)


def _mlstm_kernel(qk_ref, v_ref, og_ref, gt_ref, conv0_ref, cw_ref, cb_ref, bg_ref, ng_ref,
                  c0_ref, n0_ref, m0_ref,
                  hm_ref, c_ref, n_ref, m_ref, convn_ref, full_ref, *, heads, dk, dv):
    L = qk_ref.shape[0]
    pad = 8
    hist = CONV_W - 1
    ci = pl.program_id(1)

    @pl.when(ci == 0)
    def _():
        c_ref[...] = c0_ref[...]
        n_ref[...] = n0_ref[...]
        m_ref[...] = m0_ref[...]
        full_ref[pad - hist:pad, :] = conv0_ref[0]

    u = qk_ref[...]
    full_ref[pad:pad + L, :] = u
    acc = cb_ref[...] + u * cw_ref[hist:hist + 1, :]
    for j in range(hist):
        acc = acc + full_ref[pad - hist + j:pad - hist + j + L, :] * cw_ref[j:j + 1, :]
    tail = full_ref[pad + L - hist:pad + L, :]
    convn_ref[0] = tail
    full_ref[pad - hist:pad, :] = tail
    qk = acc * jax.nn.sigmoid(acc)

    row = lax.broadcasted_iota(jnp.int32, (L, L), 0)
    col = lax.broadcasted_iota(jnp.int32, (L, L), 1)
    causal = col <= row
    tril01 = jnp.where(causal, 1.0, 0.0).astype(BF16)
    glanes = gt_ref.shape[1]
    pre = gt_ref[...] + bg_ref[...]
    logf_all = jnp.minimum(pre, 0.0) - jnp.log1p(jnp.exp(-jnp.abs(pre)))
    b_all = _dot_exact_lhs(tril01, logf_all)
    eye01 = jnp.where(lax.broadcasted_iota(jnp.int32, (glanes, glanes), 0)
                      == lax.broadcasted_iota(jnp.int32, (glanes, glanes), 1), 1.0, 0.0).astype(BF16)
    pre_t = _dot_exact_nt(eye01, pre)
    b_t = _dot_exact_nt(eye01, b_all)

    for h in range(heads):
        q = qk[:, h * dk:(h + 1) * dk]
        k = qk[:, (heads + h) * dk:(heads + h + 1) * dk] * (dk ** -0.5)
        vb = v_ref[:, h * dv:(h + 1) * dv]
        v = vb.astype(F32)
        qb, kb = q.astype(BF16), k.astype(BF16)
        i_col = pre[:, h:h + 1]
        b_col = b_all[:, heads + h:heads + h + 1]
        dmat = jnp.where(causal, b_col + (pre_t[h:h + 1, :] - b_t[heads + h:heads + h + 1, :]), -jnp.inf)
        m_prev = m_ref[0, h:h + 1, 0:1]
        inter = b_col + m_prev
        m_t = jnp.maximum(inter, jnp.max(dmat, axis=1, keepdims=True))
        s = _dot_nt(qb, kb) * jnp.exp(dmat - m_t)
        w_prev = jnp.exp(inter - m_t)
        c_old = c_ref[0, h]
        n_old = n_ref[0, h:h + 1, :]
        num = _dot(s.astype(BF16), vb) + w_prev * _dot_nt(qb, c_old.astype(BF16))
        den = jnp.sum(s, axis=1, keepdims=True) + w_prev * jnp.sum(q * n_old, axis=1, keepdims=True)
        hh = num / jnp.maximum(jnp.abs(den), jnp.exp(-m_t))
        m_new = m_t[L - 1:L, :]
        b_last = b_col[L - 1:L, :]
        decay = jnp.exp(b_last + m_prev - m_new)
        w_src = jnp.exp(b_last - b_col + i_col - m_new)
        c_ref[0, h] = decay * c_old + _dot_tn((w_src * v).astype(BF16), kb)
        n_ref[0, h:h + 1, :] = decay * n_old + jnp.sum(w_src * k, axis=0, keepdims=True)
        m_ref[0, h:h + 1, :] = jnp.broadcast_to(m_new, (1, m_ref.shape[2]))
        hn = hh * lax.rsqrt(jnp.mean(hh * hh, -1, keepdims=True) + EPS) * ng_ref[h:h + 1, :]
        hn = hn * jax.nn.sigmoid(og_ref[:, h * dv:(h + 1) * dv].astype(F32))
        hm_ref[:, h * dv:(h + 1) * dv] = hn.astype(hm_ref.dtype)


def _mlstm(qk_raw, vo, gates, row0, nb, t, conv0, conv_w, conv_b, b_gates, norm_g, c0, n0, m0):
    heads, dv = norm_g.shape
    dk = c0.shape[-1]
    qkw = 2 * heads * dk
    vw = heads * dv
    L = _pick(t, (MLSTM_CHUNK, 128, 64, 32, 16, 8))
    nc = t // L
    assert row0 % L == 0 and L >= CONV_W - 1
    rb0 = row0 // L
    m0b = jnp.broadcast_to(m0[:, :, None], (nb, heads, LANES))
    glanes = gates.shape[1]
    assert 2 * heads <= glanes
    gate_bias = jnp.pad(b_gates.reshape(1, 2 * heads), ((0, 0), (0, glanes - 2 * heads)))
    rmap = lambda b, c: rb0 + b * nc + c
    kern = functools.partial(_mlstm_kernel, heads=heads, dk=dk, dv=dv)
    hm, c1, n1, m1, convn = pl.pallas_call(
        kern,
        grid=(nb, nc),
        in_specs=[pl.BlockSpec((L, qkw), lambda b, c: (rmap(b, c), 0)),
                  pl.BlockSpec((L, vw), lambda b, c: (rmap(b, c), 0)),
                  pl.BlockSpec((L, vw), lambda b, c: (rmap(b, c), 1)),
                  pl.BlockSpec((L, gates.shape[1]), lambda b, c: (rmap(b, c), 0)),
                  pl.BlockSpec((1, CONV_W - 1, qkw), lambda b, c: (b, 0, 0)),
                  pl.BlockSpec((CONV_W, qkw), lambda b, c: (0, 0)),
                  pl.BlockSpec((1, qkw), lambda b, c: (0, 0)),
                  pl.BlockSpec((1, glanes), lambda b, c: (0, 0)),
                  pl.BlockSpec((heads, dv), lambda b, c: (0, 0)),
                  pl.BlockSpec((1, heads, dv, dk), lambda b, c: (b, 0, 0, 0)),
                  pl.BlockSpec((1, heads, dk), lambda b, c: (b, 0, 0)),
                  pl.BlockSpec((1, heads, LANES), lambda b, c: (b, 0, 0))],
        out_specs=[pl.BlockSpec((L, vw), lambda b, c: (b * nc + c, 0)),
                   pl.BlockSpec((1, heads, dv, dk), lambda b, c: (b, 0, 0, 0)),
                   pl.BlockSpec((1, heads, dk), lambda b, c: (b, 0, 0)),
                   pl.BlockSpec((1, heads, LANES), lambda b, c: (b, 0, 0)),
                   pl.BlockSpec((1, CONV_W - 1, qkw), lambda b, c: (b, 0, 0))],
        out_shape=[jax.ShapeDtypeStruct((nb * t, vw), BF16),
                   jax.ShapeDtypeStruct((nb, heads, dv, dk), F32),
                   jax.ShapeDtypeStruct((nb, heads, dk), F32),
                   jax.ShapeDtypeStruct((nb, heads, LANES), F32),
                   jax.ShapeDtypeStruct((nb, CONV_W - 1, qkw), F32)],
        scratch_shapes=[pltpu.VMEM((8 + L, qkw), F32)],
        compiler_params=_params(("parallel", "arbitrary")),
        name="mlstm",
    )(qk_raw, vo, vo, gates, conv0, conv_w, conv_b.reshape(1, qkw), gate_bias, norm_g, c0, n0, m0b)
    return hm, c1, n1, m1[:, :, 0], convn


def _lambda(lq1, lk1, lq2, lk2, lam_init):
    return (jnp.exp(jnp.sum(lq1[...] * lk1[...], axis=1, keepdims=True))
            - jnp.exp(jnp.sum(lq2[...] * lk2[...], axis=1, keepdims=True)) + lam_init)


def _head_norm_out(o, g_row, lam_init):
    return o * lax.rsqrt(jnp.mean(o * o, -1, keepdims=True) + EPS) * g_row * (1.0 - lam_init)


def _attn_prompt_kernel(slope_ref, q_ref, k_ref, vt_ref, lq1, lk1, lq2, lk2, ng_ref, o_ref,
                        m_ref, l_ref, acc_ref, bias_ref, dbias_ref, *, dh, lam_init):
    tq = q_ref.shape[0]
    tk = vt_ref.shape[2]
    h = pl.program_id(1)
    qi = pl.program_id(2)
    slope2 = slope_ref[h] * LOG2E

    @pl.when(qi == 0)
    def _():
        krow = lax.broadcasted_iota(jnp.int32, (tk, tq), 0)
        qcol = lax.broadcasted_iota(jnp.int32, (tk, tq), 1)
        base = slope2 * krow.astype(F32)
        bias_ref[...] = base
        ahead = jnp.maximum(krow - qcol, 0).astype(F32)
        vis = (krow // ATTN_CHUNK) <= (qcol // ATTN_CHUNK)
        dbias_ref[...] = jnp.where(vis, base - (2.0 * slope2) * ahead, -jnp.inf)

    m_ref[...] = jnp.full(m_ref.shape, -jnp.inf, F32)
    l_ref[...] = jnp.zeros(l_ref.shape, F32)
    acc_ref[...] = jnp.zeros(acc_ref.shape, F32)

    def update(kjs, b_ref):
        kbs = [k_ref[pl.ds(pl.multiple_of(kj * tk, tk), tk), :] for kj in kjs]
        vts = [vt_ref[kj] for kj in kjs]
        shifts = [slope2 * ((kj * tk - qi * tq).astype(F32)) for kj in kjs]
        ss = [[_dot_nt(kb[:, c * dh:(c + 1) * dh], q_ref[:, c * dh:(c + 1) * dh]) for kb in kbs]
              for c in range(2)]
        ps, alphas = [], []
        for c in range(2):
            sb = [s + b_ref[...] for s in ss[c]]
            m_old = m_ref[c]
            m_new = m_old
            for s, shift in zip(sb, shifts):
                m_new = jnp.maximum(m_new, jnp.max(s, axis=0, keepdims=True) + shift)
            alpha = jnp.exp2(m_old - m_new)
            pc = [jnp.exp2(s - (m_new - shift)) for s, shift in zip(sb, shifts)]
            l_new = alpha * l_ref[c]
            for p in pc:
                l_new = l_new + jnp.sum(p, axis=0, keepdims=True)
            l_ref[c] = l_new
            m_ref[c] = m_new
            ps.append([p.astype(BF16) for p in pc])
            alphas.append(alpha)
        for c in range(2):
            acc = alphas[c] * acc_ref[c]
            for vt, p in zip(vts, ps[c]):
                acc = acc + _dot(vt, p)
            acc_ref[c] = acc

    def body(kj, carry):
        update([kj], bias_ref)
        return carry

    lax.fori_loop(0, qi, body, 0)
    update([qi], dbias_ref)
    lam = _lambda(lq1, lk1, lq2, lk2, lam_init)
    o = acc_ref[0] / l_ref[0] - lam * (acc_ref[1] / l_ref[1])
    o = o * lax.rsqrt(jnp.mean(o * o, axis=0, keepdims=True) + EPS)
    o_ref[...] = (o.T * ng_ref[...] * (1.0 - lam_init)).astype(o_ref.dtype)


def _attn_prompt(aq, ak, av_t, nb, t, slopes, lams, norm_g, lam_init):
    heads, dv = norm_g.shape
    dh = dv // 2
    tk = av_t.shape[2]
    tq = tk
    nq = t // tq
    assert t % tq == 0 and tq % ATTN_CHUNK == 0 and tq % LANES == 0
    kern = functools.partial(_attn_prompt_kernel, dh=dh, lam_init=lam_init)
    lam_spec = pl.BlockSpec((1, dh), lambda b, h, i: (0, 0))
    return pl.pallas_call(
        kern,
        grid=(nb, heads, nq),
        in_specs=[pl.BlockSpec(memory_space=pltpu.SMEM),
                  pl.BlockSpec((tq, dv), lambda b, h, i: (b * nq + i, h)),
                  pl.BlockSpec((t, dv), lambda b, h, i: (b, h)),
                  pl.BlockSpec((nq, dv, tk), lambda b, h, i: (b, h, 0)),
                  lam_spec, lam_spec, lam_spec, lam_spec,
                  pl.BlockSpec((None, 1, dv), lambda b, h, i: (h, 0, 0))],
        out_specs=pl.BlockSpec((tq, dv), lambda b, h, i: (b * nq + i, h)),
        out_shape=jax.ShapeDtypeStruct((nb * t, heads * dv), BF16),
        scratch_shapes=[pltpu.VMEM((2, 1, tq), F32),
                        pltpu.VMEM((2, 1, tq), F32),
                        pltpu.VMEM((2, dv, tq), F32),
                        pltpu.VMEM((tk, tq), F32),
                        pltpu.VMEM((tk, tq), F32)],
        compiler_params=_params(("arbitrary", "arbitrary", "arbitrary")),
        name="attn_prompt",
    )(slopes, aq, ak, av_t, *lams, norm_g.reshape(heads, 1, dv))


def _attn_sample_kernel(slope_ref, q_ref, kn_ref, vn_ref, kp_ref, vp_ref, lq1, lk1, lq2, lk2, ng_ref,
                        o_ref, m_ref, l_ref, acc_ref, *, heads, dh, past, lam_init):
    t = q_ref.shape[0]
    dv = 2 * dh
    tc = vp_ref.shape[0] // (2 * heads)
    ci = pl.program_id(1)
    rowp = past + lax.broadcasted_iota(jnp.int32, (t, 1), 0)

    @pl.when(ci == 0)
    def _():
        m_ref[...] = jnp.full(m_ref.shape, -jnp.inf, F32)
        l_ref[...] = jnp.zeros(l_ref.shape, F32)
        acc_ref[...] = jnp.zeros(acc_ref.shape, F32)

    def update(idx, q, k, v, bias):
        s = _dot_nt(q, k) + bias
        m_old = m_ref[idx]
        m_new = jnp.maximum(m_old, jnp.max(s, axis=1, keepdims=True))
        alpha = jnp.exp2(m_old - m_new)
        p = jnp.exp2(s - m_new)
        l_ref[idx] = alpha * l_ref[idx] + jnp.sum(p, axis=1, keepdims=True)
        acc_ref[idx] = alpha * acc_ref[idx] + _dot(p.astype(BF16), v)
        m_ref[idx] = m_new

    colp = ci * tc + lax.broadcasted_iota(jnp.int32, (1, tc), 1)
    dist = (rowp - colp).astype(F32)
    ss = []
    for h in range(heads):
        for c in range(2):
            k = kp_ref[pl.ds(2 * h + c, tc, stride=2 * heads), :].astype(BF16)
            ss.append(_dot_nt(q_ref[:, h * dv + c * dh:h * dv + (c + 1) * dh], k))
    ps, alphas = [], []
    for h in range(heads):
        bias = -(slope_ref[h] * LOG2E) * dist
        for c in range(2):
            idx = 2 * h + c
            s = ss[idx] + bias
            m_old = m_ref[idx]
            m_new = jnp.maximum(m_old, jnp.max(s, axis=1, keepdims=True))
            alpha = jnp.exp2(m_old - m_new)
            p = jnp.exp2(s - m_new)
            l_ref[idx] = alpha * l_ref[idx] + jnp.sum(p, axis=1, keepdims=True)
            m_ref[idx] = m_new
            ps.append(p.astype(BF16))
            alphas.append(alpha)
    for h in range(heads):
        v = jnp.concatenate([vp_ref[pl.ds(j * heads + h, tc, stride=2 * heads), :] for j in range(2)],
                            axis=1).astype(BF16)
        for c in range(2):
            idx = 2 * h + c
            acc_ref[idx] = alphas[idx] * acc_ref[idx] + _dot(ps[idx], v)

    @pl.when(ci == pl.num_programs(1) - 1)
    def _():
        coln = past + lax.broadcasted_iota(jnp.int32, (1, t), 1)
        dist_n = jnp.abs((rowp - coln).astype(F32))
        vis = (coln // ATTN_CHUNK) <= (rowp // ATTN_CHUNK)
        lam = _lambda(lq1, lk1, lq2, lk2, lam_init)
        for h in range(heads):
            slope2 = slope_ref[h] * LOG2E
            bias = jnp.where(vis, -slope2 * dist_n, -jnp.inf)
            v = vn_ref[:, h * dv:(h + 1) * dv].astype(BF16)
            outs = []
            for c in range(2):
                cols = slice(h * dv + c * dh, h * dv + (c + 1) * dh)
                update(2 * h + c, q_ref[:, cols], kn_ref[:, cols].astype(BF16), v, bias)
                outs.append(acc_ref[2 * h + c] / l_ref[2 * h + c])
            o = outs[0] - lam * outs[1]
            o_ref[:, h * dv:(h + 1) * dv] = _head_norm_out(o, ng_ref[h:h + 1, :], lam_init).astype(o_ref.dtype)


def _attn_sample(aq, k_new, v_new, row0, nb, t, k_past, v_past, l, slopes, lams, norm_g, lam_init):
    heads, dv = norm_g.shape
    dh = dv // 2
    past = k_past.shape[2]
    assert row0 % t == 0 and t % 8 == 0
    rb0 = row0 // t
    tc = _pick(past, (512, 256, 128, 64, 32, 16, 8))
    kp = k_past.reshape(k_past.shape[0], nb, past * heads * 2, dh)
    vp = v_past.reshape(v_past.shape[0], nb, past, heads, 2, dh).transpose(0, 1, 2, 4, 3, 5)
    vp = vp.reshape(v_past.shape[0], nb, past * 2 * heads, dh)
    kern = functools.partial(_attn_sample_kernel, heads=heads, dh=dh, past=past, lam_init=lam_init)
    lam_spec = pl.BlockSpec((1, dh), lambda b, c: (0, 0))
    new_spec = pl.BlockSpec((t, heads * dv), lambda b, c: (b, 0))
    return pl.pallas_call(
        kern,
        grid=(nb, past // tc),
        in_specs=[pl.BlockSpec(memory_space=pltpu.SMEM),
                  pl.BlockSpec((t, heads * dv), lambda b, c: (rb0 + b, 0)), new_spec, new_spec,
                  pl.BlockSpec((None, None, tc * heads * 2, dh), lambda b, c: (l, b, c, 0)),
                  pl.BlockSpec((None, None, tc * heads * 2, dh), lambda b, c: (l, b, c, 0)),
                  lam_spec, lam_spec, lam_spec, lam_spec,
                  pl.BlockSpec((heads, dv), lambda b, c: (0, 0))],
        out_specs=pl.BlockSpec((t, heads * dv), lambda b, c: (b, 0)),
        out_shape=jax.ShapeDtypeStruct((nb * t, heads * dv), BF16),
        scratch_shapes=[pltpu.VMEM((2 * heads, t, 1), F32),
                        pltpu.VMEM((2 * heads, t, 1), F32),
                        pltpu.VMEM((2 * heads, t, dv), F32)],
        compiler_params=_params(("arbitrary", "arbitrary")),
        name="attn_sample",
    )(slopes, aq, k_new, v_new, kp, vp, *lams, norm_g)


def _merge_kernel(hmp_ref, hms_ref, hap_ref, has_ref, wa_ref, wb_ref, g0_ref, g1_ref, o_ref, *, nbp):
    i = pl.program_id(0)

    def body(hm_ref, ha_ref):
        ya = _dot(hm_ref[...], wa_ref[...])
        yb = _dot(ha_ref[...], wb_ref[...])
        g0 = jax.nn.sigmoid(g0_ref[...].astype(F32))
        g1 = jax.nn.sigmoid(g1_ref[...].astype(F32))
        o_ref[...] = (g0 * ya + g1 * yb).astype(o_ref.dtype)

    @pl.when(i < nbp)
    def _():
        body(hmp_ref, hap_ref)

    @pl.when(i >= nbp)
    def _():
        body(hms_ref, has_ref)


def _merge(hm_p, hm_s, ha_p, ha_s, w_a, w_b, gates, rows):
    ka, kb = hm_p.shape[1], ha_p.shape[1]
    d = w_a.shape[1]
    tm = rows.tm
    tn = _pick(d, (512, 256, 128))
    nj = d // tn
    pmap = lambda i, j: (rows.p_tile(i), 0)
    smap = lambda i, j: (rows.s_tile(i), 0)
    return pl.pallas_call(
        functools.partial(_merge_kernel, nbp=rows.nbp),
        grid=(rows.nb, nj),
        in_specs=[pl.BlockSpec((tm, ka), pmap), pl.BlockSpec((tm, ka), smap, pipeline_mode=pl.Buffered(1)),
                  pl.BlockSpec((tm, kb), pmap), pl.BlockSpec((tm, kb), smap, pipeline_mode=pl.Buffered(1)),
                  pl.BlockSpec((ka, tn), lambda i, j: (0, j)),
                  pl.BlockSpec((kb, tn), lambda i, j: (0, j)),
                  pl.BlockSpec((tm, tn), lambda i, j: (i, j)),
                  pl.BlockSpec((tm, tn), lambda i, j: (i, nj + j))],
        out_specs=pl.BlockSpec((tm, tn), lambda i, j: (i, j)),
        out_shape=jax.ShapeDtypeStruct((rows.nb * tm, d), BF16),
        compiler_params=_params(("arbitrary", "arbitrary")),
        name="merge",
    )(hm_p, hm_s, ha_p, ha_s, w_a, w_b, gates, gates)


def _outproj_kernel(a_ref, w_ref, xp_ref, xs_ref, gp_ref, gs_ref, o_ref, *, nbp):
    i = pl.program_id(0)
    y = _dot(a_ref[...], w_ref[...])

    def body(x_ref, gt_ref):
        o_ref[...] = x_ref[...] + _mod_apply(y, [gt_ref[...]], lambda y3, gt: gt * y3)

    @pl.when(i < nbp)
    def _():
        body(xp_ref, gp_ref)

    @pl.when(i >= nbp)
    def _():
        body(xs_ref, gs_ref)


def _outproj(a, w, xp, xs, mod, k_gate, rows):
    k = a.shape[1]
    d = w.shape[1]
    tm = rows.tm
    tn = _pick(d, (512, 256, 128))
    gp, gs = rows.mod_specs(k_gate, tn, col=lambda i, j: j)
    return pl.pallas_call(
        functools.partial(_outproj_kernel, nbp=rows.nbp),
        grid=(rows.nb, d // tn),
        in_specs=[pl.BlockSpec((tm, k), lambda i, j: (i, 0)),
                  pl.BlockSpec((k, tn), lambda i, j: (0, j)),
                  pl.BlockSpec((tm, tn), lambda i, j: (rows.p_tile(i), j)),
                  pl.BlockSpec((tm, tn), lambda i, j: (rows.s_tile(i), j)),
                  gp, gs],
        out_specs=pl.BlockSpec((tm, tn), lambda i, j: (i, j)),
        out_shape=jax.ShapeDtypeStruct((rows.nb * tm, d), F32),
        compiler_params=_params(("arbitrary", "arbitrary")),
        name="outproj",
    )(a, w, xp, xs, mod, mod)


def _top_values(s, k):
    vals = []
    for r in range(k):
        m = jnp.max(s, axis=0, keepdims=True)
        vals.append(m)
        if r + 1 < k:
            s = jnp.where(s == m, -jnp.inf, s)
    return vals


def _peer_select_kernel(q_ref, sk_ref, s_ref, aux_ref, *, heads, dsub):
    tm = q_ref.shape[0]
    for h in range(heads):
        tops = []
        for c in range(2):
            qh = q_ref[:, (2 * h + c) * dsub:(2 * h + c + 1) * dsub]
            sk = sk_ref[h, c]
            qh_hi = qh.astype(BF16)
            qh_lo = (qh - qh_hi.astype(F32)).astype(BF16)
            sk_hi = sk.astype(BF16)
            sk_lo = (sk - sk_hi.astype(F32)).astype(BF16)
            s = _dot_nt(sk_hi, qh_hi) + _dot_nt(sk_hi, qh_lo) + _dot_nt(sk_lo, qh_hi)
            s_ref[h, c] = s
            tops.append(_top_values(s, P_TOPK))
        t2 = jnp.concatenate(tops[1], axis=0)
        half = P_TOPK // 2
        cand = jnp.concatenate([tops[0][0] + t2]
                               + [tops[0][a] + t2[0:half] for a in range(1, half)]
                               + [jnp.concatenate(tops[0][half:], axis=0) + tops[1][0]], axis=0)
        best = _top_values(cand, P_TOPK)
        z = jnp.ones((1, tm), F32)
        for r in range(1, P_TOPK):
            z = z + jnp.exp(best[r] - best[0])
        rows = [best[P_TOPK - 1], tops[0][0], tops[1][0], 1.0 / z]
        aux_ref[h] = jnp.concatenate(rows + [jnp.zeros((8 - len(rows), tm), F32)], axis=0)


def _peer_select(qp, sub_keys, l):
    n = qp.shape[0]
    _, heads, _, nkeys, dsub = sub_keys.shape
    tm = _pick(n, (512, 256, 128))
    kern = functools.partial(_peer_select_kernel, heads=heads, dsub=dsub)
    return pl.pallas_call(
        kern,
        grid=(n // tm,),
        in_specs=[pl.BlockSpec((tm, qp.shape[1]), lambda i: (i, 0)),
                  pl.BlockSpec((None,) + sub_keys.shape[1:], lambda i: (l, 0, 0, 0, 0))],
        out_specs=[pl.BlockSpec((heads, 2, nkeys, tm), lambda i: (0, 0, 0, i)),
                   pl.BlockSpec((heads, 8, tm), lambda i: (0, 0, i))],
        out_shape=[jax.ShapeDtypeStruct((heads, 2, nkeys, n), F32),
                   jax.ShapeDtypeStruct((heads, 8, n), F32)],
        compiler_params=_params(("parallel",)),
        name="peer_select",
    )(qp, sub_keys)


def _peer_dense_kernel(x_ref, u_ref, v_ref, s1_ref, s2_ref, aux_ref, o_ref,
                       s1t_ref, e1_ref, e2_ref, p_ref, pblk_ref, *, heads, nkeys):
    te = u_ref.shape[0]
    tm = x_ref.shape[0]
    e = pl.program_id(1)
    nsub = te // nkeys
    grows = te // PEER_SPLIT
    gsub = nsub // PEER_SPLIT

    @pl.when(e == 0)
    def _():
        o_ref[...] = jnp.zeros(o_ref.shape, F32)
        for h in range(heads):
            e2_ref[h] = jnp.exp(s2_ref[h] - aux_ref[h, 2:3, :]) * aux_ref[h, 3:4, :]

    scores = [_dot_nt(u_ref[g * grows:(g + 1) * grows, :], x_ref[...]) for g in range(PEER_SPLIT)]

    grp = e % (8 // nsub)
    for h in range(heads):
        rows = s1_ref[h, 0:nsub, :]
        for gi in range(1, 8 // nsub):
            rows = jnp.where(grp == gi, s1_ref[h, gi * nsub:(gi + 1) * nsub, :], rows)
        s1t_ref[h, 0:nsub, :] = rows
        e1_ref[h, 0:nsub, :] = jnp.exp(rows - aux_ref[h, 1:2, :])

    for g in range(PEER_SPLIT):
        for tb in range(tm // LANES):
            cs = slice(tb * LANES, (tb + 1) * LANES)
            for kb in range(nkeys // PEER_KEY_BLOCK):
                rs = slice(kb * PEER_KEY_BLOCK, (kb + 1) * PEER_KEY_BLOCK)
                w = [None] * gsub
                for h in range(heads):
                    s2_blk = s2_ref[h, rs, cs]
                    e2_blk = e2_ref[h, rs, cs]
                    tau = aux_ref[h, 0:1, cs]
                    for a in range(gsub):
                        ar = g * gsub + a
                        hit = s1t_ref[h, ar:ar + 1, cs] + s2_blk >= tau
                        gate = jnp.where(hit, e1_ref[h, ar:ar + 1, cs] * e2_blk, 0.0)
                        w[a] = gate if w[a] is None else w[a] + gate
                for a in range(gsub):
                    r0 = a * nkeys + kb * PEER_KEY_BLOCK
                    act = jax.nn.gelu(scores[g][r0:r0 + PEER_KEY_BLOCK, cs], approximate=True)
                    pblk_ref[tb, g * gsub + a, rs, :] = w[a] * act
            for a in range(gsub):
                ar = g * gsub + a
                p_ref[cs, ar * nkeys:(ar + 1) * nkeys] = pblk_ref[tb, ar].T.astype(BF16)
        o_ref[...] += _dot(p_ref[:, g * grows:(g + 1) * grows], v_ref[g * grows:(g + 1) * grows, :])


def _peer_dense(x, u_tab, v_tab, s_t, aux):
    n, d = x.shape
    ne = u_tab.shape[0]
    heads, _, nkeys, _ = s_t.shape
    tm = _pick(n, (1024, 512, 256, 128))
    te = _pick(ne, (512, 256, 128))
    assert te % nkeys == 0 and ne == nkeys * nkeys
    n_tiles = ne // te
    nsub = te // nkeys
    assert 8 % nsub == 0 and nkeys % 8 == 0 and nsub % PEER_SPLIT == 0
    kern = functools.partial(_peer_dense_kernel, heads=heads, nkeys=nkeys)
    once = pl.Buffered(1)
    return pl.pallas_call(
        kern,
        grid=(n // tm, n_tiles),
        in_specs=[pl.BlockSpec((tm, d), lambda i, e: (i, 0), pipeline_mode=once),
                  pl.BlockSpec((te, d), lambda i, e: (e, 0)),
                  pl.BlockSpec((te, d), lambda i, e: (e, 0)),
                  pl.BlockSpec((heads, None, 8, tm), lambda i, e: (0, 0, e * nsub // 8, i)),
                  pl.BlockSpec((heads, None, nkeys, tm), lambda i, e: (0, 1, 0, i), pipeline_mode=once),
                  pl.BlockSpec((heads, 8, tm), lambda i, e: (0, 0, i), pipeline_mode=once)],
        out_specs=pl.BlockSpec((tm, d), lambda i, e: (i, 0), pipeline_mode=once),
        out_shape=jax.ShapeDtypeStruct((n, d), F32),
        scratch_shapes=[pltpu.VMEM((heads, 8, tm), F32),
                        pltpu.VMEM((heads, 8, tm), F32),
                        pltpu.VMEM((heads, nkeys, tm), F32),
                        pltpu.VMEM((tm, te), BF16),
                        pltpu.VMEM((tm // LANES, nsub, nkeys, LANES), F32)],
        compiler_params=_params(("parallel", "arbitrary"), 60),
        name="peer_dense",
    )(x, u_tab, v_tab, s_t, s_t, aux)


def _final_kernel(x_ref, p_ref, gp_ref, gs_ref, g_ref, yp_ref, ys_ref, *, nbp):
    i = pl.program_id(0)

    def body(gt_ref, y_ref):
        x = x_ref[...] + _mod_apply(p_ref[...], [gt_ref[...]], lambda p3, gt: gt * p3)
        y_ref[...] = x * lax.rsqrt(jnp.mean(x * x, -1, keepdims=True) + EPS) * g_ref[...]

    @pl.when(i < nbp)
    def _():
        body(gp_ref, yp_ref)

    @pl.when(i >= nbp)
    def _():
        body(gs_ref, ys_ref)


def _final(x, peer, mod, k_gate, g_final, rows):
    d = x.shape[1]
    tm = rows.tm
    gp, gs = rows.mod_specs(k_gate, d)
    return pl.pallas_call(
        functools.partial(_final_kernel, nbp=rows.nbp),
        grid=(rows.nb,),
        in_specs=[pl.BlockSpec((tm, d), lambda i: (i, 0)),
                  pl.BlockSpec((tm, d), lambda i: (i, 0)),
                  gp, gs,
                  pl.BlockSpec((1, d), lambda i: (0, 0))],
        out_specs=[pl.BlockSpec((tm, d), lambda i: (rows.p_tile(i), 0)),
                   pl.BlockSpec((tm, d), lambda i: (rows.s_tile(i), 0))],
        out_shape=[jax.ShapeDtypeStruct((rows.nbp * tm, d), F32),
                   jax.ShapeDtypeStruct((rows.nbs * tm, d), F32)],
        compiler_params=_params(("arbitrary",)),
        name="final_norm",
    )(x, peer, mod, mod, g_final.reshape(1, d))


def kernel(x_prompt, x_sample, cache_k, cache_v, state_C, state_n, state_m, state_conv, c_prompt, c_sample, w_ada, b_ada, g_mix, w_in, conv_w, conv_b, b_gates, m_norm_g, lam_q1, lam_k1, lam_q2, lam_k2, a_norm_g, w_a, w_b, w_out, g_ffn, w_pq, sub_keys, u_tab, v_tab, g_final):
    bp, tp, d = x_prompt.shape
    bs, ts, _ = x_sample.shape
    depth = w_in.shape[0]
    assert depth == 1, "the per-layer state plumbing below is written for a single layer"
    l = 0
    heads, dv = m_norm_g.shape[1:]
    dk = state_C.shape[-1]
    a_heads, a_dv = a_norm_g.shape[1:]
    dh = a_dv // 2
    n_p, n_s = bp * tp, bs * ts
    xp = x_prompt.reshape(n_p, d)
    xs = x_sample.reshape(n_s, d)
    rows_ew = _Rows(bp, tp, bs, ts, _pick(math.gcd(tp, n_s), (256, 128, 64, 32)))
    rows_mm = _Rows(bp, tp, bs, ts, _pick(math.gcd(tp, n_s), (1024, 512, 256, 128, 64, 32)))

    qk_w, v_w = 2 * heads * dk, heads * dv
    a_w = a_heads * a_dv
    col_mv = qk_w
    col_gate = qk_w + 2 * v_w
    col_aq = col_gate + 2 * heads
    col_ak = col_aq + a_w
    col_av = col_ak + a_w
    col_bg = col_av + a_w

    c_all = jnp.concatenate([c_sample, c_prompt], axis=0)
    c_all = jnp.pad(c_all, ((0, (-c_all.shape[0]) % 8), (0, 0)))
    mod = _ada(c_all, w_ada, b_ada, l).reshape(c_all.shape[0], 6, 1, d)

    hn = _norm_mod(xp, xs, 0, rows_ew, g_mix[l], mod, 1, 0)
    w_t = jnp.swapaxes(w_in, 1, 2)
    qk_raw, = _proj(hn, w_t, l, 0, qk_w, "f32", n_p)
    vo, = _proj(hn, w_t, l, col_mv, 2 * v_w, "bf16", n_p)
    aq, = _proj(hn, w_t, l, col_aq, a_w, "bf16", n_p, scale=LOG2E * dh ** -0.5)
    ak, k_p, k_s = _proj(hn, w_t, l, col_ak, a_w, "kv", n_p)
    av_t, v_p, v_s = _proj(hn, w_t, l, col_av, a_w, "kvt", n_p)
    bgates, = _proj(hn, w_t, l, col_bg, 2 * d, "bf16", n_p)
    gates = _gate_proj(hn, w_t, l, col_gate)

    zc = jnp.zeros((bp, heads, dv, dk), F32)
    zn = jnp.zeros((bp, heads, dk), F32)
    zm = jnp.zeros((bp, heads), F32)
    zconv = jnp.zeros((bp, CONV_W - 1, qk_w), F32)
    mp = _mlstm(qk_raw, vo, gates, 0, bp, tp, zconv, conv_w[l], conv_b[l], b_gates[l], m_norm_g[l], zc, zn, zm)
    ms = _mlstm(qk_raw, vo, gates, n_p, bs, ts, state_conv[l], conv_w[l], conv_b[l], b_gates[l], m_norm_g[l],
                state_C[l], state_n[l], state_m[l])

    lam_init = 0.8 - 0.6 * math.exp(-0.3 * l)
    slopes = jnp.array([2.0 ** (-8.0 * (h + 1) / a_heads) for h in range(a_heads)], F32)
    lams = [a[l].reshape(1, -1) for a in (lam_q1, lam_k1, lam_q2, lam_k2)]
    ha_p = _attn_prompt(aq, ak, av_t, bp, tp, slopes, lams, a_norm_g[l], lam_init)
    ha_s = _attn_sample(aq, k_s, v_s, n_p, bs, ts, cache_k, cache_v, l, slopes, lams, a_norm_g[l], lam_init)

    merged = _merge(mp[0], ms[0], ha_p, ha_s, _cast_bf16(w_a, l), _cast_bf16(w_b, l), bgates, rows_mm)
    x1 = _outproj(merged, _cast_bf16(w_out, l), xp, xs, mod, 2, rows_mm)

    hn2 = _norm_mod(x1, x1, rows_ew.nbp, rows_ew, g_ffn[l], mod, 4, 3)
    qp = _matmul(hn2, _cast_bf16(w_pq, l), F32, "peer_query")
    s_t, aux = _peer_select(qp, sub_keys, l)
    peer = _peer_dense(hn2, _cast_bf16(u_tab, l), _cast_bf16(v_tab, l), s_t, aux)
    y_p, y_s = _final(x1, peer, mod, 5, g_final, rows_ew)

    return (y_p.reshape(bp, tp, d), y_s.reshape(bs, ts, d),
            k_p.reshape(1, bp, tp, a_heads, 2, dh), v_p.reshape(1, bp, tp, a_heads, a_dv),
            mp[1][None], mp[2][None], mp[3][None], mp[4][None],
            k_s.reshape(1, bs, ts, a_heads, 2, dh), v_s.reshape(1, bs, ts, a_heads, a_dv),
            ms[1][None], ms[2][None], ms[3][None], ms[4][None])
```

```python
import functools
import math

import jax
import jax.numpy as jnp
from jax import lax
from jax.experimental import pallas as pl
from jax.experimental.pallas import tpu as pltpu

F32 = jnp.float32
BF16 = jnp.bfloat16

EPS = 1e-6
ATTN_CHUNK = 64
ATTN_KEY_BLOCK = 512
ATTN_HEADS_PER_STEP = 2
CONV_W = 4
P_TOPK = 16
PEER_KEY_BLOCK = 32
PEER_SPLIT = 1
LANES = 128
SUBLANES = 8
BF16_ROWS = 16
MLSTM_CHUNK = 256
VMEM_LIMIT_MB = 56
PEER_VMEM_LIMIT_MB = 60
LOG2E = math.log2(math.e)


def _params(sem, vmem_mb=VMEM_LIMIT_MB):
    return pltpu.CompilerParams(dimension_semantics=sem, vmem_limit_bytes=vmem_mb << 20)


def _pick(n, cands):
    for c in cands:
        if n % c == 0:
            return c
    raise ValueError(f"no tile in {cands} divides {n}")


def _dot(a, b):
    return jnp.dot(a, b, preferred_element_type=F32)


def _dot_nt(a, b):
    return lax.dot_general(a, b, (((1,), (1,)), ((), ())), preferred_element_type=F32)


def _dot_tn(a, b):
    return lax.dot_general(a, b, (((0,), (0,)), ((), ())), preferred_element_type=F32)


def _split3(x):
    hi = x.astype(BF16)
    r = x - hi.astype(F32)
    mid = r.astype(BF16)
    lo = (r - mid.astype(F32)).astype(BF16)
    return hi, mid, lo


def _dot_exact_lhs(a01, x):
    hi, mid, lo = _split3(x)
    return _dot(a01, hi) + _dot(a01, mid) + _dot(a01, lo)


def _dot_exact_nt(a01, x):
    hi, mid, lo = _split3(x)
    return _dot_nt(a01, hi) + _dot_nt(a01, mid) + _dot_nt(a01, lo)


class _Rows:
    def __init__(self, bp, tp, bs, ts, tm):
        assert (bp * tp) % tm == 0 and (bs * ts) % tm == 0
        assert tp % tm == 0 and tm % ts == 0, "a prompt tile sits in one sequence, a sample tile holds whole ones"
        self.bp, self.tp, self.bs, self.ts, self.tm = bp, tp, bs, ts, tm
        self.nbp = bp * tp // tm
        self.nbs = bs * ts // tm
        self.nb = self.nbp + self.nbs
        self.ng = tm // ts

    def p_tile(self, i):
        return jnp.minimum(i, self.nbp - 1)

    def s_tile(self, i):
        return jnp.maximum(i - self.nbp, 0)

    def mod_specs(self, k, width, col=None):
        cj = (lambda *g: 0) if col is None else col
        p = pl.BlockSpec((None, None, 1, width),
                         lambda *g: (self.bs + (self.p_tile(g[0]) * self.tm) // self.tp, k, 0, cj(*g)))
        s = pl.BlockSpec((self.ng, None, 1, width), lambda *g: (self.s_tile(g[0]), k, 0, cj(*g)))
        return p, s


def _mod_apply(x, gate_rows, fn):
    tm, w = x.shape
    rows = [r.reshape((-1, 1, w)) for r in gate_rows]
    ng = rows[0].shape[0]
    return fn(x.reshape(ng, tm // ng, w), *rows).reshape(tm, w)


def _cast_kernel(w_ref, o_ref):
    o_ref[...] = w_ref[...].astype(o_ref.dtype)


def _cast_bf16(w3, l):
    _, r, c = w3.shape
    tr = _pick(r, (512, 256, 128))
    tc = _pick(c, (2048, 1024, 512, 256, 128))
    return pl.pallas_call(
        _cast_kernel,
        grid=(r // tr, c // tc),
        in_specs=[pl.BlockSpec((None, tr, tc), lambda i, j: (l, i, j))],
        out_specs=pl.BlockSpec((tr, tc), lambda i, j: (i, j)),
        out_shape=jax.ShapeDtypeStruct((r, c), BF16),
        compiler_params=_params(("parallel", "parallel")),
        name="cast_bf16",
    )(w3)


def _ada_kernel(c_ref, w_ref, b_ref, o_ref):
    c = c_ref[...]
    a = (c * jax.nn.sigmoid(c)).astype(BF16)
    o_ref[...] = _dot(a, w_ref[...].astype(BF16)) + b_ref[...]


def _ada(c_all, w_ada, b_ada, l):
    m, d = c_all.shape
    n = w_ada.shape[2]
    tn = _pick(n, (512, 256, 128))
    return pl.pallas_call(
        _ada_kernel,
        grid=(n // tn,),
        in_specs=[pl.BlockSpec((m, d), lambda j: (0, 0)),
                  pl.BlockSpec((None, d, tn), lambda j: (l, 0, j)),
                  pl.BlockSpec((None, 1, tn), lambda j: (l, 0, j))],
        out_specs=pl.BlockSpec((m, tn), lambda j: (0, j)),
        out_shape=jax.ShapeDtypeStruct((m, n), F32),
        compiler_params=_params(("parallel",)),
        name="ada_mod",
    )(c_all, w_ada, b_ada.reshape(b_ada.shape[0], 1, n))


def _norm_mod_kernel(xp_ref, xs_ref, g_ref, scp_ref, shp_ref, scs_ref, shs_ref, o_ref, *, nbp):
    i = pl.program_id(0)

    def body(x_ref, sc_ref, sh_ref):
        x = x_ref[...]
        y = x * lax.rsqrt(jnp.mean(x * x, -1, keepdims=True) + EPS) * g_ref[...]
        y = _mod_apply(y, [sc_ref[...], sh_ref[...]], lambda y3, sc, sh: y3 * (1.0 + sc) + sh)
        o_ref[...] = y.astype(o_ref.dtype)

    @pl.when(i < nbp)
    def _():
        body(xp_ref, scp_ref, shp_ref)

    @pl.when(i >= nbp)
    def _():
        body(xs_ref, scs_ref, shs_ref)


def _norm_mod(xp, xs, s_off, rows, g, mod, k_scale, k_shift):
    d = xp.shape[1]
    tm = rows.tm
    scp, scs = rows.mod_specs(k_scale, d)
    shp, shs = rows.mod_specs(k_shift, d)
    return pl.pallas_call(
        functools.partial(_norm_mod_kernel, nbp=rows.nbp),
        grid=(rows.nb,),
        in_specs=[pl.BlockSpec((tm, d), lambda i: (rows.p_tile(i), 0)),
                  pl.BlockSpec((tm, d), lambda i: (s_off + rows.s_tile(i), 0)),
                  pl.BlockSpec((1, d), lambda i: (0, 0)),
                  scp, shp, scs, shs],
        out_specs=pl.BlockSpec((tm, d), lambda i: (i, 0)),
        out_shape=jax.ShapeDtypeStruct((rows.nb * tm, d), BF16),
        compiler_params=_params(("arbitrary",)),
        name="norm_mod",
    )(xp, xs, g.reshape(1, d), mod, mod, mod, mod)


def _proj_kernel(x_ref, w_ref, wn_ref, *refs, shift, scale, mode, nip):
    outs, wb_ref = refs[:-1], refs[-1]
    i = pl.program_id(1)

    @pl.when(i == 0)
    def _():
        tn = w_ref.shape[0]
        cr = _pick(tn, (256, 128))
        for r in range(0, tn, cr):
            if r + cr + shift <= tn:
                wb_ref[r:r + cr, :] = w_ref[r + shift:r + cr + shift, :].astype(BF16)
            else:
                wb_ref[r:r + cr - shift, :] = w_ref[r + shift:r + cr, :].astype(BF16)
                wb_ref[r + cr - shift:r + cr, :] = wn_ref[...].astype(BF16)

    acc = _dot_nt(x_ref[...], wb_ref[...])
    if scale != 1.0:
        acc = acc * scale
    if mode in ("kv", "kvt"):
        outs[0][...] = (acc.T if mode == "kvt" else acc).astype(BF16)

        @pl.when(i < nip)
        def _():
            outs[1][...] = acc

        @pl.when(i >= nip)
        def _():
            outs[2][...] = acc
    else:
        outs[0][...] = acc.astype(outs[0].dtype)


def _proj(hn, w_t, l, col0, ncols, mode, n_p, scale=1.0):
    n, k = hn.shape
    tm = _pick(math.gcd(n_p, n - n_p), ((1024,) if mode != "kvt" else ()) + (ATTN_KEY_BLOCK, 256, 128))
    tn = _pick(ncols, ((1024,) if mode == "bf16" else ()) + (512, 256, 128))
    shift = col0 % tn
    assert (col0 - shift) % tn == 0 and shift % BF16_ROWS == 0 and shift < tn
    cb0 = (col0 - shift) // tn
    nip = n_p // tm
    ni, nj = n // tm, ncols // tn
    once = pl.Buffered(1)
    nxt = shift if shift else BF16_ROWS
    in_specs = [pl.BlockSpec((tm, k), lambda j, i: (i, 0)),
                pl.BlockSpec((None, tn, k), lambda j, i: (l, cb0 + j, 0), pipeline_mode=once),
                pl.BlockSpec((None, nxt, k), lambda j, i: (l, (cb0 + j + 1) * (tn // nxt) if shift else 0, 0),
                             pipeline_mode=once)]
    all_spec = pl.BlockSpec((tm, tn), lambda j, i: (i, j))
    if mode in ("kv", "kvt"):
        first = (pl.BlockSpec((None, tn, tm), lambda j, i: (i, j, 0)) if mode == "kvt" else all_spec)
        out_specs = [first,
                     pl.BlockSpec((tm, tn), lambda j, i: (jnp.minimum(i, nip - 1), j)),
                     pl.BlockSpec((tm, tn), lambda j, i: (jnp.maximum(i - nip, 0), j))]
        out_shape = [jax.ShapeDtypeStruct((ni, ncols, tm) if mode == "kvt" else (n, ncols), BF16),
                     jax.ShapeDtypeStruct((n_p, ncols), F32),
                     jax.ShapeDtypeStruct((n - n_p, ncols), F32)]
    else:
        out_specs = [all_spec]
        out_shape = [jax.ShapeDtypeStruct((n, ncols), F32 if mode == "f32" else BF16)]
    kern = functools.partial(_proj_kernel, shift=shift, scale=scale, mode=mode, nip=nip)
    return pl.pallas_call(
        kern,
        grid=(nj, ni),
        in_specs=in_specs,
        out_specs=out_specs,
        out_shape=out_shape,
        scratch_shapes=[pltpu.VMEM((tn, k), BF16)],
        compiler_params=_params(("arbitrary", "arbitrary")),
        name="proj_" + mode,
    )(hn, w_t, w_t)


def _gate_kernel(x_ref, w_ref, o_ref):
    o_ref[...] = _dot_nt(x_ref[...], w_ref[...].astype(BF16))


def _gate_proj(hn, w_t, l, col0):
    n, k = hn.shape
    assert col0 % LANES == 0
    tm = _pick(n, (1024, 512, 256, 128))
    return pl.pallas_call(
        _gate_kernel,
        grid=(n // tm,),
        in_specs=[pl.BlockSpec((tm, k), lambda i: (i, 0)),
                  pl.BlockSpec((None, LANES, k), lambda i: (l, col0 // LANES, 0))],
        out_specs=pl.BlockSpec((tm, LANES), lambda i: (i, 0)),
        out_shape=jax.ShapeDtypeStruct((n, LANES), F32),
        compiler_params=_params(("parallel",)),
        name="gate_proj",
    )(hn, w_t)


def _mm_kernel(a_ref, b_ref, o_ref):
    o_ref[...] = _dot(a_ref[...], b_ref[...]).astype(o_ref.dtype)


def _matmul(a, b, out_dtype, name):
    m, k = a.shape
    n = b.shape[1]
    tm = _pick(m, (1024, 512, 256, 128))
    tn = _pick(n, (1024, 512, 256, 128))
    return pl.pallas_call(
        _mm_kernel,
        grid=(m // tm, n // tn),
        in_specs=[pl.BlockSpec((tm, k), lambda i, j: (i, 0)),
                  pl.BlockSpec((k, tn), lambda i, j: (0, j))],
        out_specs=pl.BlockSpec((tm, tn), lambda i, j: (i, j)),
        out_shape=jax.ShapeDtypeStruct((m, n), out_dtype),
        compiler_params=_params(("parallel", "parallel")),
        name=name,
    )(a, b)


def _mlstm_kernel(qk_ref, v_ref, og_ref, gt_ref, conv0_ref, cw_ref, cb_ref, bg_ref, ng_ref,
                  c0_ref, n0_ref, m0_ref,
                  hm_ref, c_ref, n_ref, m_ref, convn_ref, full_ref, *, heads, dk, dv):
    L = qk_ref.shape[0]
    pad = SUBLANES
    hist = CONV_W - 1
    ci = pl.program_id(1)

    @pl.when(ci == 0)
    def _():
        c_ref[...] = c0_ref[...]
        n_ref[...] = n0_ref[...]
        m_ref[...] = m0_ref[...]
        full_ref[pad - hist:pad, :] = conv0_ref[0]

    u = qk_ref[...]
    full_ref[pad:pad + L, :] = u
    acc = cb_ref[...] + u * cw_ref[hist:hist + 1, :]
    for j in range(hist):
        acc = acc + full_ref[pad - hist + j:pad - hist + j + L, :] * cw_ref[j:j + 1, :]
    tail = full_ref[pad + L - hist:pad + L, :]
    convn_ref[0] = tail
    full_ref[pad - hist:pad, :] = tail
    qk = acc * jax.nn.sigmoid(acc)

    row = lax.broadcasted_iota(jnp.int32, (L, L), 0)
    col = lax.broadcasted_iota(jnp.int32, (L, L), 1)
    causal = col <= row
    tril01 = jnp.where(causal, 1.0, 0.0).astype(BF16)
    glanes = gt_ref.shape[1]
    pre = gt_ref[...] + bg_ref[...]
    logf_all = jnp.minimum(pre, 0.0) - jnp.log1p(jnp.exp(-jnp.abs(pre)))
    b_all = _dot_exact_lhs(tril01, logf_all)
    eye01 = jnp.where(lax.broadcasted_iota(jnp.int32, (glanes, glanes), 0)
                      == lax.broadcasted_iota(jnp.int32, (glanes, glanes), 1), 1.0, 0.0).astype(BF16)
    pre_t = _dot_exact_nt(eye01, pre)
    b_t = _dot_exact_nt(eye01, b_all)

    for h in range(heads):
        q = qk[:, h * dk:(h + 1) * dk]
        k = qk[:, (heads + h) * dk:(heads + h + 1) * dk] * (dk ** -0.5)
        vb = v_ref[:, h * dv:(h + 1) * dv]
        v = vb.astype(F32)
        qb, kb = q.astype(BF16), k.astype(BF16)
        i_col = pre[:, h:h + 1]
        b_col = b_all[:, heads + h:heads + h + 1]
        dmat = jnp.where(causal, b_col + (pre_t[h:h + 1, :] - b_t[heads + h:heads + h + 1, :]), -jnp.inf)
        m_prev = m_ref[0, h:h + 1, 0:1]
        inter = b_col + m_prev
        m_t = jnp.maximum(inter, jnp.max(dmat, axis=1, keepdims=True))
        s = _dot_nt(qb, kb) * jnp.exp(dmat - m_t)
        w_prev = jnp.exp(inter - m_t)
        c_old = c_ref[0, h]
        n_old = n_ref[0, h:h + 1, :]
        num = _dot(s.astype(BF16), vb) + w_prev * _dot_nt(qb, c_old.astype(BF16))
        den = jnp.sum(s, axis=1, keepdims=True) + w_prev * jnp.sum(q * n_old, axis=1, keepdims=True)
        hh = num / jnp.maximum(jnp.abs(den), jnp.exp(-m_t))
        m_new = m_t[L - 1:L, :]
        b_last = b_col[L - 1:L, :]
        decay = jnp.exp(b_last + m_prev - m_new)
        w_src = jnp.exp(b_last - b_col + i_col - m_new)
        c_ref[0, h] = decay * c_old + _dot_tn((w_src * v).astype(BF16), kb)
        n_ref[0, h:h + 1, :] = decay * n_old + jnp.sum(w_src * k, axis=0, keepdims=True)
        m_ref[0, h:h + 1, :] = jnp.broadcast_to(m_new, (1, m_ref.shape[2]))
        hn = hh * lax.rsqrt(jnp.mean(hh * hh, -1, keepdims=True) + EPS) * ng_ref[h:h + 1, :]
        hn = hn * jax.nn.sigmoid(og_ref[:, h * dv:(h + 1) * dv].astype(F32))
        hm_ref[:, h * dv:(h + 1) * dv] = hn.astype(hm_ref.dtype)


def _mlstm(qk_raw, vo, gates, row0, nb, t, conv0, conv_w, conv_b, b_gates, norm_g, c0, n0, m0):
    heads, dv = norm_g.shape
    dk = c0.shape[-1]
    qkw = 2 * heads * dk
    vw = heads * dv
    L = _pick(t, (MLSTM_CHUNK, 128, 64, 32, 16, 8))
    nc = t // L
    assert row0 % L == 0 and L >= CONV_W - 1
    rb0 = row0 // L
    m0b = jnp.broadcast_to(m0[:, :, None], (nb, heads, LANES))
    glanes = gates.shape[1]
    assert 2 * heads <= glanes
    gate_bias = jnp.pad(b_gates.reshape(1, 2 * heads), ((0, 0), (0, glanes - 2 * heads)))
    rmap = lambda b, c: rb0 + b * nc + c
    kern = functools.partial(_mlstm_kernel, heads=heads, dk=dk, dv=dv)
    hm, c1, n1, m1, convn = pl.pallas_call(
        kern,
        grid=(nb, nc),
        in_specs=[pl.BlockSpec((L, qkw), lambda b, c: (rmap(b, c), 0)),
                  pl.BlockSpec((L, vw), lambda b, c: (rmap(b, c), 0)),
                  pl.BlockSpec((L, vw), lambda b, c: (rmap(b, c), 1)),
                  pl.BlockSpec((L, gates.shape[1]), lambda b, c: (rmap(b, c), 0)),
                  pl.BlockSpec((1, CONV_W - 1, qkw), lambda b, c: (b, 0, 0)),
                  pl.BlockSpec((CONV_W, qkw), lambda b, c: (0, 0)),
                  pl.BlockSpec((1, qkw), lambda b, c: (0, 0)),
                  pl.BlockSpec((1, glanes), lambda b, c: (0, 0)),
                  pl.BlockSpec((heads, dv), lambda b, c: (0, 0)),
                  pl.BlockSpec((1, heads, dv, dk), lambda b, c: (b, 0, 0, 0)),
                  pl.BlockSpec((1, heads, dk), lambda b, c: (b, 0, 0)),
                  pl.BlockSpec((1, heads, LANES), lambda b, c: (b, 0, 0))],
        out_specs=[pl.BlockSpec((L, vw), lambda b, c: (b * nc + c, 0)),
                   pl.BlockSpec((1, heads, dv, dk), lambda b, c: (b, 0, 0, 0)),
                   pl.BlockSpec((1, heads, dk), lambda b, c: (b, 0, 0)),
                   pl.BlockSpec((1, heads, LANES), lambda b, c: (b, 0, 0)),
                   pl.BlockSpec((1, CONV_W - 1, qkw), lambda b, c: (b, 0, 0))],
        out_shape=[jax.ShapeDtypeStruct((nb * t, vw), BF16),
                   jax.ShapeDtypeStruct((nb, heads, dv, dk), F32),
                   jax.ShapeDtypeStruct((nb, heads, dk), F32),
                   jax.ShapeDtypeStruct((nb, heads, LANES), F32),
                   jax.ShapeDtypeStruct((nb, CONV_W - 1, qkw), F32)],
        scratch_shapes=[pltpu.VMEM((SUBLANES + L, qkw), F32)],
        compiler_params=_params(("parallel", "arbitrary")),
        name="mlstm",
    )(qk_raw, vo, vo, gates, conv0, conv_w, conv_b.reshape(1, qkw), gate_bias, norm_g, c0, n0, m0b)
    return hm, c1, n1, m1[:, :, 0], convn


def _lambda(lq1, lk1, lq2, lk2, lam_init):
    return (jnp.exp(jnp.sum(lq1[...] * lk1[...], axis=1, keepdims=True))
            - jnp.exp(jnp.sum(lq2[...] * lk2[...], axis=1, keepdims=True)) + lam_init)


def _head_norm_out(o, g_row, lam_init):
    return o * lax.rsqrt(jnp.mean(o * o, -1, keepdims=True) + EPS) * g_row * (1.0 - lam_init)


def _attn_prompt_kernel(slope_ref, q_ref, k_ref, vt_ref, lq1, lk1, lq2, lk2, ng_ref, o_ref,
                        m_ref, l_ref, acc_ref, bias_ref, dbias_ref, *, dh, lam_init):
    tq = q_ref.shape[0]
    tk = vt_ref.shape[2]
    dv = 2 * dh
    hp = pl.program_id(1)
    qi = pl.program_id(2)
    nh = ATTN_HEADS_PER_STEP
    slopes2 = [slope_ref[hp * nh + hh] * LOG2E for hh in range(nh)]
    chains = [(hh, c) for hh in range(nh) for c in range(2)]

    @pl.when(qi == 0)
    def _():
        krow = lax.broadcasted_iota(jnp.int32, (tk, tq), 0)
        qcol = lax.broadcasted_iota(jnp.int32, (tk, tq), 1)
        kf = krow.astype(F32)
        ahead = jnp.maximum(krow - qcol, 0).astype(F32)
        vis = (krow // ATTN_CHUNK) <= (qcol // ATTN_CHUNK)
        for hh in range(nh):
            bias_ref[hh] = slopes2[hh] * kf
            dbias_ref[hh] = jnp.where(vis, slopes2[hh] * kf - (2.0 * slopes2[hh]) * ahead, -jnp.inf)

    m_ref[...] = jnp.full(m_ref.shape, -jnp.inf, F32)
    l_ref[...] = jnp.zeros(l_ref.shape, F32)
    acc_ref[...] = jnp.zeros(acc_ref.shape, F32)

    def update(kj, b_ref):
        kb = k_ref[pl.ds(pl.multiple_of(kj * tk, tk), tk), :]
        vt = vt_ref[kj]
        rel = (kj * tk - qi * tq).astype(F32)
        ss = [_dot_nt(kb[:, hh * dv + c * dh:hh * dv + (c + 1) * dh],
                      q_ref[:, hh * dv + c * dh:hh * dv + (c + 1) * dh]) for hh, c in chains]
        ps, alphas = [], []
        for ci, (hh, c) in enumerate(chains):
            shift = slopes2[hh] * rel
            s = ss[ci] + b_ref[hh]
            m_old = m_ref[ci]
            m_new = jnp.maximum(m_old, jnp.max(s, axis=0, keepdims=True) + shift)
            alpha = jnp.exp2(m_old - m_new)
            p = jnp.exp2(s - (m_new - shift))
            l_ref[ci] = alpha * l_ref[ci] + jnp.sum(p, axis=0, keepdims=True)
            m_ref[ci] = m_new
            ps.append(p.astype(BF16))
            alphas.append(alpha)
        for ci, (hh, c) in enumerate(chains):
            acc_ref[ci] = alphas[ci] * acc_ref[ci] + _dot(vt[hh * dv:(hh + 1) * dv, :], ps[ci])

    def body(kj, carry):
        update(kj, bias_ref)
        return carry

    lax.fori_loop(0, qi, body, 0)
    update(qi, dbias_ref)
    lam = _lambda(lq1, lk1, lq2, lk2, lam_init)
    for hh in range(nh):
        o = acc_ref[2 * hh] / l_ref[2 * hh] - lam * (acc_ref[2 * hh + 1] / l_ref[2 * hh + 1])
        o = o * lax.rsqrt(jnp.mean(o * o, axis=0, keepdims=True) + EPS)
        o_ref[:, hh * dv:(hh + 1) * dv] = (o.T * ng_ref[hh:hh + 1, :] * (1.0 - lam_init)).astype(o_ref.dtype)


def _attn_prompt(aq, ak, av_t, nb, t, slopes, lams, norm_g, lam_init):
    heads, dv = norm_g.shape
    dh = dv // 2
    tk = av_t.shape[2]
    tq = tk
    nq = t // tq
    nh = ATTN_HEADS_PER_STEP
    assert t % tq == 0 and tq % ATTN_CHUNK == 0 and tq % LANES == 0 and heads % nh == 0
    kern = functools.partial(_attn_prompt_kernel, dh=dh, lam_init=lam_init)
    lam_spec = pl.BlockSpec((1, dh), lambda b, h, i: (0, 0))
    return pl.pallas_call(
        kern,
        grid=(nb, heads // nh, nq),
        in_specs=[pl.BlockSpec(memory_space=pltpu.SMEM),
                  pl.BlockSpec((tq, nh * dv), lambda b, h, i: (b * nq + i, h)),
                  pl.BlockSpec((t, nh * dv), lambda b, h, i: (b, h)),
                  pl.BlockSpec((nq, nh * dv, tk), lambda b, h, i: (b, h, 0)),
                  lam_spec, lam_spec, lam_spec, lam_spec,
                  pl.BlockSpec((None, nh, dv), lambda b, h, i: (h, 0, 0))],
        out_specs=pl.BlockSpec((tq, nh * dv), lambda b, h, i: (b * nq + i, h)),
        out_shape=jax.ShapeDtypeStruct((nb * t, heads * dv), BF16),
        scratch_shapes=[pltpu.VMEM((2 * nh, 1, tq), F32),
                        pltpu.VMEM((2 * nh, 1, tq), F32),
                        pltpu.VMEM((2 * nh, dv, tq), F32),
                        pltpu.VMEM((nh, tk, tq), F32),
                        pltpu.VMEM((nh, tk, tq), F32)],
        compiler_params=_params(("arbitrary", "arbitrary", "arbitrary")),
        name="attn_prompt",
    )(slopes, aq, ak, av_t, *lams, norm_g.reshape(heads // nh, nh, dv))


def _attn_sample_kernel(slope_ref, q_ref, kn_ref, vn_ref, kp_ref, vp_ref, lq1, lk1, lq2, lk2, ng_ref,
                        o_ref, m_ref, l_ref, acc_ref, *, heads, dh, past, lam_init):
    t = q_ref.shape[0]
    dv = 2 * dh
    tc = vp_ref.shape[0] // (2 * heads)
    ci = pl.program_id(1)
    rowp = past + lax.broadcasted_iota(jnp.int32, (t, 1), 0)

    @pl.when(ci == 0)
    def _():
        m_ref[...] = jnp.full(m_ref.shape, -jnp.inf, F32)
        l_ref[...] = jnp.zeros(l_ref.shape, F32)
        acc_ref[...] = jnp.zeros(acc_ref.shape, F32)

    def update(idx, q, k, v, bias):
        s = _dot_nt(q, k) + bias
        m_old = m_ref[idx]
        m_new = jnp.maximum(m_old, jnp.max(s, axis=1, keepdims=True))
        alpha = jnp.exp2(m_old - m_new)
        p = jnp.exp2(s - m_new)
        l_ref[idx] = alpha * l_ref[idx] + jnp.sum(p, axis=1, keepdims=True)
        acc_ref[idx] = alpha * acc_ref[idx] + _dot(p.astype(BF16), v)
        m_ref[idx] = m_new

    colp = ci * tc + lax.broadcasted_iota(jnp.int32, (1, tc), 1)
    dist = (rowp - colp).astype(F32)
    ss = []
    for h in range(heads):
        for c in range(2):
            k = kp_ref[pl.ds(2 * h + c, tc, stride=2 * heads), :].astype(BF16)
            ss.append(_dot_nt(q_ref[:, h * dv + c * dh:h * dv + (c + 1) * dh], k))
    ps, alphas = [], []
    for h in range(heads):
        bias = -(slope_ref[h] * LOG2E) * dist
        for c in range(2):
            idx = 2 * h + c
            s = ss[idx] + bias
            m_old = m_ref[idx]
            m_new = jnp.maximum(m_old, jnp.max(s, axis=1, keepdims=True))
            alpha = jnp.exp2(m_old - m_new)
            p = jnp.exp2(s - m_new)
            l_ref[idx] = alpha * l_ref[idx] + jnp.sum(p, axis=1, keepdims=True)
            m_ref[idx] = m_new
            ps.append(p.astype(BF16))
            alphas.append(alpha)
    for h in range(heads):
        v = jnp.concatenate([vp_ref[pl.ds(j * heads + h, tc, stride=2 * heads), :] for j in range(2)],
                            axis=1).astype(BF16)
        for c in range(2):
            idx = 2 * h + c
            acc_ref[idx] = alphas[idx] * acc_ref[idx] + _dot(ps[idx], v)

    @pl.when(ci == pl.num_programs(1) - 1)
    def _():
        coln = past + lax.broadcasted_iota(jnp.int32, (1, t), 1)
        dist_n = jnp.abs((rowp - coln).astype(F32))
        vis = (coln // ATTN_CHUNK) <= (rowp // ATTN_CHUNK)
        lam = _lambda(lq1, lk1, lq2, lk2, lam_init)
        for h in range(heads):
            slope2 = slope_ref[h] * LOG2E
            bias = jnp.where(vis, -slope2 * dist_n, -jnp.inf)
            v = vn_ref[:, h * dv:(h + 1) * dv].astype(BF16)
            outs = []
            for c in range(2):
                cols = slice(h * dv + c * dh, h * dv + (c + 1) * dh)
                update(2 * h + c, q_ref[:, cols], kn_ref[:, cols].astype(BF16), v, bias)
                outs.append(acc_ref[2 * h + c] / l_ref[2 * h + c])
            o = outs[0] - lam * outs[1]
            o_ref[:, h * dv:(h + 1) * dv] = _head_norm_out(o, ng_ref[h:h + 1, :], lam_init).astype(o_ref.dtype)


def _attn_sample(aq, k_new, v_new, row0, nb, t, k_past, v_past, l, slopes, lams, norm_g, lam_init):
    heads, dv = norm_g.shape
    dh = dv // 2
    past = k_past.shape[2]
    assert row0 % t == 0 and t % 8 == 0
    rb0 = row0 // t
    tc = _pick(past, (512, 256, 128, 64, 32, 16, 8))
    kp = k_past.reshape(k_past.shape[0], nb, past * heads * 2, dh)
    vp = v_past.reshape(v_past.shape[0], nb, past, heads, 2, dh).transpose(0, 1, 2, 4, 3, 5)
    vp = vp.reshape(v_past.shape[0], nb, past * 2 * heads, dh)
    kern = functools.partial(_attn_sample_kernel, heads=heads, dh=dh, past=past, lam_init=lam_init)
    lam_spec = pl.BlockSpec((1, dh), lambda b, c: (0, 0))
    new_spec = pl.BlockSpec((t, heads * dv), lambda b, c: (b, 0))
    return pl.pallas_call(
        kern,
        grid=(nb, past // tc),
        in_specs=[pl.BlockSpec(memory_space=pltpu.SMEM),
                  pl.BlockSpec((t, heads * dv), lambda b, c: (rb0 + b, 0)), new_spec, new_spec,
                  pl.BlockSpec((None, None, tc * heads * 2, dh), lambda b, c: (l, b, c, 0)),
                  pl.BlockSpec((None, None, tc * heads * 2, dh), lambda b, c: (l, b, c, 0)),
                  lam_spec, lam_spec, lam_spec, lam_spec,
                  pl.BlockSpec((heads, dv), lambda b, c: (0, 0))],
        out_specs=pl.BlockSpec((t, heads * dv), lambda b, c: (b, 0)),
        out_shape=jax.ShapeDtypeStruct((nb * t, heads * dv), BF16),
        scratch_shapes=[pltpu.VMEM((2 * heads, t, 1), F32),
                        pltpu.VMEM((2 * heads, t, 1), F32),
                        pltpu.VMEM((2 * heads, t, dv), F32)],
        compiler_params=_params(("arbitrary", "arbitrary")),
        name="attn_sample",
    )(slopes, aq, k_new, v_new, kp, vp, *lams, norm_g)


def _merge_kernel(hmp_ref, hms_ref, hap_ref, has_ref, wa_ref, wb_ref, g0_ref, g1_ref, o_ref, *, nbp):
    i = pl.program_id(0)

    def body(hm_ref, ha_ref):
        ya = _dot(hm_ref[...], wa_ref[...])
        yb = _dot(ha_ref[...], wb_ref[...])
        g0 = jax.nn.sigmoid(g0_ref[...].astype(F32))
        g1 = jax.nn.sigmoid(g1_ref[...].astype(F32))
        o_ref[...] = (g0 * ya + g1 * yb).astype(o_ref.dtype)

    @pl.when(i < nbp)
    def _():
        body(hmp_ref, hap_ref)

    @pl.when(i >= nbp)
    def _():
        body(hms_ref, has_ref)


def _merge(hm_p, hm_s, ha_p, ha_s, w_a, w_b, gates, rows):
    ka, kb = hm_p.shape[1], ha_p.shape[1]
    d = w_a.shape[1]
    tm = rows.tm
    tn = _pick(d, (512, 256, 128))
    nj = d // tn
    pmap = lambda i, j: (rows.p_tile(i), 0)
    smap = lambda i, j: (rows.s_tile(i), 0)
    return pl.pallas_call(
        functools.partial(_merge_kernel, nbp=rows.nbp),
        grid=(rows.nb, nj),
        in_specs=[pl.BlockSpec((tm, ka), pmap), pl.BlockSpec((tm, ka), smap, pipeline_mode=pl.Buffered(1)),
                  pl.BlockSpec((tm, kb), pmap), pl.BlockSpec((tm, kb), smap, pipeline_mode=pl.Buffered(1)),
                  pl.BlockSpec((ka, tn), lambda i, j: (0, j)),
                  pl.BlockSpec((kb, tn), lambda i, j: (0, j)),
                  pl.BlockSpec((tm, tn), lambda i, j: (i, j)),
                  pl.BlockSpec((tm, tn), lambda i, j: (i, nj + j))],
        out_specs=pl.BlockSpec((tm, tn), lambda i, j: (i, j)),
        out_shape=jax.ShapeDtypeStruct((rows.nb * tm, d), BF16),
        compiler_params=_params(("arbitrary", "arbitrary")),
        name="merge",
    )(hm_p, hm_s, ha_p, ha_s, w_a, w_b, gates, gates)


def _outproj_kernel(a_ref, w_ref, xp_ref, xs_ref, gp_ref, gs_ref, o_ref, *, nbp):
    i = pl.program_id(0)
    y = _dot(a_ref[...], w_ref[...])

    def body(x_ref, gt_ref):
        o_ref[...] = x_ref[...] + _mod_apply(y, [gt_ref[...]], lambda y3, gt: gt * y3)

    @pl.when(i < nbp)
    def _():
        body(xp_ref, gp_ref)

    @pl.when(i >= nbp)
    def _():
        body(xs_ref, gs_ref)


def _outproj(a, w, xp, xs, mod, k_gate, rows):
    k = a.shape[1]
    d = w.shape[1]
    tm = rows.tm
    tn = _pick(d, (512, 256, 128))
    gp, gs = rows.mod_specs(k_gate, tn, col=lambda i, j: j)
    return pl.pallas_call(
        functools.partial(_outproj_kernel, nbp=rows.nbp),
        grid=(rows.nb, d // tn),
        in_specs=[pl.BlockSpec((tm, k), lambda i, j: (i, 0)),
                  pl.BlockSpec((k, tn), lambda i, j: (0, j)),
                  pl.BlockSpec((tm, tn), lambda i, j: (rows.p_tile(i), j)),
                  pl.BlockSpec((tm, tn), lambda i, j: (rows.s_tile(i), j)),
                  gp, gs],
        out_specs=pl.BlockSpec((tm, tn), lambda i, j: (i, j)),
        out_shape=jax.ShapeDtypeStruct((rows.nb * tm, d), F32),
        compiler_params=_params(("arbitrary", "arbitrary")),
        name="outproj",
    )(a, w, xp, xs, mod, mod)


def _top_values(s, k):
    vals = []
    for r in range(k):
        m = jnp.max(s, axis=0, keepdims=True)
        vals.append(m)
        if r + 1 < k:
            s = jnp.where(s == m, -jnp.inf, s)
    return vals


def _peer_select_kernel(q_ref, sk_ref, s_ref, aux_ref, *, heads, dsub):
    tm = q_ref.shape[0]
    for h in range(heads):
        tops = []
        for c in range(2):
            qh = q_ref[:, (2 * h + c) * dsub:(2 * h + c + 1) * dsub]
            sk = sk_ref[h, c]
            qh_hi = qh.astype(BF16)
            qh_lo = (qh - qh_hi.astype(F32)).astype(BF16)
            sk_hi = sk.astype(BF16)
            sk_lo = (sk - sk_hi.astype(F32)).astype(BF16)
            s = _dot_nt(sk_hi, qh_hi) + _dot_nt(sk_hi, qh_lo) + _dot_nt(sk_lo, qh_hi)
            s_ref[h, c] = s
            tops.append(_top_values(s, P_TOPK))
        t2 = jnp.concatenate(tops[1], axis=0)
        half = P_TOPK // 2
        cand = jnp.concatenate([tops[0][0] + t2]
                               + [tops[0][a] + t2[0:half] for a in range(1, half)]
                               + [jnp.concatenate(tops[0][half:], axis=0) + tops[1][0]], axis=0)
        best = _top_values(cand, P_TOPK)
        z = jnp.ones((1, tm), F32)
        for r in range(1, P_TOPK):
            z = z + jnp.exp(best[r] - best[0])
        rows = [best[P_TOPK - 1], tops[0][0], tops[1][0], 1.0 / z]
        aux_ref[h] = jnp.concatenate(rows + [jnp.zeros((SUBLANES - len(rows), tm), F32)], axis=0)


def _peer_select(qp, sub_keys, l):
    n = qp.shape[0]
    _, heads, _, nkeys, dsub = sub_keys.shape
    tm = _pick(n, (512, 256, 128))
    kern = functools.partial(_peer_select_kernel, heads=heads, dsub=dsub)
    return pl.pallas_call(
        kern,
        grid=(n // tm,),
        in_specs=[pl.BlockSpec((tm, qp.shape[1]), lambda i: (i, 0)),
                  pl.BlockSpec((None,) + sub_keys.shape[1:], lambda i: (l, 0, 0, 0, 0))],
        out_specs=[pl.BlockSpec((heads, 2, nkeys, tm), lambda i: (0, 0, 0, i)),
                   pl.BlockSpec((heads, SUBLANES, tm), lambda i: (0, 0, i))],
        out_shape=[jax.ShapeDtypeStruct((heads, 2, nkeys, n), F32),
                   jax.ShapeDtypeStruct((heads, SUBLANES, n), F32)],
        compiler_params=_params(("parallel",)),
        name="peer_select",
    )(qp, sub_keys)


def _peer_dense_kernel(x_ref, u_ref, v_ref, s1_ref, s2_ref, aux_ref, o_ref,
                       s1t_ref, e1_ref, e2_ref, p_ref, pblk_ref, *, heads, nkeys):
    te = u_ref.shape[0]
    tm = x_ref.shape[0]
    e = pl.program_id(1)
    nsub = te // nkeys
    grows = te // PEER_SPLIT
    gsub = nsub // PEER_SPLIT

    @pl.when(e == 0)
    def _():
        o_ref[...] = jnp.zeros(o_ref.shape, F32)
        for h in range(heads):
            e2_ref[h] = jnp.exp(s2_ref[h] - aux_ref[h, 2:3, :]) * aux_ref[h, 3:4, :]

    scores = _dot_nt(u_ref[...], x_ref[...])

    grp = e % (SUBLANES // nsub)
    for h in range(heads):
        rows = s1_ref[h, 0:nsub, :]
        for gi in range(1, SUBLANES // nsub):
            rows = jnp.where(grp == gi, s1_ref[h, gi * nsub:(gi + 1) * nsub, :], rows)
        s1t_ref[h, 0:nsub, :] = rows
        e1_ref[h, 0:nsub, :] = jnp.exp(rows - aux_ref[h, 1:2, :])

    for g in range(PEER_SPLIT):
        for tb in range(tm // LANES):
            cs = slice(tb * LANES, (tb + 1) * LANES)
            for kb in range(nkeys // PEER_KEY_BLOCK):
                rs = slice(kb * PEER_KEY_BLOCK, (kb + 1) * PEER_KEY_BLOCK)
                w = [None] * gsub
                for h in range(heads):
                    s2_blk = s2_ref[h, rs, cs]
                    e2_blk = e2_ref[h, rs, cs]
                    tau = aux_ref[h, 0:1, cs]
                    for a in range(gsub):
                        ar = g * gsub + a
                        hit = s1t_ref[h, ar:ar + 1, cs] + s2_blk >= tau
                        gate = jnp.where(hit, e1_ref[h, ar:ar + 1, cs] * e2_blk, 0.0)
                        w[a] = gate if w[a] is None else w[a] + gate
                for a in range(gsub):
                    r0 = a * nkeys + kb * PEER_KEY_BLOCK
                    act = jax.nn.gelu(scores[g * grows + r0:g * grows + r0 + PEER_KEY_BLOCK, cs], approximate=True)
                    pblk_ref[tb, g * gsub + a, rs, :] = w[a] * act
            for a in range(gsub):
                ar = g * gsub + a
                p_ref[cs, ar * nkeys:(ar + 1) * nkeys] = pblk_ref[tb, ar].T.astype(BF16)
        o_ref[...] += _dot(p_ref[:, g * grows:(g + 1) * grows], v_ref[g * grows:(g + 1) * grows, :])


def _peer_dense(x, u_tab, v_tab, s_t, aux):
    n, d = x.shape
    ne = u_tab.shape[0]
    heads, _, nkeys, _ = s_t.shape
    tm = _pick(n, (1024, 512, 256, 128))
    te = _pick(ne, (512, 256, 128))
    assert te % nkeys == 0 and ne == nkeys * nkeys
    n_tiles = ne // te
    nsub = te // nkeys
    assert SUBLANES % nsub == 0 and nkeys % SUBLANES == 0 and nsub % PEER_SPLIT == 0
    kern = functools.partial(_peer_dense_kernel, heads=heads, nkeys=nkeys)
    once = pl.Buffered(1)
    return pl.pallas_call(
        kern,
        grid=(n // tm, n_tiles),
        in_specs=[pl.BlockSpec((tm, d), lambda i, e: (i, 0), pipeline_mode=once),
                  pl.BlockSpec((te, d), lambda i, e: (e, 0)),
                  pl.BlockSpec((te, d), lambda i, e: (e, 0)),
                  pl.BlockSpec((heads, None, SUBLANES, tm), lambda i, e: (0, 0, e * nsub // SUBLANES, i)),
                  pl.BlockSpec((heads, None, nkeys, tm), lambda i, e: (0, 1, 0, i), pipeline_mode=once),
                  pl.BlockSpec((heads, SUBLANES, tm), lambda i, e: (0, 0, i), pipeline_mode=once)],
        out_specs=pl.BlockSpec((tm, d), lambda i, e: (i, 0), pipeline_mode=once),
        out_shape=jax.ShapeDtypeStruct((n, d), F32),
        scratch_shapes=[pltpu.VMEM((heads, SUBLANES, tm), F32),
                        pltpu.VMEM((heads, SUBLANES, tm), F32),
                        pltpu.VMEM((heads, nkeys, tm), F32),
                        pltpu.VMEM((tm, te), BF16),
                        pltpu.VMEM((tm // LANES, nsub, nkeys, LANES), F32)],
        compiler_params=_params(("parallel", "arbitrary"), PEER_VMEM_LIMIT_MB),
        name="peer_dense",
    )(x, u_tab, v_tab, s_t, s_t, aux)


def _final_kernel(x_ref, p_ref, gp_ref, gs_ref, g_ref, yp_ref, ys_ref, *, nbp):
    i = pl.program_id(0)

    def body(gt_ref, y_ref):
        x = x_ref[...] + _mod_apply(p_ref[...], [gt_ref[...]], lambda p3, gt: gt * p3)
        y_ref[...] = x * lax.rsqrt(jnp.mean(x * x, -1, keepdims=True) + EPS) * g_ref[...]

    @pl.when(i < nbp)
    def _():
        body(gp_ref, yp_ref)

    @pl.when(i >= nbp)
    def _():
        body(gs_ref, ys_ref)


def _final(x, peer, mod, k_gate, g_final, rows):
    d = x.shape[1]
    tm = rows.tm
    gp, gs = rows.mod_specs(k_gate, d)
    return pl.pallas_call(
        functools.partial(_final_kernel, nbp=rows.nbp),
        grid=(rows.nb,),
        in_specs=[pl.BlockSpec((tm, d), lambda i: (i, 0)),
                  pl.BlockSpec((tm, d), lambda i: (i, 0)),
                  gp, gs,
                  pl.BlockSpec((1, d), lambda i: (0, 0))],
        out_specs=[pl.BlockSpec((tm, d), lambda i: (rows.p_tile(i), 0)),
                   pl.BlockSpec((tm, d), lambda i: (rows.s_tile(i), 0))],
        out_shape=[jax.ShapeDtypeStruct((rows.nbp * tm, d), F32),
                   jax.ShapeDtypeStruct((rows.nbs * tm, d), F32)],
        compiler_params=_params(("arbitrary",)),
        name="final_norm",
    )(x, peer, mod, mod, g_final.reshape(1, d))


def kernel(x_prompt, x_sample, cache_k, cache_v, state_C, state_n, state_m, state_conv, c_prompt, c_sample, w_ada, b_ada, g_mix, w_in, conv_w, conv_b, b_gates, m_norm_g, lam_q1, lam_k1, lam_q2, lam_k2, a_norm_g, w_a, w_b, w_out, g_ffn, w_pq, sub_keys, u_tab, v_tab, g_final):
    bp, tp, d = x_prompt.shape
    bs, ts, _ = x_sample.shape
    depth = w_in.shape[0]
    assert depth == 1, "the per-layer state plumbing below is written for a single layer"
    l = 0
    heads, dv = m_norm_g.shape[1:]
    dk = state_C.shape[-1]
    a_heads, a_dv = a_norm_g.shape[1:]
    dh = a_dv // 2
    n_p, n_s = bp * tp, bs * ts
    xp = x_prompt.reshape(n_p, d)
    xs = x_sample.reshape(n_s, d)
    rows_ew = _Rows(bp, tp, bs, ts, _pick(math.gcd(tp, n_s), (256, 128, 64, 32)))
    rows_mm = _Rows(bp, tp, bs, ts, _pick(math.gcd(tp, n_s), (1024, 512, 256, 128, 64, 32)))

    qk_w, v_w = 2 * heads * dk, heads * dv
    a_w = a_heads * a_dv
    col_mv = qk_w
    col_gate = qk_w + 2 * v_w
    col_aq = col_gate + 2 * heads
    col_ak = col_aq + a_w
    col_av = col_ak + a_w
    col_bg = col_av + a_w

    c_all = jnp.concatenate([c_sample, c_prompt], axis=0)
    c_all = jnp.pad(c_all, ((0, (-c_all.shape[0]) % SUBLANES), (0, 0)))
    mod = _ada(c_all, w_ada, b_ada, l).reshape(c_all.shape[0], 6, 1, d)

    hn = _norm_mod(xp, xs, 0, rows_ew, g_mix[l], mod, 1, 0)
    w_t = jnp.swapaxes(w_in, 1, 2)
    qk_raw, = _proj(hn, w_t, l, 0, qk_w, "f32", n_p)
    vo, = _proj(hn, w_t, l, col_mv, 2 * v_w, "bf16", n_p)
    aq, = _proj(hn, w_t, l, col_aq, a_w, "bf16", n_p, scale=LOG2E * dh ** -0.5)
    ak, k_p, k_s = _proj(hn, w_t, l, col_ak, a_w, "kv", n_p)
    av_t, v_p, v_s = _proj(hn, w_t, l, col_av, a_w, "kvt", n_p)
    bgates, = _proj(hn, w_t, l, col_bg, 2 * d, "bf16", n_p)
    gates = _gate_proj(hn, w_t, l, col_gate)

    zc = jnp.zeros((bp, heads, dv, dk), F32)
    zn = jnp.zeros((bp, heads, dk), F32)
    zm = jnp.zeros((bp, heads), F32)
    zconv = jnp.zeros((bp, CONV_W - 1, qk_w), F32)
    mp = _mlstm(qk_raw, vo, gates, 0, bp, tp, zconv, conv_w[l], conv_b[l], b_gates[l], m_norm_g[l], zc, zn, zm)
    ms = _mlstm(qk_raw, vo, gates, n_p, bs, ts, state_conv[l], conv_w[l], conv_b[l], b_gates[l], m_norm_g[l],
                state_C[l], state_n[l], state_m[l])

    lam_init = 0.8 - 0.6 * math.exp(-0.3 * l)
    slopes = jnp.array([2.0 ** (-8.0 * (h + 1) / a_heads) for h in range(a_heads)], F32)
    lams = [a[l].reshape(1, -1) for a in (lam_q1, lam_k1, lam_q2, lam_k2)]
    ha_p = _attn_prompt(aq, ak, av_t, bp, tp, slopes, lams, a_norm_g[l], lam_init)
    ha_s = _attn_sample(aq, k_s, v_s, n_p, bs, ts, cache_k, cache_v, l, slopes, lams, a_norm_g[l], lam_init)

    merged = _merge(mp[0], ms[0], ha_p, ha_s, _cast_bf16(w_a, l), _cast_bf16(w_b, l), bgates, rows_mm)
    x1 = _outproj(merged, _cast_bf16(w_out, l), xp, xs, mod, 2, rows_mm)

    hn2 = _norm_mod(x1, x1, rows_ew.nbp, rows_ew, g_ffn[l], mod, 4, 3)
    qp = _matmul(hn2, _cast_bf16(w_pq, l), F32, "peer_query")
    s_t, aux = _peer_select(qp, sub_keys, l)
    peer = _peer_dense(hn2, _cast_bf16(u_tab, l), _cast_bf16(v_tab, l), s_t, aux)
    y_p, y_s = _final(x1, peer, mod, 5, g_final, rows_ew)

    return (y_p.reshape(bp, tp, d), y_s.reshape(bs, ts, d),
            k_p.reshape(1, bp, tp, a_heads, 2, dh), v_p.reshape(1, bp, tp, a_heads, a_dv),
            mp[1][None], mp[2][None], mp[3][None], mp[4][None],
            k_s.reshape(1, bs, ts, a_heads, 2, dh), v_s.reshape(1, bs, ts, a_heads, a_dv),
            ms[1][None], ms[2][None], ms[3][None], ms[4][None])
```

```python
import functools
import math

import jax
import jax.numpy as jnp
from jax import lax
from jax.experimental import pallas as pl
from jax.experimental.pallas import tpu as pltpu

F32 = jnp.float32
BF16 = jnp.bfloat16

EPS = 1e-6
ATTN_CHUNK = 64
ATTN_KEY_BLOCK = 512
ATTN_HEADS_PER_STEP = 2
CONV_W = 4
P_TOPK = 16
PEER_KEY_BLOCK = 32
PEER_SPLIT = 1
LANES = 128
SUBLANES = 8
BF16_ROWS = 16
MLSTM_CHUNK = 256
VMEM_LIMIT_MB = 56
PEER_VMEM_LIMIT_MB = 60
LOG2E = math.log2(math.e)


def _params(sem, vmem_mb=VMEM_LIMIT_MB):
    return pltpu.CompilerParams(dimension_semantics=sem, vmem_limit_bytes=vmem_mb << 20)


def _pick(n, cands):
    for c in cands:
        if n % c == 0:
            return c
    raise ValueError(f"no tile in {cands} divides {n}")


def _dot(a, b):
    return jnp.dot(a, b, preferred_element_type=F32)


def _dot_nt(a, b):
    return lax.dot_general(a, b, (((1,), (1,)), ((), ())), preferred_element_type=F32)


def _dot_tn(a, b):
    return lax.dot_general(a, b, (((0,), (0,)), ((), ())), preferred_element_type=F32)


def _split3(x):
    hi = x.astype(BF16)
    r = x - hi.astype(F32)
    mid = r.astype(BF16)
    lo = (r - mid.astype(F32)).astype(BF16)
    return hi, mid, lo


def _dot_exact_lhs(a01, x):
    hi, mid, lo = _split3(x)
    return _dot(a01, hi) + _dot(a01, mid) + _dot(a01, lo)


def _dot_exact_nt(a01, x):
    hi, mid, lo = _split3(x)
    return _dot_nt(a01, hi) + _dot_nt(a01, mid) + _dot_nt(a01, lo)


class _Rows:
    def __init__(self, bp, tp, bs, ts, tm):
        assert (bp * tp) % tm == 0 and (bs * ts) % tm == 0
        assert tp % tm == 0 and tm % ts == 0, "a prompt tile sits in one sequence, a sample tile holds whole ones"
        self.bp, self.tp, self.bs, self.ts, self.tm = bp, tp, bs, ts, tm
        self.nbp = bp * tp // tm
        self.nbs = bs * ts // tm
        self.nb = self.nbp + self.nbs
        self.ng = tm // ts

    def p_tile(self, i):
        return jnp.minimum(i, self.nbp - 1)

    def s_tile(self, i):
        return jnp.maximum(i - self.nbp, 0)

    def mod_specs(self, k, width, col=None):
        cj = (lambda *g: 0) if col is None else col
        p = pl.BlockSpec((None, None, 1, width),
                         lambda *g: (self.bs + (self.p_tile(g[0]) * self.tm) // self.tp, k, 0, cj(*g)))
        s = pl.BlockSpec((self.ng, None, 1, width), lambda *g: (self.s_tile(g[0]), k, 0, cj(*g)))
        return p, s


def _mod_apply(x, gate_rows, fn):
    tm, w = x.shape
    rows = [r.reshape((-1, 1, w)) for r in gate_rows]
    ng = rows[0].shape[0]
    return fn(x.reshape(ng, tm // ng, w), *rows).reshape(tm, w)


def _cast_kernel(w_ref, o_ref):
    o_ref[...] = w_ref[...].astype(o_ref.dtype)


def _cast_bf16(w3, l):
    _, r, c = w3.shape
    tr = _pick(r, (512, 256, 128))
    tc = _pick(c, (2048, 1024, 512, 256, 128))
    return pl.pallas_call(
        _cast_kernel,
        grid=(r // tr, c // tc),
        in_specs=[pl.BlockSpec((None, tr, tc), lambda i, j: (l, i, j))],
        out_specs=pl.BlockSpec((tr, tc), lambda i, j: (i, j)),
        out_shape=jax.ShapeDtypeStruct((r, c), BF16),
        compiler_params=_params(("parallel", "parallel")),
        name="cast_bf16",
    )(w3)


def _ada_kernel(c_ref, w_ref, b_ref, o_ref):
    c = c_ref[...]
    a = (c * jax.nn.sigmoid(c)).astype(BF16)
    o_ref[...] = _dot(a, w_ref[...].astype(BF16)) + b_ref[...]


def _ada(c_all, w_ada, b_ada, l):
    m, d = c_all.shape
    n = w_ada.shape[2]
    tn = _pick(n, (512, 256, 128))
    return pl.pallas_call(
        _ada_kernel,
        grid=(n // tn,),
        in_specs=[pl.BlockSpec((m, d), lambda j: (0, 0)),
                  pl.BlockSpec((None, d, tn), lambda j: (l, 0, j)),
                  pl.BlockSpec((None, 1, tn), lambda j: (l, 0, j))],
        out_specs=pl.BlockSpec((m, tn), lambda j: (0, j)),
        out_shape=jax.ShapeDtypeStruct((m, n), F32),
        compiler_params=_params(("parallel",)),
        name="ada_mod",
    )(c_all, w_ada, b_ada.reshape(b_ada.shape[0], 1, n))


def _norm_mod_kernel(xp_ref, xs_ref, g_ref, scp_ref, shp_ref, scs_ref, shs_ref, o_ref, *, nbp):
    i = pl.program_id(0)

    def body(x_ref, sc_ref, sh_ref):
        x = x_ref[...]
        y = x * lax.rsqrt(jnp.mean(x * x, -1, keepdims=True) + EPS) * g_ref[...]
        y = _mod_apply(y, [sc_ref[...], sh_ref[...]], lambda y3, sc, sh: y3 * (1.0 + sc) + sh)
        o_ref[...] = y.astype(o_ref.dtype)

    @pl.when(i < nbp)
    def _():
        body(xp_ref, scp_ref, shp_ref)

    @pl.when(i >= nbp)
    def _():
        body(xs_ref, scs_ref, shs_ref)


def _norm_mod(xp, xs, s_off, rows, g, mod, k_scale, k_shift):
    d = xp.shape[1]
    tm = rows.tm
    scp, scs = rows.mod_specs(k_scale, d)
    shp, shs = rows.mod_specs(k_shift, d)
    return pl.pallas_call(
        functools.partial(_norm_mod_kernel, nbp=rows.nbp),
        grid=(rows.nb,),
        in_specs=[pl.BlockSpec((tm, d), lambda i: (rows.p_tile(i), 0)),
                  pl.BlockSpec((tm, d), lambda i: (s_off + rows.s_tile(i), 0)),
                  pl.BlockSpec((1, d), lambda i: (0, 0)),
                  scp, shp, scs, shs],
        out_specs=pl.BlockSpec((tm, d), lambda i: (i, 0)),
        out_shape=jax.ShapeDtypeStruct((rows.nb * tm, d), BF16),
        compiler_params=_params(("arbitrary",)),
        name="norm_mod",
    )(xp, xs, g.reshape(1, d), mod, mod, mod, mod)


def _proj_kernel(x_ref, w_ref, wn_ref, *refs, shift, scale, mode, nip):
    outs, wb_ref = refs[:-1], refs[-1]
    i = pl.program_id(1)

    @pl.when(i == 0)
    def _():
        tn = w_ref.shape[0]
        cr = _pick(tn, (256, 128))
        for r in range(0, tn, cr):
            if r + cr + shift <= tn:
                wb_ref[r:r + cr, :] = w_ref[r + shift:r + cr + shift, :].astype(BF16)
            else:
                wb_ref[r:r + cr - shift, :] = w_ref[r + shift:r + cr, :].astype(BF16)
                wb_ref[r + cr - shift:r + cr, :] = wn_ref[...].astype(BF16)

    acc = _dot_nt(x_ref[...], wb_ref[...])
    if scale != 1.0:
        acc = acc * scale
    if mode in ("kv", "kvt"):
        outs[0][...] = (acc.T if mode == "kvt" else acc).astype(BF16)

        @pl.when(i < nip)
        def _():
            outs[1][...] = acc

        @pl.when(i >= nip)
        def _():
            outs[2][...] = acc
    else:
        outs[0][...] = acc.astype(outs[0].dtype)


def _proj(hn, w_t, l, col0, ncols, mode, n_p, scale=1.0):
    n, k = hn.shape
    tm = _pick(math.gcd(n_p, n - n_p), ((1024,) if mode != "kvt" else ()) + (ATTN_KEY_BLOCK, 256, 128))
    tn = _pick(ncols, ((1024,) if mode == "bf16" else ()) + (512, 256, 128))
    shift = col0 % tn
    assert (col0 - shift) % tn == 0 and shift % BF16_ROWS == 0 and shift < tn
    cb0 = (col0 - shift) // tn
    nip = n_p // tm
    ni, nj = n // tm, ncols // tn
    once = pl.Buffered(1)
    nxt = shift if shift else BF16_ROWS
    in_specs = [pl.BlockSpec((tm, k), lambda j, i: (i, 0)),
                pl.BlockSpec((None, tn, k), lambda j, i: (l, cb0 + j, 0), pipeline_mode=once),
                pl.BlockSpec((None, nxt, k), lambda j, i: (l, (cb0 + j + 1) * (tn // nxt) if shift else 0, 0),
                             pipeline_mode=once)]
    all_spec = pl.BlockSpec((tm, tn), lambda j, i: (i, j))
    if mode in ("kv", "kvt"):
        first = (pl.BlockSpec((None, tn, tm), lambda j, i: (i, j, 0)) if mode == "kvt" else all_spec)
        out_specs = [first,
                     pl.BlockSpec((tm, tn), lambda j, i: (jnp.minimum(i, nip - 1), j)),
                     pl.BlockSpec((tm, tn), lambda j, i: (jnp.maximum(i - nip, 0), j))]
        out_shape = [jax.ShapeDtypeStruct((ni, ncols, tm) if mode == "kvt" else (n, ncols), BF16),
                     jax.ShapeDtypeStruct((n_p, ncols), F32),
                     jax.ShapeDtypeStruct((n - n_p, ncols), F32)]
    else:
        out_specs = [all_spec]
        out_shape = [jax.ShapeDtypeStruct((n, ncols), F32 if mode == "f32" else BF16)]
    kern = functools.partial(_proj_kernel, shift=shift, scale=scale, mode=mode, nip=nip)
    return pl.pallas_call(
        kern,
        grid=(nj, ni),
        in_specs=in_specs,
        out_specs=out_specs,
        out_shape=out_shape,
        scratch_shapes=[pltpu.VMEM((tn, k), BF16)],
        compiler_params=_params(("arbitrary", "arbitrary")),
        name="proj_" + mode,
    )(hn, w_t, w_t)


def _gate_kernel(x_ref, w_ref, o_ref):
    o_ref[...] = _dot_nt(x_ref[...], w_ref[...].astype(BF16))


def _gate_proj(hn, w_t, l, col0):
    n, k = hn.shape
    assert col0 % LANES == 0
    tm = _pick(n, (1024, 512, 256, 128))
    return pl.pallas_call(
        _gate_kernel,
        grid=(n // tm,),
        in_specs=[pl.BlockSpec((tm, k), lambda i: (i, 0)),
                  pl.BlockSpec((None, LANES, k), lambda i: (l, col0 // LANES, 0))],
        out_specs=pl.BlockSpec((tm, LANES), lambda i: (i, 0)),
        out_shape=jax.ShapeDtypeStruct((n, LANES), F32),
        compiler_params=_params(("parallel",)),
        name="gate_proj",
    )(hn, w_t)


def _mm_kernel(a_ref, b_ref, o_ref):
    o_ref[...] = _dot(a_ref[...], b_ref[...]).astype(o_ref.dtype)


def _matmul(a, b, out_dtype, name):
    m, k = a.shape
    n = b.shape[1]
    tm = _pick(m, (1024, 512, 256, 128))
    tn = _pick(n, (1024, 512, 256, 128))
    return pl.pallas_call(
        _mm_kernel,
        grid=(m // tm, n // tn),
        in_specs=[pl.BlockSpec((tm, k), lambda i, j: (i, 0)),
                  pl.BlockSpec((k, tn), lambda i, j: (0, j))],
        out_specs=pl.BlockSpec((tm, tn), lambda i, j: (i, j)),
        out_shape=jax.ShapeDtypeStruct((m, n), out_dtype),
        compiler_params=_params(("parallel", "parallel")),
        name=name,
    )(a, b)


def _mlstm_kernel(qk_ref, v_ref, og_ref, gt_ref, conv0_ref, cw_ref, cb_ref, bg_ref, ng_ref,
                  c0_ref, n0_ref, m0_ref,
                  hm_ref, c_ref, n_ref, m_ref, convn_ref, full_ref, *, heads, dk, dv):
    L = qk_ref.shape[0]
    pad = SUBLANES
    hist = CONV_W - 1
    ci = pl.program_id(1)

    @pl.when(ci == 0)
    def _():
        c_ref[...] = c0_ref[...]
        n_ref[...] = n0_ref[...]
        m_ref[...] = m0_ref[...]
        full_ref[pad - hist:pad, :] = conv0_ref[0]

    u = qk_ref[...]
    full_ref[pad:pad + L, :] = u
    acc = cb_ref[...] + u * cw_ref[hist:hist + 1, :]
    for j in range(hist):
        acc = acc + full_ref[pad - hist + j:pad - hist + j + L, :] * cw_ref[j:j + 1, :]
    tail = full_ref[pad + L - hist:pad + L, :]
    convn_ref[0] = tail
    full_ref[pad - hist:pad, :] = tail
    qk = acc * jax.nn.sigmoid(acc)

    row = lax.broadcasted_iota(jnp.int32, (L, L), 0)
    col = lax.broadcasted_iota(jnp.int32, (L, L), 1)
    causal = col <= row
    tril01 = jnp.where(causal, 1.0, 0.0).astype(BF16)
    glanes = gt_ref.shape[1]
    pre = gt_ref[...] + bg_ref[...]
    logf_all = jnp.minimum(pre, 0.0) - jnp.log1p(jnp.exp(-jnp.abs(pre)))
    b_all = _dot_exact_lhs(tril01, logf_all)
    eye01 = jnp.where(lax.broadcasted_iota(jnp.int32, (glanes, glanes), 0)
                      == lax.broadcasted_iota(jnp.int32, (glanes, glanes), 1), 1.0, 0.0).astype(BF16)
    pre_t = _dot_exact_nt(eye01, pre)
    b_t = _dot_exact_nt(eye01, b_all)

    for h in range(heads):
        q = qk[:, h * dk:(h + 1) * dk]
        k = qk[:, (heads + h) * dk:(heads + h + 1) * dk] * (dk ** -0.5)
        vb = v_ref[:, h * dv:(h + 1) * dv]
        v = vb.astype(F32)
        qb, kb = q.astype(BF16), k.astype(BF16)
        i_col = pre[:, h:h + 1]
        b_col = b_all[:, heads + h:heads + h + 1]
        dmat = jnp.where(causal, b_col + (pre_t[h:h + 1, :] - b_t[heads + h:heads + h + 1, :]), -jnp.inf)
        m_prev = m_ref[0, h:h + 1, 0:1]
        inter = b_col + m_prev
        m_t = jnp.maximum(inter, jnp.max(dmat, axis=1, keepdims=True))
        s = _dot_nt(qb, kb) * jnp.exp(dmat - m_t)
        w_prev = jnp.exp(inter - m_t)
        c_old = c_ref[0, h]
        n_old = n_ref[0, h:h + 1, :]
        num = _dot(s.astype(BF16), vb) + w_prev * _dot_nt(qb, c_old.astype(BF16))
        den = jnp.sum(s, axis=1, keepdims=True) + w_prev * jnp.sum(q * n_old, axis=1, keepdims=True)
        hh = num / jnp.maximum(jnp.abs(den), jnp.exp(-m_t))
        m_new = m_t[L - 1:L, :]
        b_last = b_col[L - 1:L, :]
        decay = jnp.exp(b_last + m_prev - m_new)
        w_src = jnp.exp(b_last - b_col + i_col - m_new)
        c_ref[0, h] = decay * c_old + _dot_tn((w_src * v).astype(BF16), kb)
        n_ref[0, h:h + 1, :] = decay * n_old + jnp.sum(w_src * k, axis=0, keepdims=True)
        m_ref[0, h:h + 1, :] = jnp.broadcast_to(m_new, (1, m_ref.shape[2]))
        hn = hh * lax.rsqrt(jnp.mean(hh * hh, -1, keepdims=True) + EPS) * ng_ref[h:h + 1, :]
        hn = hn * jax.nn.sigmoid(og_ref[:, h * dv:(h + 1) * dv].astype(F32))
        hm_ref[:, h * dv:(h + 1) * dv] = hn.astype(hm_ref.dtype)


def _mlstm(qk_raw, vo, gates, row0, nb, t, conv0, conv_w, conv_b, b_gates, norm_g, c0, n0, m0):
    heads, dv = norm_g.shape
    dk = c0.shape[-1]
    qkw = 2 * heads * dk
    vw = heads * dv
    L = _pick(t, (MLSTM_CHUNK, 128, 64, 32, 16, 8))
    nc = t // L
    assert row0 % L == 0 and L >= CONV_W - 1
    rb0 = row0 // L
    m0b = jnp.broadcast_to(m0[:, :, None], (nb, heads, LANES))
    glanes = gates.shape[1]
    assert 2 * heads <= glanes
    gate_bias = jnp.pad(b_gates.reshape(1, 2 * heads), ((0, 0), (0, glanes - 2 * heads)))
    rmap = lambda b, c: rb0 + b * nc + c
    kern = functools.partial(_mlstm_kernel, heads=heads, dk=dk, dv=dv)
    hm, c1, n1, m1, convn = pl.pallas_call(
        kern,
        grid=(nb, nc),
        in_specs=[pl.BlockSpec((L, qkw), lambda b, c: (rmap(b, c), 0)),
                  pl.BlockSpec((L, vw), lambda b, c: (rmap(b, c), 0)),
                  pl.BlockSpec((L, vw), lambda b, c: (rmap(b, c), 1)),
                  pl.BlockSpec((L, gates.shape[1]), lambda b, c: (rmap(b, c), 0)),
                  pl.BlockSpec((1, CONV_W - 1, qkw), lambda b, c: (b, 0, 0)),
                  pl.BlockSpec((CONV_W, qkw), lambda b, c: (0, 0)),
                  pl.BlockSpec((1, qkw), lambda b, c: (0, 0)),
                  pl.BlockSpec((1, glanes), lambda b, c: (0, 0)),
                  pl.BlockSpec((heads, dv), lambda b, c: (0, 0)),
                  pl.BlockSpec((1, heads, dv, dk), lambda b, c: (b, 0, 0, 0)),
                  pl.BlockSpec((1, heads, dk), lambda b, c: (b, 0, 0)),
                  pl.BlockSpec((1, heads, LANES), lambda b, c: (b, 0, 0))],
        out_specs=[pl.BlockSpec((L, vw), lambda b, c: (b * nc + c, 0)),
                   pl.BlockSpec((1, heads, dv, dk), lambda b, c: (b, 0, 0, 0)),
                   pl.BlockSpec((1, heads, dk), lambda b, c: (b, 0, 0)),
                   pl.BlockSpec((1, heads, LANES), lambda b, c: (b, 0, 0)),
                   pl.BlockSpec((1, CONV_W - 1, qkw), lambda b, c: (b, 0, 0))],
        out_shape=[jax.ShapeDtypeStruct((nb * t, vw), BF16),
                   jax.ShapeDtypeStruct((nb, heads, dv, dk), F32),
                   jax.ShapeDtypeStruct((nb, heads, dk), F32),
                   jax.ShapeDtypeStruct((nb, heads, LANES), F32),
                   jax.ShapeDtypeStruct((nb, CONV_W - 1, qkw), F32)],
        scratch_shapes=[pltpu.VMEM((SUBLANES + L, qkw), F32)],
        compiler_params=_params(("parallel", "arbitrary")),
        name="mlstm",
    )(qk_raw, vo, vo, gates, conv0, conv_w, conv_b.reshape(1, qkw), gate_bias, norm_g, c0, n0, m0b)
    return hm, c1, n1, m1[:, :, 0], convn


def _lambda(lq1, lk1, lq2, lk2, lam_init):
    return (jnp.exp(jnp.sum(lq1[...] * lk1[...], axis=1, keepdims=True))
            - jnp.exp(jnp.sum(lq2[...] * lk2[...], axis=1, keepdims=True)) + lam_init)


def _head_norm_out(o, g_row, lam_init):
    return o * lax.rsqrt(jnp.mean(o * o, -1, keepdims=True) + EPS) * g_row * (1.0 - lam_init)


def _attn_prompt_kernel(slope_ref, q_ref, k_ref, vt_ref, lq1, lk1, lq2, lk2, ng_ref, o_ref,
                        m_ref, l_ref, acc_ref, bias_ref, dbias_ref, *, dh, lam_init):
    tq = q_ref.shape[0]
    tk = vt_ref.shape[2]
    dv = 2 * dh
    hp = pl.program_id(1)
    qi = pl.program_id(2)
    nh = ATTN_HEADS_PER_STEP
    slopes2 = [slope_ref[hp * nh + hh] * LOG2E for hh in range(nh)]
    chains = [(hh, c) for hh in range(nh) for c in range(2)]

    @pl.when(qi == 0)
    def _():
        krow = lax.broadcasted_iota(jnp.int32, (tk, tq), 0)
        qcol = lax.broadcasted_iota(jnp.int32, (tk, tq), 1)
        kf = krow.astype(F32)
        ahead = jnp.maximum(krow - qcol, 0).astype(F32)
        vis = (krow // ATTN_CHUNK) <= (qcol // ATTN_CHUNK)
        for hh in range(nh):
            bias_ref[hh] = slopes2[hh] * kf
            dbias_ref[hh] = jnp.where(vis, slopes2[hh] * kf - (2.0 * slopes2[hh]) * ahead, -jnp.inf)

    m_ref[...] = jnp.full(m_ref.shape, -jnp.inf, F32)
    l_ref[...] = jnp.zeros(l_ref.shape, F32)
    acc_ref[...] = jnp.zeros(acc_ref.shape, F32)

    def update(kj, b_ref):
        kb = k_ref[pl.ds(pl.multiple_of(kj * tk, tk), tk), :]
        vt = vt_ref[kj]
        rel = (kj * tk - qi * tq).astype(F32)
        ss = [_dot_nt(kb[:, hh * dv + c * dh:hh * dv + (c + 1) * dh],
                      q_ref[:, hh * dv + c * dh:hh * dv + (c + 1) * dh]) for hh, c in chains]
        ps, alphas = [], []
        for ci, (hh, c) in enumerate(chains):
            shift = slopes2[hh] * rel
            s = ss[ci] + b_ref[hh]
            m_old = m_ref[ci]
            m_new = jnp.maximum(m_old, jnp.max(s, axis=0, keepdims=True) + shift)
            alpha = jnp.exp2(m_old - m_new)
            p = jnp.exp2(s - (m_new - shift))
            l_ref[ci] = alpha * l_ref[ci] + jnp.sum(p, axis=0, keepdims=True)
            m_ref[ci] = m_new
            ps.append(p.astype(BF16))
            alphas.append(alpha)
        for ci, (hh, c) in enumerate(chains):
            acc_ref[ci] = alphas[ci] * acc_ref[ci] + _dot(vt[hh * dv:(hh + 1) * dv, :], ps[ci])

    def body(kj, carry):
        update(kj, bias_ref)
        return carry

    lax.fori_loop(0, qi, body, 0)
    update(qi, dbias_ref)
    lam = _lambda(lq1, lk1, lq2, lk2, lam_init)
    for hh in range(nh):
        o = acc_ref[2 * hh] / l_ref[2 * hh] - lam * (acc_ref[2 * hh + 1] / l_ref[2 * hh + 1])
        o = o * lax.rsqrt(jnp.mean(o * o, axis=0, keepdims=True) + EPS)
        o_ref[:, hh * dv:(hh + 1) * dv] = (o.T * ng_ref[hh:hh + 1, :] * (1.0 - lam_init)).astype(o_ref.dtype)


def _attn_prompt(aq, ak, av_t, nb, t, slopes, lams, norm_g, lam_init):
    heads, dv = norm_g.shape
    dh = dv // 2
    tk = av_t.shape[2]
    tq = tk
    nq = t // tq
    nh = ATTN_HEADS_PER_STEP
    assert t % tq == 0 and tq % ATTN_CHUNK == 0 and tq % LANES == 0 and heads % nh == 0
    kern = functools.partial(_attn_prompt_kernel, dh=dh, lam_init=lam_init)
    lam_spec = pl.BlockSpec((1, dh), lambda b, h, i: (0, 0))
    return pl.pallas_call(
        kern,
        grid=(nb, heads // nh, nq),
        in_specs=[pl.BlockSpec(memory_space=pltpu.SMEM),
                  pl.BlockSpec((tq, nh * dv), lambda b, h, i: (b * nq + i, h)),
                  pl.BlockSpec((t, nh * dv), lambda b, h, i: (b, h)),
                  pl.BlockSpec((nq, nh * dv, tk), lambda b, h, i: (b, h, 0)),
                  lam_spec, lam_spec, lam_spec, lam_spec,
                  pl.BlockSpec((None, nh, dv), lambda b, h, i: (h, 0, 0))],
        out_specs=pl.BlockSpec((tq, nh * dv), lambda b, h, i: (b * nq + i, h)),
        out_shape=jax.ShapeDtypeStruct((nb * t, heads * dv), BF16),
        scratch_shapes=[pltpu.VMEM((2 * nh, 1, tq), F32),
                        pltpu.VMEM((2 * nh, 1, tq), F32),
                        pltpu.VMEM((2 * nh, dv, tq), F32),
                        pltpu.VMEM((nh, tk, tq), F32),
                        pltpu.VMEM((nh, tk, tq), F32)],
        compiler_params=_params(("arbitrary", "arbitrary", "arbitrary")),
        name="attn_prompt",
    )(slopes, aq, ak, av_t, *lams, norm_g.reshape(heads // nh, nh, dv))


def _attn_sample_kernel(slope_ref, q_ref, kn_ref, vn_ref, kp_ref, vp_ref, lq1, lk1, lq2, lk2, ng_ref,
                        o_ref, m_ref, l_ref, acc_ref, *, heads, dh, past, lam_init):
    t = q_ref.shape[0]
    dv = 2 * dh
    tc = vp_ref.shape[0] // (2 * heads)
    ci = pl.program_id(1)
    rowp = past + lax.broadcasted_iota(jnp.int32, (t, 1), 0)

    @pl.when(ci == 0)
    def _():
        m_ref[...] = jnp.full(m_ref.shape, -jnp.inf, F32)
        l_ref[...] = jnp.zeros(l_ref.shape, F32)
        acc_ref[...] = jnp.zeros(acc_ref.shape, F32)

    def update(idx, q, k, v, bias):
        s = _dot_nt(q, k) + bias
        m_old = m_ref[idx]
        m_new = jnp.maximum(m_old, jnp.max(s, axis=1, keepdims=True))
        alpha = jnp.exp2(m_old - m_new)
        p = jnp.exp2(s - m_new)
        l_ref[idx] = alpha * l_ref[idx] + jnp.sum(p, axis=1, keepdims=True)
        acc_ref[idx] = alpha * acc_ref[idx] + _dot(p.astype(BF16), v)
        m_ref[idx] = m_new

    colp = ci * tc + lax.broadcasted_iota(jnp.int32, (1, tc), 1)
    dist = (rowp - colp).astype(F32)
    ss = []
    for h in range(heads):
        for c in range(2):
            k = kp_ref[pl.ds(2 * h + c, tc, stride=2 * heads), :].astype(BF16)
            ss.append(_dot_nt(q_ref[:, h * dv + c * dh:h * dv + (c + 1) * dh], k))
    ps, alphas = [], []
    for h in range(heads):
        bias = -(slope_ref[h] * LOG2E) * dist
        for c in range(2):
            idx = 2 * h + c
            s = ss[idx] + bias
            m_old = m_ref[idx]
            m_new = jnp.maximum(m_old, jnp.max(s, axis=1, keepdims=True))
            alpha = jnp.exp2(m_old - m_new)
            p = jnp.exp2(s - m_new)
            l_ref[idx] = alpha * l_ref[idx] + jnp.sum(p, axis=1, keepdims=True)
            m_ref[idx] = m_new
            ps.append(p.astype(BF16))
            alphas.append(alpha)
    for h in range(heads):
        v = jnp.concatenate([vp_ref[pl.ds(j * heads + h, tc, stride=2 * heads), :] for j in range(2)],
                            axis=1).astype(BF16)
        for c in range(2):
            idx = 2 * h + c
            acc_ref[idx] = alphas[idx] * acc_ref[idx] + _dot(ps[idx], v)

    @pl.when(ci == pl.num_programs(1) - 1)
    def _():
        coln = past + lax.broadcasted_iota(jnp.int32, (1, t), 1)
        dist_n = jnp.abs((rowp - coln).astype(F32))
        vis = (coln // ATTN_CHUNK) <= (rowp // ATTN_CHUNK)
        lam = _lambda(lq1, lk1, lq2, lk2, lam_init)
        for h in range(heads):
            slope2 = slope_ref[h] * LOG2E
            bias = jnp.where(vis, -slope2 * dist_n, -jnp.inf)
            v = vn_ref[:, h * dv:(h + 1) * dv].astype(BF16)
            outs = []
            for c in range(2):
                cols = slice(h * dv + c * dh, h * dv + (c + 1) * dh)
                update(2 * h + c, q_ref[:, cols], kn_ref[:, cols].astype(BF16), v, bias)
                outs.append(acc_ref[2 * h + c] / l_ref[2 * h + c])
            o = outs[0] - lam * outs[1]
            o_ref[:, h * dv:(h + 1) * dv] = _head_norm_out(o, ng_ref[h:h + 1, :], lam_init).astype(o_ref.dtype)


def _attn_sample(aq, k_new, v_new, row0, nb, t, k_past, v_past, l, slopes, lams, norm_g, lam_init):
    heads, dv = norm_g.shape
    dh = dv // 2
    past = k_past.shape[2]
    assert row0 % t == 0 and t % 8 == 0
    rb0 = row0 // t
    tc = _pick(past, (512, 256, 128, 64, 32, 16, 8))
    kp = k_past.reshape(k_past.shape[0], nb, past * heads * 2, dh)
    vp = v_past.reshape(v_past.shape[0], nb, past, heads, 2, dh).transpose(0, 1, 2, 4, 3, 5)
    vp = vp.reshape(v_past.shape[0], nb, past * 2 * heads, dh)
    kern = functools.partial(_attn_sample_kernel, heads=heads, dh=dh, past=past, lam_init=lam_init)
    lam_spec = pl.BlockSpec((1, dh), lambda b, c: (0, 0))
    new_spec = pl.BlockSpec((t, heads * dv), lambda b, c: (b, 0))
    return pl.pallas_call(
        kern,
        grid=(nb, past // tc),
        in_specs=[pl.BlockSpec(memory_space=pltpu.SMEM),
                  pl.BlockSpec((t, heads * dv), lambda b, c: (rb0 + b, 0)), new_spec, new_spec,
                  pl.BlockSpec((None, None, tc * heads * 2, dh), lambda b, c: (l, b, c, 0)),
                  pl.BlockSpec((None, None, tc * heads * 2, dh), lambda b, c: (l, b, c, 0)),
                  lam_spec, lam_spec, lam_spec, lam_spec,
                  pl.BlockSpec((heads, dv), lambda b, c: (0, 0))],
        out_specs=pl.BlockSpec((t, heads * dv), lambda b, c: (b, 0)),
        out_shape=jax.ShapeDtypeStruct((nb * t, heads * dv), BF16),
        scratch_shapes=[pltpu.VMEM((2 * heads, t, 1), F32),
                        pltpu.VMEM((2 * heads, t, 1), F32),
                        pltpu.VMEM((2 * heads, t, dv), F32)],
        compiler_params=_params(("arbitrary", "arbitrary")),
        name="attn_sample",
    )(slopes, aq, k_new, v_new, kp, vp, *lams, norm_g)


def _merge_kernel(hmp_ref, hms_ref, hap_ref, has_ref, wa_ref, wb_ref, g0_ref, g1_ref, o_ref, *, nbp):
    i = pl.program_id(0)

    def body(hm_ref, ha_ref):
        ya = _dot(hm_ref[...], wa_ref[...])
        yb = _dot(ha_ref[...], wb_ref[...])
        g0 = jax.nn.sigmoid(g0_ref[...].astype(F32))
        g1 = jax.nn.sigmoid(g1_ref[...].astype(F32))
        o_ref[...] = (g0 * ya + g1 * yb).astype(o_ref.dtype)

    @pl.when(i < nbp)
    def _():
        body(hmp_ref, hap_ref)

    @pl.when(i >= nbp)
    def _():
        body(hms_ref, has_ref)


def _merge(hm_p, hm_s, ha_p, ha_s, w_a, w_b, gates, rows):
    ka, kb = hm_p.shape[1], ha_p.shape[1]
    d = w_a.shape[1]
    tm = rows.tm
    tn = _pick(d, (512, 256, 128))
    nj = d // tn
    pmap = lambda i, j: (rows.p_tile(i), 0)
    smap = lambda i, j: (rows.s_tile(i), 0)
    return pl.pallas_call(
        functools.partial(_merge_kernel, nbp=rows.nbp),
        grid=(rows.nb, nj),
        in_specs=[pl.BlockSpec((tm, ka), pmap), pl.BlockSpec((tm, ka), smap, pipeline_mode=pl.Buffered(1)),
                  pl.BlockSpec((tm, kb), pmap), pl.BlockSpec((tm, kb), smap, pipeline_mode=pl.Buffered(1)),
                  pl.BlockSpec((ka, tn), lambda i, j: (0, j)),
                  pl.BlockSpec((kb, tn), lambda i, j: (0, j)),
                  pl.BlockSpec((tm, tn), lambda i, j: (i, j)),
                  pl.BlockSpec((tm, tn), lambda i, j: (i, nj + j))],
        out_specs=pl.BlockSpec((tm, tn), lambda i, j: (i, j)),
        out_shape=jax.ShapeDtypeStruct((rows.nb * tm, d), BF16),
        compiler_params=_params(("arbitrary", "arbitrary")),
        name="merge",
    )(hm_p, hm_s, ha_p, ha_s, w_a, w_b, gates, gates)


def _outproj_kernel(a_ref, w_ref, xp_ref, xs_ref, gp_ref, gs_ref, o_ref, *, nbp):
    i = pl.program_id(0)
    y = _dot(a_ref[...], w_ref[...])

    def body(x_ref, gt_ref):
        o_ref[...] = x_ref[...] + _mod_apply(y, [gt_ref[...]], lambda y3, gt: gt * y3)

    @pl.when(i < nbp)
    def _():
        body(xp_ref, gp_ref)

    @pl.when(i >= nbp)
    def _():
        body(xs_ref, gs_ref)


def _outproj(a, w, xp, xs, mod, k_gate, rows):
    k = a.shape[1]
    d = w.shape[1]
    tm = rows.tm
    tn = _pick(d, (512, 256, 128))
    gp, gs = rows.mod_specs(k_gate, tn, col=lambda i, j: j)
    return pl.pallas_call(
        functools.partial(_outproj_kernel, nbp=rows.nbp),
        grid=(rows.nb, d // tn),
        in_specs=[pl.BlockSpec((tm, k), lambda i, j: (i, 0)),
                  pl.BlockSpec((k, tn), lambda i, j: (0, j)),
                  pl.BlockSpec((tm, tn), lambda i, j: (rows.p_tile(i), j)),
                  pl.BlockSpec((tm, tn), lambda i, j: (rows.s_tile(i), j)),
                  gp, gs],
        out_specs=pl.BlockSpec((tm, tn), lambda i, j: (i, j)),
        out_shape=jax.ShapeDtypeStruct((rows.nb * tm, d), F32),
        compiler_params=_params(("arbitrary", "arbitrary")),
        name="outproj",
    )(a, w, xp, xs, mod, mod)


def _top_values(s, k):
    vals = []
    for r in range(k):
        m = jnp.max(s, axis=0, keepdims=True)
        vals.append(m)
        if r + 1 < k:
            s = jnp.where(s == m, -jnp.inf, s)
    return vals


def _peer_select_kernel(q_ref, sk_ref, s_ref, aux_ref, *, heads, dsub):
    tm = q_ref.shape[0]
    for h in range(heads):
        tops = []
        for c in range(2):
            qh = q_ref[:, (2 * h + c) * dsub:(2 * h + c + 1) * dsub]
            sk = sk_ref[h, c]
            qh_hi = qh.astype(BF16)
            qh_lo = (qh - qh_hi.astype(F32)).astype(BF16)
            sk_hi = sk.astype(BF16)
            sk_lo = (sk - sk_hi.astype(F32)).astype(BF16)
            s = _dot_nt(sk_hi, qh_hi) + _dot_nt(sk_hi, qh_lo) + _dot_nt(sk_lo, qh_hi)
            s_ref[h, c] = s
            tops.append(_top_values(s, P_TOPK))
        t2 = jnp.concatenate(tops[1], axis=0)
        half = P_TOPK // 2
        cand = jnp.concatenate([tops[0][0] + t2]
                               + [tops[0][a] + t2[0:half] for a in range(1, half)]
                               + [jnp.concatenate(tops[0][half:], axis=0) + tops[1][0]], axis=0)
        best = _top_values(cand, P_TOPK)
        z = jnp.ones((1, tm), F32)
        for r in range(1, P_TOPK):
            z = z + jnp.exp(best[r] - best[0])
        rz = 1.0 / z
        e1 = [jnp.exp(v - tops[0][0]) for v in tops[0]]
        e2 = jnp.exp(t2 - tops[1][0]) * rz
        prods = jnp.concatenate([e1[0] * e2]
                                + [e1[a] * e2[0:half] for a in range(1, half)]
                                + [jnp.concatenate(e1[half:], axis=0) * e2[0:1]], axis=0)
        theta = jnp.min(jnp.where(cand >= best[P_TOPK - 1], prods, jnp.inf), axis=0, keepdims=True)
        rows = [theta, tops[0][0], tops[1][0], rz]
        aux_ref[h] = jnp.concatenate(rows + [jnp.zeros((SUBLANES - len(rows), tm), F32)], axis=0)


def _peer_select(qp, sub_keys, l):
    n = qp.shape[0]
    _, heads, _, nkeys, dsub = sub_keys.shape
    tm = _pick(n, (512, 256, 128))
    kern = functools.partial(_peer_select_kernel, heads=heads, dsub=dsub)
    return pl.pallas_call(
        kern,
        grid=(n // tm,),
        in_specs=[pl.BlockSpec((tm, qp.shape[1]), lambda i: (i, 0)),
                  pl.BlockSpec((None,) + sub_keys.shape[1:], lambda i: (l, 0, 0, 0, 0))],
        out_specs=[pl.BlockSpec((heads, 2, nkeys, tm), lambda i: (0, 0, 0, i)),
                   pl.BlockSpec((heads, SUBLANES, tm), lambda i: (0, 0, i))],
        out_shape=[jax.ShapeDtypeStruct((heads, 2, nkeys, n), F32),
                   jax.ShapeDtypeStruct((heads, SUBLANES, n), F32)],
        compiler_params=_params(("parallel",)),
        name="peer_select",
    )(qp, sub_keys)


def _peer_dense_kernel(x_ref, u_ref, v_ref, s1_ref, s2_ref, aux_ref, o_ref,
                       s1t_ref, e1_ref, e2_ref, p_ref, pblk_ref, *, heads, nkeys):
    te = u_ref.shape[0]
    tm = x_ref.shape[0]
    e = pl.program_id(1)
    nsub = te // nkeys
    grows = te // PEER_SPLIT
    gsub = nsub // PEER_SPLIT

    @pl.when(e == 0)
    def _():
        o_ref[...] = jnp.zeros(o_ref.shape, F32)
        for h in range(heads):
            e2_ref[h] = jnp.exp(s2_ref[h] - aux_ref[h, 2:3, :]) * aux_ref[h, 3:4, :]

    scores = _dot_nt(u_ref[...], x_ref[...])

    grp = e % (SUBLANES // nsub)
    for h in range(heads):
        rows = s1_ref[h, 0:nsub, :]
        for gi in range(1, SUBLANES // nsub):
            rows = jnp.where(grp == gi, s1_ref[h, gi * nsub:(gi + 1) * nsub, :], rows)
        s1t_ref[h, 0:nsub, :] = rows
        e1_ref[h, 0:nsub, :] = jnp.exp(rows - aux_ref[h, 1:2, :])

    for g in range(PEER_SPLIT):
        for tb in range(tm // LANES):
            cs = slice(tb * LANES, (tb + 1) * LANES)
            for kb in range(nkeys // PEER_KEY_BLOCK):
                rs = slice(kb * PEER_KEY_BLOCK, (kb + 1) * PEER_KEY_BLOCK)
                w = [None] * gsub
                for h in range(heads):
                    e2_blk = e2_ref[h, rs, cs]
                    tau = aux_ref[h, 0:1, cs]
                    for a in range(gsub):
                        ar = g * gsub + a
                        prod = e1_ref[h, ar:ar + 1, cs] * e2_blk
                        gate = jnp.where(prod >= tau, prod, 0.0)
                        w[a] = gate if w[a] is None else w[a] + gate
                for a in range(gsub):
                    r0 = a * nkeys + kb * PEER_KEY_BLOCK
                    act = jax.nn.gelu(scores[g * grows + r0:g * grows + r0 + PEER_KEY_BLOCK, cs], approximate=True)
                    pblk_ref[tb, g * gsub + a, rs, :] = w[a] * act
            for a in range(gsub):
                ar = g * gsub + a
                p_ref[cs, ar * nkeys:(ar + 1) * nkeys] = pblk_ref[tb, ar].T.astype(BF16)
        o_ref[...] += _dot(p_ref[:, g * grows:(g + 1) * grows], v_ref[g * grows:(g + 1) * grows, :])


def _peer_dense(x, u_tab, v_tab, s_t, aux):
    n, d = x.shape
    ne = u_tab.shape[0]
    heads, _, nkeys, _ = s_t.shape
    tm = _pick(n, (1024, 512, 256, 128))
    te = _pick(ne, (512, 256, 128))
    assert te % nkeys == 0 and ne == nkeys * nkeys
    n_tiles = ne // te
    nsub = te // nkeys
    assert SUBLANES % nsub == 0 and nkeys % SUBLANES == 0 and nsub % PEER_SPLIT == 0
    kern = functools.partial(_peer_dense_kernel, heads=heads, nkeys=nkeys)
    once = pl.Buffered(1)
    return pl.pallas_call(
        kern,
        grid=(n // tm, n_tiles),
        in_specs=[pl.BlockSpec((tm, d), lambda i, e: (i, 0), pipeline_mode=once),
                  pl.BlockSpec((te, d), lambda i, e: (e, 0)),
                  pl.BlockSpec((te, d), lambda i, e: (e, 0)),
                  pl.BlockSpec((heads, None, SUBLANES, tm), lambda i, e: (0, 0, e * nsub // SUBLANES, i)),
                  pl.BlockSpec((heads, None, nkeys, tm), lambda i, e: (0, 1, 0, i), pipeline_mode=once),
                  pl.BlockSpec((heads, SUBLANES, tm), lambda i, e: (0, 0, i), pipeline_mode=once)],
        out_specs=pl.BlockSpec((tm, d), lambda i, e: (i, 0), pipeline_mode=once),
        out_shape=jax.ShapeDtypeStruct((n, d), F32),
        scratch_shapes=[pltpu.VMEM((heads, SUBLANES, tm), F32),
                        pltpu.VMEM((heads, SUBLANES, tm), F32),
                        pltpu.VMEM((heads, nkeys, tm), F32),
                        pltpu.VMEM((tm, te), BF16),
                        pltpu.VMEM((tm // LANES, nsub, nkeys, LANES), F32)],
        compiler_params=_params(("parallel", "arbitrary"), PEER_VMEM_LIMIT_MB),
        name="peer_dense",
    )(x, u_tab, v_tab, s_t, s_t, aux)


def _final_kernel(x_ref, p_ref, gp_ref, gs_ref, g_ref, yp_ref, ys_ref, *, nbp):
    i = pl.program_id(0)

    def body(gt_ref, y_ref):
        x = x_ref[...] + _mod_apply(p_ref[...], [gt_ref[...]], lambda p3, gt: gt * p3)
        y_ref[...] = x * lax.rsqrt(jnp.mean(x * x, -1, keepdims=True) + EPS) * g_ref[...]

    @pl.when(i < nbp)
    def _():
        body(gp_ref, yp_ref)

    @pl.when(i >= nbp)
    def _():
        body(gs_ref, ys_ref)


def _final(x, peer, mod, k_gate, g_final, rows):
    d = x.shape[1]
    tm = rows.tm
    gp, gs = rows.mod_specs(k_gate, d)
    return pl.pallas_call(
        functools.partial(_final_kernel, nbp=rows.nbp),
        grid=(rows.nb,),
        in_specs=[pl.BlockSpec((tm, d), lambda i: (i, 0)),
                  pl.BlockSpec((tm, d), lambda i: (i, 0)),
                  gp, gs,
                  pl.BlockSpec((1, d), lambda i: (0, 0))],
        out_specs=[pl.BlockSpec((tm, d), lambda i: (rows.p_tile(i), 0)),
                   pl.BlockSpec((tm, d), lambda i: (rows.s_tile(i), 0))],
        out_shape=[jax.ShapeDtypeStruct((rows.nbp * tm, d), F32),
                   jax.ShapeDtypeStruct((rows.nbs * tm, d), F32)],
        compiler_params=_params(("arbitrary",)),
        name="final_norm",
    )(x, peer, mod, mod, g_final.reshape(1, d))


def kernel(x_prompt, x_sample, cache_k, cache_v, state_C, state_n, state_m, state_conv, c_prompt, c_sample, w_ada, b_ada, g_mix, w_in, conv_w, conv_b, b_gates, m_norm_g, lam_q1, lam_k1, lam_q2, lam_k2, a_norm_g, w_a, w_b, w_out, g_ffn, w_pq, sub_keys, u_tab, v_tab, g_final):
    bp, tp, d = x_prompt.shape
    bs, ts, _ = x_sample.shape
    depth = w_in.shape[0]
    assert depth == 1, "the per-layer state plumbing below is written for a single layer"
    l = 0
    heads, dv = m_norm_g.shape[1:]
    dk = state_C.shape[-1]
    a_heads, a_dv = a_norm_g.shape[1:]
    dh = a_dv // 2
    n_p, n_s = bp * tp, bs * ts
    xp = x_prompt.reshape(n_p, d)
    xs = x_sample.reshape(n_s, d)
    rows_ew = _Rows(bp, tp, bs, ts, _pick(math.gcd(tp, n_s), (256, 128, 64, 32)))
    rows_mm = _Rows(bp, tp, bs, ts, _pick(math.gcd(tp, n_s), (1024, 512, 256, 128, 64, 32)))

    qk_w, v_w = 2 * heads * dk, heads * dv
    a_w = a_heads * a_dv
    col_mv = qk_w
    col_gate = qk_w + 2 * v_w
    col_aq = col_gate + 2 * heads
    col_ak = col_aq + a_w
    col_av = col_ak + a_w
    col_bg = col_av + a_w

    c_all = jnp.concatenate([c_sample, c_prompt], axis=0)
    c_all = jnp.pad(c_all, ((0, (-c_all.shape[0]) % SUBLANES), (0, 0)))
    mod = _ada(c_all, w_ada, b_ada, l).reshape(c_all.shape[0], 6, 1, d)

    hn = _norm_mod(xp, xs, 0, rows_ew, g_mix[l], mod, 1, 0)
    w_t = jnp.swapaxes(w_in, 1, 2)
    qk_raw, = _proj(hn, w_t, l, 0, qk_w, "f32", n_p)
    vo, = _proj(hn, w_t, l, col_mv, 2 * v_w, "bf16", n_p)
    aq, = _proj(hn, w_t, l, col_aq, a_w, "bf16", n_p, scale=LOG2E * dh ** -0.5)
    ak, k_p, k_s = _proj(hn, w_t, l, col_ak, a_w, "kv", n_p)
    av_t, v_p, v_s = _proj(hn, w_t, l, col_av, a_w, "kvt", n_p)
    bgates, = _proj(hn, w_t, l, col_bg, 2 * d, "bf16", n_p)
    gates = _gate_proj(hn, w_t, l, col_gate)

    zc = jnp.zeros((bp, heads, dv, dk), F32)
    zn = jnp.zeros((bp, heads, dk), F32)
    zm = jnp.zeros((bp, heads), F32)
    zconv = jnp.zeros((bp, CONV_W - 1, qk_w), F32)
    mp = _mlstm(qk_raw, vo, gates, 0, bp, tp, zconv, conv_w[l], conv_b[l], b_gates[l], m_norm_g[l], zc, zn, zm)
    ms = _mlstm(qk_raw, vo, gates, n_p, bs, ts, state_conv[l], conv_w[l], conv_b[l], b_gates[l], m_norm_g[l],
                state_C[l], state_n[l], state_m[l])

    lam_init = 0.8 - 0.6 * math.exp(-0.3 * l)
    slopes = jnp.array([2.0 ** (-8.0 * (h + 1) / a_heads) for h in range(a_heads)], F32)
    lams = [a[l].reshape(1, -1) for a in (lam_q1, lam_k1, lam_q2, lam_k2)]
    ha_p = _attn_prompt(aq, ak, av_t, bp, tp, slopes, lams, a_norm_g[l], lam_init)
    ha_s = _attn_sample(aq, k_s, v_s, n_p, bs, ts, cache_k, cache_v, l, slopes, lams, a_norm_g[l], lam_init)

    merged = _merge(mp[0], ms[0], ha_p, ha_s, _cast_bf16(w_a, l), _cast_bf16(w_b, l), bgates, rows_mm)
    x1 = _outproj(merged, _cast_bf16(w_out, l), xp, xs, mod, 2, rows_mm)

    hn2 = _norm_mod(x1, x1, rows_ew.nbp, rows_ew, g_ffn[l], mod, 4, 3)
    qp = _matmul(hn2, _cast_bf16(w_pq, l), F32, "peer_query")
    s_t, aux = _peer_select(qp, sub_keys, l)
    peer = _peer_dense(hn2, _cast_bf16(u_tab, l), _cast_bf16(v_tab, l), s_t, aux)
    y_p, y_s = _final(x1, peer, mod, 5, g_final, rows_ew)

    return (y_p.reshape(bp, tp, d), y_s.reshape(bs, ts, d),
            k_p.reshape(1, bp, tp, a_heads, 2, dh), v_p.reshape(1, bp, tp, a_heads, a_dv),
            mp[1][None], mp[2][None], mp[3][None], mp[4][None],
            k_s.reshape(1, bs, ts, a_heads, 2, dh), v_s.reshape(1, bs, ts, a_heads, a_dv),
            ms[1][None], ms[2][None], ms[3][None], ms[4][None])
```
